```python
import math
import jax, jax.numpy as jnp
from jax import lax
import numpy as np

D_MODEL = 1024
BATCH = 4
SEQ = 4096
DEPTH = 1
DEC_BATCH = 32
DEC_SEQ = 8
PAST_LEN = 8192
PAGE_SIZE = 128

MIX_WIDTH = D_MODEL
ATTN_WIDTH = MIX_WIDTH // 2
LRU_WIDTH = MIX_WIDTH - ATTN_WIDTH
HEAD_DIM = 64
N_HEADS = ATTN_WIDTH // HEAD_DIM
LRU_BLOCKS = 8
LRU_BLOCK_DIM = LRU_WIDTH // LRU_BLOCKS
CONV_W = 4
LRU_C = 8.0
MOBA_BLOCK = 256
MOBA_TOPK = 3
Q_CHUNK = 64
ROPE_THETA = 10000.0
N_EXPERTS = 64
TOP_K = 8
N_GROUPS = 8
TOPK_GROUPS = 4
D_EXPERT = D_MODEL // 4
D_SHARED = D_EXPERT
ROUTED_SCALE = 2.5
EPS = 1e-6
IN_COLS = 3 * ATTN_WIDTH + 2 * LRU_WIDTH

kernel_name = 'moba_rglru_moe_hybrid_step'

F32 = jnp.float32


def _rmsnorm(x, g):
    xf = x.astype(F32)
    y = xf * lax.rsqrt(jnp.mean(xf * xf, axis=-1, keepdims=True) + EPS)
    return (y * g.astype(F32)).astype(x.dtype)


def _rope(x, pos):
    half = HEAD_DIM // 2
    inv_freq = ROPE_THETA ** (-jnp.arange(half, dtype=F32) / half)
    ang = pos.astype(F32)[:, None] * inv_freq[None, :]
    cos = jnp.cos(ang)[None, :, None, :]
    sin = jnp.sin(ang)[None, :, None, :]
    xf = x.astype(F32)
    x1, x2 = xf[..., :half], xf[..., half:]
    out = jnp.concatenate([x1 * cos - x2 * sin, x2 * cos + x1 * sin], axis=-1)
    return out.astype(x.dtype)


def _moba_chunk(q, qpos, k_blk, v_blk, k_mean):
    B, C, H, D = q.shape
    NB = k_blk.shape[2]
    n_sel = min(MOBA_TOPK, NB)
    qf = q.astype(F32)
    gate = jnp.einsum('bchd,bhnd->bchn', qf, k_mean)
    cur = qpos // MOBA_BLOCK
    past = jnp.arange(NB)[None, :] < cur[:, None]
    gate = jnp.where(past[None, :, None, :], gate, -jnp.inf)
    _, top_idx = lax.top_k(gate, n_sel)
    own = jnp.broadcast_to(cur[None, :, None, None], (B, C, H, 1)).astype(top_idx.dtype)
    idx = jnp.concatenate([top_idx, own], axis=-1)
    bi = jnp.arange(B)[:, None, None, None]
    hi = jnp.arange(H)[None, None, :, None]
    kg = k_blk[bi, hi, idx].astype(F32)
    vg = v_blk[bi, hi, idx].astype(F32)
    s = jnp.einsum('bchd,bchnkd->bchnk', qf, kg) * (HEAD_DIM ** -0.5)
    key_pos = idx[..., None] * MOBA_BLOCK + jnp.arange(MOBA_BLOCK)
    sel_ok = jnp.concatenate([jnp.arange(n_sel)[None, :] < cur[:, None],
                              jnp.ones((C, 1), dtype=bool)], axis=-1)
    valid = sel_ok[None, :, None, :, None] & (key_pos <= qpos[None, :, None, None, None])
    s = jnp.where(valid, s, -jnp.inf)
    p = jax.nn.softmax(s.reshape(B, C, H, -1), axis=-1).reshape(s.shape)
    o = jnp.einsum('bchnk,bchnkd->bchd', p, vg)
    return o.astype(q.dtype)


def _moba_attend(q, k_all, v_all, qpos):
    B, S, H, D = q.shape
    T = k_all.shape[1]
    NB = -(-T // MOBA_BLOCK)
    pad = NB * MOBA_BLOCK - T
    widths = ((0, 0), (0, pad), (0, 0), (0, 0))
    k_blk = jnp.pad(k_all, widths).reshape(B, NB, MOBA_BLOCK, H, D).transpose(0, 3, 1, 2, 4)
    v_blk = jnp.pad(v_all, widths).reshape(B, NB, MOBA_BLOCK, H, D).transpose(0, 3, 1, 2, 4)
    k_mean = jnp.mean(k_blk.astype(F32), axis=3)
    chunk = Q_CHUNK if S % Q_CHUNK == 0 else S
    n_chunks = S // chunk
    qc = q.reshape(B, n_chunks, chunk, H, D).transpose(1, 0, 2, 3, 4)
    pc = qpos.reshape(n_chunks, chunk)
    out = lax.map(lambda qp: _moba_chunk(qp[0], qp[1], k_blk, v_blk, k_mean), (qc, pc))
    return out.transpose(1, 0, 2, 3, 4).reshape(B, S, H * D)


def _lru_combine(e1, e2):
    a1, b1 = e1
    a2, b2 = e2
    return a1 * a2, a2 * b1 + b2


def _rg_lru(xb, h0, conv_buf, start_pos, conv_w, conv_b, wa, ba, wx, bx, lam):
    B, S, W = xb.shape
    xp = jnp.concatenate([conv_buf.astype(xb.dtype), xb], axis=1)
    conv = conv_b + conv_w[0] * xp[:, 0:S]
    for j in range(1, CONV_W):
        conv = conv + conv_w[j] * xp[:, j:j + S]
    new_buf = xp[:, S:]
    cf = conv.astype(F32)
    xc = cf.reshape(B, S, LRU_BLOCKS, LRU_BLOCK_DIM)
    r = jax.nn.sigmoid(jnp.einsum('bsnd,nde->bsne', xc, wa.astype(F32)).reshape(B, S, W) + ba.astype(F32))
    i = jax.nn.sigmoid(jnp.einsum('bsnd,nde->bsne', xc, wx.astype(F32)).reshape(B, S, W) + bx.astype(F32))
    log_a = -LRU_C * r * jax.nn.softplus(-lam.astype(F32))
    pos = start_pos + jnp.arange(S, dtype=jnp.int32)
    reset = (pos == 0)[None, :, None]
    a = jnp.where(reset, 0.0, jnp.exp(log_a))
    mult = jnp.where(reset, 1.0, jnp.sqrt(-jnp.expm1(2.0 * log_a)))
    b = mult * i * cf
    a_cum, b_cum = lax.associative_scan(_lru_combine, (a, b), axis=1)
    h = a_cum * h0.astype(F32)[:, None, :] + b_cum
    return h.astype(xb.dtype), h[:, -1].astype(h0.dtype), new_buf.astype(conv_buf.dtype)


def _swiglu(x, w_gu, w_down):
    g, u = jnp.split(jnp.dot(x, w_gu), 2, axis=-1)
    return jnp.dot(jax.nn.silu(g) * u, w_down)


def _moe(x, router_w, router_bias, w_gu, w_down, ws_gu, ws_down):
    T = x.shape[0]
    scores = jax.nn.sigmoid(jnp.dot(x.astype(F32), router_w.astype(F32)))
    biased = scores + router_bias.astype(F32)
    grp = biased.reshape(T, N_GROUPS, N_EXPERTS // N_GROUPS)
    gscore = jnp.sum(lax.top_k(grp, 2)[0], axis=-1)
    _, gidx = lax.top_k(gscore, TOPK_GROUPS)
    gmask = jnp.sum(jax.nn.one_hot(gidx, N_GROUPS, dtype=F32), axis=-2) > 0
    emask = jnp.repeat(gmask, N_EXPERTS // N_GROUPS, axis=-1)
    _, eidx = lax.top_k(jnp.where(emask, biased, -jnp.inf), TOP_K)
    w = jnp.take_along_axis(scores, eidx, axis=-1)
    w = ROUTED_SCALE * w / jnp.sum(w, axis=-1, keepdims=True)
    gates = jnp.einsum('tk,tke->te', w, jax.nn.one_hot(eidx, N_EXPERTS, dtype=F32)).astype(x.dtype)
    out = _swiglu(x, ws_gu, ws_down)
    for e in range(N_EXPERTS):
        out = out + gates[:, e:e + 1] * _swiglu(x, w_gu[e], w_down[e])
    return out


def _layer(x, c, past_k, past_v, h0, conv0, start_pos,
           ada_w, ada_b, norm_mix_g, w_in, conv_w, conv_b, gate_a_w, gate_a_b,
           gate_x_w, gate_x_b, lru_lambda, attn_out_g, lru_out_g, w_out, norm_ffn_g,
           router_w, router_bias, exp_w_gu, exp_w_down, shared_w_gu, shared_w_down):
    B, S, _ = x.shape
    mod = jnp.dot(jax.nn.silu(c), ada_w) + ada_b
    shift_m, scale_m, gate_m, shift_f, scale_f, gate_f = [m[:, None, :] for m in jnp.split(mod, 6, axis=-1)]
    h = _rmsnorm(x, norm_mix_g) * (1.0 + scale_m) + shift_m
    proj = jnp.dot(h, w_in)
    q, k, v, xl, gl = jnp.split(proj, [ATTN_WIDTH, 2 * ATTN_WIDTH, 3 * ATTN_WIDTH,
                                       3 * ATTN_WIDTH + LRU_WIDTH], axis=-1)
    pos = start_pos + jnp.arange(S, dtype=jnp.int32)
    q = _rope(q.reshape(B, S, N_HEADS, HEAD_DIM), pos)
    k = _rope(k.reshape(B, S, N_HEADS, HEAD_DIM), pos)
    v = v.reshape(B, S, N_HEADS, HEAD_DIM)
    k_all = jnp.concatenate([past_k.astype(k.dtype), k], axis=1)
    v_all = jnp.concatenate([past_v.astype(v.dtype), v], axis=1)
    attn = _moba_attend(q, k_all, v_all, pos)
    lru, h_last, new_buf = _rg_lru(xl, h0, conv0, start_pos, conv_w, conv_b,
                                   gate_a_w, gate_a_b, gate_x_w, gate_x_b, lru_lambda)
    lru = lru * jax.nn.gelu(gl)
    mixed = jnp.dot(jnp.concatenate([_rmsnorm(attn, attn_out_g), _rmsnorm(lru, lru_out_g)], axis=-1), w_out)
    x = x + gate_m * mixed
    h2 = _rmsnorm(x, norm_ffn_g) * (1.0 + scale_f) + shift_f
    f = _moe(h2.reshape(B * S, D_MODEL), router_w, router_bias, exp_w_gu, exp_w_down,
             shared_w_gu, shared_w_down).reshape(B, S, D_MODEL)
    x = x + gate_f * f
    return x, k, v, h_last, new_buf


def setup_inputs(seed: int = 0) -> dict:
    key = jax.random.key(seed)
    ks = jax.random.split(key, 40)
    n_pages = PAST_LEN // PAGE_SIZE
    n_pool = (DEC_BATCH * n_pages * 5) // 4
    nrm = jax.random.normal
    page_table = jax.random.permutation(ks[9], n_pool)[:DEC_BATCH * n_pages]
    page_table = page_table.reshape(DEC_BATCH, n_pages).astype(jnp.int32)
    a_base = jax.random.uniform(ks[18], (DEPTH, LRU_WIDTH), F32, minval=0.9, maxval=0.999)
    return {
        'x_prompt': nrm(ks[0], (BATCH, SEQ, D_MODEL), F32),
        'x_sample': nrm(ks[1], (DEC_BATCH, DEC_SEQ, D_MODEL), F32),
        'c_prompt': nrm(ks[2], (BATCH, D_MODEL), F32),
        'c_sample': nrm(ks[3], (DEC_BATCH, D_MODEL), F32),
        'cache_k': nrm(ks[4], (DEPTH, n_pool, PAGE_SIZE, N_HEADS, HEAD_DIM), F32),
        'cache_v': nrm(ks[5], (DEPTH, n_pool, PAGE_SIZE, N_HEADS, HEAD_DIM), F32),
        'state_h': 0.5 * nrm(ks[6], (DEPTH, DEC_BATCH, LRU_WIDTH), F32),
        'state_conv': nrm(ks[7], (DEPTH, DEC_BATCH, CONV_W - 1, LRU_WIDTH), F32),
        'page_table': page_table,
        'ada_w': 0.2 * D_MODEL ** -0.5 * nrm(ks[10], (DEPTH, D_MODEL, 6 * D_MODEL), F32),
        'ada_b': 0.02 * nrm(ks[11], (DEPTH, 6 * D_MODEL), F32),
        'norm_mix_g': 1.0 + 0.02 * nrm(ks[12], (DEPTH, D_MODEL), F32),
        'w_in': D_MODEL ** -0.5 * nrm(ks[13], (DEPTH, D_MODEL, IN_COLS), F32),
        'conv_w': CONV_W ** -0.5 * nrm(ks[14], (DEPTH, CONV_W, LRU_WIDTH), F32),
        'conv_b': 0.02 * nrm(ks[15], (DEPTH, LRU_WIDTH), F32),
        'gate_a_w': LRU_BLOCK_DIM ** -0.5 * nrm(ks[16], (DEPTH, LRU_BLOCKS, LRU_BLOCK_DIM, LRU_BLOCK_DIM), F32),
        'gate_a_b': 0.02 * nrm(ks[17], (DEPTH, LRU_WIDTH), F32),
        'gate_x_w': LRU_BLOCK_DIM ** -0.5 * nrm(ks[19], (DEPTH, LRU_BLOCKS, LRU_BLOCK_DIM, LRU_BLOCK_DIM), F32),
        'gate_x_b': 0.02 * nrm(ks[20], (DEPTH, LRU_WIDTH), F32),
        'lru_lambda': jnp.log(a_base) - jnp.log1p(-a_base),
        'attn_out_g': 1.0 + 0.02 * nrm(ks[21], (DEPTH, ATTN_WIDTH), F32),
        'lru_out_g': 1.0 + 0.02 * nrm(ks[22], (DEPTH, LRU_WIDTH), F32),
        'w_out': MIX_WIDTH ** -0.5 * nrm(ks[23], (DEPTH, MIX_WIDTH, D_MODEL), F32),
        'norm_ffn_g': 1.0 + 0.02 * nrm(ks[24], (DEPTH, D_MODEL), F32),
        'router_w': D_MODEL ** -0.5 * nrm(ks[25], (DEPTH, D_MODEL, N_EXPERTS), F32),
        'router_bias': 0.01 * nrm(ks[26], (DEPTH, N_EXPERTS), F32),
        'exp_w_gu': D_MODEL ** -0.5 * nrm(ks[27], (DEPTH, N_EXPERTS, D_MODEL, 2 * D_EXPERT), F32),
        'exp_w_down': D_EXPERT ** -0.5 * nrm(ks[28], (DEPTH, N_EXPERTS, D_EXPERT, D_MODEL), F32),
        'shared_w_gu': D_MODEL ** -0.5 * nrm(ks[29], (DEPTH, D_MODEL, 2 * D_SHARED), F32),
        'shared_w_down': D_SHARED ** -0.5 * nrm(ks[30], (DEPTH, D_SHARED, D_MODEL), F32),
        'final_g': 1.0 + 0.02 * nrm(ks[31], (D_MODEL,), F32),
    }


def reference(x_prompt, x_sample, c_prompt, c_sample, cache_k, cache_v, state_h, state_conv,
              page_table, ada_w, ada_b, norm_mix_g, w_in, conv_w, conv_b, gate_a_w, gate_a_b,
              gate_x_w, gate_x_b, lru_lambda, attn_out_g, lru_out_g, w_out, norm_ffn_g,
              router_w, router_bias, exp_w_gu, exp_w_down, shared_w_gu, shared_w_down, final_g):
    n_pages = page_table.shape[1]
    past_len = n_pages * PAGE_SIZE
    n_dec = x_sample.shape[0]
    n_prm = x_prompt.shape[0]
    yp, ys = x_prompt, x_sample
    kp_l, vp_l, hp_l, cp_l, ks_l, vs_l, hs_l, cs_l = [], [], [], [], [], [], [], []
    for l in range(DEPTH):
        w = (ada_w[l], ada_b[l], norm_mix_g[l], w_in[l], conv_w[l], conv_b[l], gate_a_w[l],
             gate_a_b[l], gate_x_w[l], gate_x_b[l], lru_lambda[l], attn_out_g[l], lru_out_g[l],
             w_out[l], norm_ffn_g[l], router_w[l], router_bias[l], exp_w_gu[l], exp_w_down[l],
             shared_w_gu[l], shared_w_down[l])
        no_kv = jnp.zeros((n_prm, 0, N_HEADS, HEAD_DIM), yp.dtype)
        h0 = jnp.zeros((n_prm, LRU_WIDTH), yp.dtype)
        c0 = jnp.zeros((n_prm, CONV_W - 1, LRU_WIDTH), yp.dtype)
        yp, k1, v1, h1, c1 = _layer(yp, c_prompt, no_kv, no_kv, h0, c0, 0, *w)
        past_k = cache_k[l][page_table].reshape(n_dec, past_len, N_HEADS, HEAD_DIM)
        past_v = cache_v[l][page_table].reshape(n_dec, past_len, N_HEADS, HEAD_DIM)
        ys, k2, v2, h2, c2 = _layer(ys, c_sample, past_k, past_v, state_h[l], state_conv[l], past_len, *w)
        kp_l.append(k1); vp_l.append(v1); hp_l.append(h1); cp_l.append(c1)
        ks_l.append(k2); vs_l.append(v2); hs_l.append(h2); cs_l.append(c2)
    y_prompt = _rmsnorm(yp, final_g)
    y_sample = _rmsnorm(ys, final_g)
    k_prompt = jnp.stack(kp_l)
    v_prompt = jnp.stack(vp_l)
    h_prompt = jnp.stack(hp_l)
    conv_prompt = jnp.stack(cp_l)
    k_sample = jnp.stack(ks_l)
    v_sample = jnp.stack(vs_l)
    h_sample = jnp.stack(hs_l)
    conv_sample = jnp.stack(cs_l)
    return (y_prompt, y_sample, k_prompt, v_prompt, h_prompt, conv_prompt,
            k_sample, v_sample, h_sample, conv_sample)
```

```python
import functools
import math

import jax
import jax.numpy as jnp
from jax import lax
from jax.experimental import pallas as pl
from jax.experimental.pallas import tpu as pltpu

F32 = jnp.float32
BF16 = jnp.bfloat16

HEAD_DIM = 64
LRU_BLOCKS = 8
CONV_W = 4
LRU_C = 8.0
MOBA_BLOCK = 256
MOBA_TOPK = 3
ROPE_THETA = 10000.0
N_GROUPS = 8
TOPK_GROUPS = 4
TOP_K = 8
ROUTED_SCALE = 2.5
EPS = 1e-6
PAGE_SIZE = 128

LANES = 128
SUBLANES = 8
NEG = -1e30
VMEM_LIMIT = 56 * 1024 * 1024

NN = (((1,), (0,)), ((), ()))
NT = (((1,), (1,)), ((), ()))


def _cparams(sem):
    return pltpu.CompilerParams(dimension_semantics=sem, vmem_limit_bytes=VMEM_LIMIT)


def _split(x):
    hi = x.astype(BF16)
    lo = (x - hi.astype(F32)).astype(BF16)
    return hi, lo


def _dot3(a, b, dims):
    ah, al = _split(a)
    bh, bl = _split(b)
    d = lambda u, v: lax.dot_general(u, v, dims, preferred_element_type=F32)
    return d(ah, bh) + (d(ah, bl) + d(al, bh))


def _dot(a, b, dims=NN):
    return lax.dot_general(a, b, dims, preferred_element_type=F32)


def _rms(x, g):
    return x * lax.rsqrt(jnp.mean(x * x, axis=-1, keepdims=True) + EPS) * g


def _silu(x):
    return x * jax.nn.sigmoid(x)


def _gelu_tanh(x):
    c = math.sqrt(2.0 / math.pi)
    return 0.5 * x * (1.0 + jnp.tanh(c * (x + 0.044715 * (x * x * x))))


def _ada_kernel(c_ref, w_ref, b_ref, o_ref):
    o_ref[...] = _dot3(_silu(c_ref[...]), w_ref[...], NN) + b_ref[...]


def _ada_mod(c, ada_w, ada_b):
    r, d = c.shape
    n = ada_w.shape[1]
    tn = 1536
    return pl.pallas_call(
        _ada_kernel,
        grid=(n // tn,),
        in_specs=[pl.BlockSpec((r, d), lambda j: (0, 0)),
                  pl.BlockSpec((d, tn), lambda j: (0, j)),
                  pl.BlockSpec((1, tn), lambda j: (0, j))],
        out_specs=pl.BlockSpec((r, tn), lambda j: (0, j)),
        out_shape=jax.ShapeDtypeStruct((r, n), F32),
        compiler_params=_cparams(("parallel",)),
        name="ada_mod",
    )(c, ada_w, ada_b.reshape(1, n))


def _inproj_kernel(x_ref, shift_ref, scale_ref, cos_ref, sin_ref, g_ref, w_ref,
                   q_ref, k_ref, v_ref, xl_ref, gl_ref, *, aw):
    x = x_ref[0]
    h = _rms(x, g_ref[...]) * (1.0 + scale_ref[0]) + shift_ref[0]
    proj = _dot(h.astype(BF16), w_ref[...])
    cos = cos_ref[...]
    sin = sin_ref[...]
    lane = lax.broadcasted_iota(jnp.int32, cos.shape, 1)
    first = (lane % HEAD_DIM) < (HEAD_DIM // 2)

    def rope(t):
        outs = []
        for c in range(aw // LANES):
            tc = t[:, LANES * c:LANES * (c + 1)]
            partner = jnp.where(first, pltpu.roll(tc, LANES - HEAD_DIM // 2, 1),
                                pltpu.roll(tc, HEAD_DIM // 2, 1))
            outs.append(tc * cos + partner * sin)
        return jnp.concatenate(outs, axis=1)

    q_ref[0] = rope(proj[:, 0:aw])
    k_ref[0] = rope(proj[:, aw:2 * aw])
    v_ref[0] = proj[:, 2 * aw:3 * aw]
    lw = (proj.shape[1] - 3 * aw) // 2
    xl_ref[0] = proj[:, 3 * aw:3 * aw + lw]
    gl_ref[0] = proj[:, 3 * aw + lw:]


def _mod_spec(mod, ts):
    if mod.shape[1] == 1:
        return pl.BlockSpec((1, 1, mod.shape[2]), lambda b, i: (b, 0, 0))
    return pl.BlockSpec((1, ts, mod.shape[2]), lambda b, i: (b, i, 0))


def _inproj(x, shift, scale, cos, sin, g, w_bf, ts):
    nb, s, d = x.shape
    aw = 512
    lw = (w_bf.shape[1] - 3 * aw) // 2
    tok = lambda width: pl.BlockSpec((1, ts, width), lambda b, i: (b, i, 0))
    shp = lambda width: jax.ShapeDtypeStruct((nb, s, width), F32)
    return pl.pallas_call(
        functools.partial(_inproj_kernel, aw=aw),
        grid=(nb, s // ts),
        in_specs=[tok(d), _mod_spec(shift, ts), _mod_spec(scale, ts),
                  pl.BlockSpec((ts, LANES), lambda b, i: (i, 0)),
                  pl.BlockSpec((ts, LANES), lambda b, i: (i, 0)),
                  pl.BlockSpec((1, d), lambda b, i: (0, 0)),
                  pl.BlockSpec(w_bf.shape, lambda b, i: (0, 0))],
        out_specs=[tok(aw), tok(aw), tok(aw), tok(lw), tok(lw)],
        out_shape=[shp(aw), shp(aw), shp(aw), shp(lw), shp(lw)],
        compiler_params=_cparams(("parallel", "parallel")),
        name="inproj",
    )(x, shift, scale, cos, sin, g, w_bf)


def _moba_select_bias(km, qa, cur, nblk, tq):
    gt = _dot3(km, qa, NT)[0:nblk, :]
    n_iota = lax.broadcasted_iota(jnp.int32, (nblk, tq), 0)
    past = n_iota < cur
    gm = jnp.where(past, gt, -jnp.inf)
    rank = jnp.zeros((nblk, tq), jnp.int32)
    for m in range(nblk):
        row = gm[m:m + 1, :]
        beats = (row > gm) | ((row == gm) & (m < n_iota))
        rank = rank + beats.astype(jnp.int32)
    selb = jnp.where(past & (rank < MOBA_TOPK), 0.0, NEG)
    selb = jnp.concatenate([selb, jnp.full((LANES - nblk, tq), NEG, F32)], axis=0)
    return selb.T


def _moba_prompt_kernel(q_ref, k_ref, v_ref, o_ref, kbf_ref, vbf_ref, km_ref, *, tq):
    s_len = k_ref.shape[1]
    nblk = s_len // MOBA_BLOCK
    kbf_ref[...] = k_ref[0].astype(BF16)
    vbf_ref[...] = v_ref[0].astype(BF16)
    km_ref[...] = jnp.zeros(km_ref.shape, F32)
    for n in range(nblk):
        km_ref[n:n + 1, :] = jnp.mean(k_ref[0, n * MOBA_BLOCK:(n + 1) * MOBA_BLOCK, :],
                                      axis=0, keepdims=True)
    lane = lax.broadcasted_iota(jnp.int32, (tq, LANES), 1)
    qrow = lax.broadcasted_iota(jnp.int32, (tq, MOBA_BLOCK), 0)
    kcol = lax.broadcasted_iota(jnp.int32, (tq, MOBA_BLOCK), 1)

    def qtile(t, carry):
        r0 = pl.multiple_of(t * tq, tq)
        cur = r0 // MOBA_BLOCK
        c0 = pl.multiple_of(cur * MOBA_BLOCK, MOBA_BLOCK)
        q = q_ref[0, pl.ds(r0, tq), :]
        km = km_ref[...]
        kd = kbf_ref[pl.ds(c0, MOBA_BLOCK), :]
        vd = vbf_ref[pl.ds(c0, MOBA_BLOCK), :]
        causal = (c0 + kcol) <= (r0 + qrow)
        outs = []
        for hh in range(LANES // HEAD_DIM):
            qa = jnp.where((lane // HEAD_DIM) == hh, q, 0.0)
            selb = _moba_select_bias(km, qa, cur, nblk, tq)
            qb = (qa * (HEAD_DIM ** -0.5)).astype(BF16)
            s = jnp.where(causal, _dot(qb, kd, NT), NEG)
            m = jnp.max(s, axis=1, keepdims=True)
            p = jnp.exp(s - m)
            l = jnp.sum(p, axis=1, keepdims=True)
            acc = _dot(p.astype(BF16), vd)

            def body(j, c):
                m, l, acc = c
                j0 = pl.multiple_of(j * MOBA_BLOCK, MOBA_BLOCK)
                col = jnp.sum(jnp.where(lane == j, selb, 0.0), axis=1, keepdims=True)
                s = _dot(qb, kbf_ref[pl.ds(j0, MOBA_BLOCK), :], NT) + col
                m_new = jnp.maximum(m, jnp.max(s, axis=1, keepdims=True))
                alpha = jnp.exp(m - m_new)
                p = jnp.exp(s - m_new)
                l = alpha * l + jnp.sum(p, axis=1, keepdims=True)
                acc = alpha * acc + _dot(p.astype(BF16), vbf_ref[pl.ds(j0, MOBA_BLOCK), :])
                return m_new, l, acc

            m, l, acc = lax.fori_loop(0, cur, body, (m, l, acc))
            outs.append(acc / l)
        o_ref[0, pl.ds(r0, tq), :] = jnp.where((lane // HEAD_DIM) == 0, outs[0], outs[1])
        return carry

    lax.fori_loop(0, s_len // tq, qtile, 0)


def _moba_prompt(q, k, v):
    nb, s, aw = q.shape
    spec = pl.BlockSpec((1, s, LANES), lambda b, h: (b, 0, h))
    return pl.pallas_call(
        functools.partial(_moba_prompt_kernel, tq=128),
        grid=(nb, aw // LANES),
        in_specs=[spec, spec, spec],
        out_specs=spec,
        out_shape=jax.ShapeDtypeStruct((nb, s, aw), F32),
        scratch_shapes=[pltpu.VMEM((s, LANES), BF16), pltpu.VMEM((s, LANES), BF16),
                        pltpu.VMEM((LANES, LANES), F32)],
        compiler_params=_cparams(("parallel", "parallel")),
        name="moba_prompt",
    )(q, k, v)


def _moba_sample_kernel(pt_ref, q_ref, kn_ref, vn_ref, k0_ref, k1_ref, v0_ref, v1_ref, o_ref,
                        qbd_ref, km_ref, m_ref, l_ref, o_scr, *, n_heads, n_q):
    j = pl.program_id(1)
    nblk = pl.num_programs(1)
    rows = n_heads * n_q
    aw = q_ref.shape[2]
    rowi = lax.broadcasted_iota(jnp.int32, (rows, aw), 0)
    coli = lax.broadcasted_iota(jnp.int32, (rows, aw), 1)
    diag = (rowi // n_q) == (coli // HEAD_DIM)
    lane = lax.broadcasted_iota(jnp.int32, (rows, LANES), 1)

    @pl.when(j == 0)
    def _():
        qt = jnp.concatenate([q_ref[0]] * n_heads, axis=0)
        qbd_ref[...] = jnp.where(diag, qt, 0.0)
        km_ref[...] = jnp.zeros(km_ref.shape, F32)
        m_ref[...] = jnp.full(m_ref.shape, NEG, F32)
        l_ref[...] = jnp.zeros(l_ref.shape, F32)

    qbd = qbd_ref[...]
    qb = (qbd * (HEAD_DIM ** -0.5)).astype(BF16)
    kblk = jnp.concatenate([k0_ref[0], k1_ref[0]], axis=0)
    vblk = jnp.concatenate([v0_ref[0], v1_ref[0]], axis=0)
    km_ref[pl.ds(j, 1), :] = jnp.mean(kblk, axis=0, keepdims=True)
    s = _dot(qb, kblk.astype(BF16), NT)
    mj = jnp.max(s, axis=1, keepdims=True)
    p = jnp.exp(s - mj)
    lj = jnp.sum(p, axis=1, keepdims=True)
    o_scr[j] = _dot(p.astype(BF16), vblk.astype(BF16))
    m_ref[...] = jnp.where(lane == j, mj, m_ref[...])
    l_ref[...] = jnp.where(lane == j, lj, l_ref[...])

    @pl.when(j == nblk - 1)
    def _():
        npast = o_scr.shape[0]
        gate = _dot3(qbd, km_ref[...], NT)
        valid = lane < npast
        gm = jnp.where(valid, gate, -jnp.inf)
        rank = jnp.zeros((rows, LANES), jnp.int32)
        for mm in range(npast):
            colv = gm[:, mm:mm + 1]
            beats = (colv > gm) | ((colv == gm) & (mm < lane))
            rank = rank + beats.astype(jnp.int32)
        sel = valid & (rank < MOBA_TOPK)
        so = _dot(qb, kn_ref[0].astype(BF16), NT)
        qi = lax.broadcasted_iota(jnp.int32, (rows, n_q), 0) % n_q
        ki = lax.broadcasted_iota(jnp.int32, (rows, n_q), 1)
        so = jnp.where(ki <= qi, so, NEG)
        mo = jnp.max(so, axis=1, keepdims=True)
        po = jnp.exp(so - mo)
        lo = jnp.sum(po, axis=1, keepdims=True)
        oo = _dot(po.astype(BF16), vn_ref[0].astype(BF16))
        mall = jnp.where(sel, m_ref[...], NEG)
        mx = jnp.maximum(jnp.max(mall, axis=1, keepdims=True), mo)
        w = jnp.where(sel, jnp.exp(mall - mx), 0.0)
        wo = jnp.exp(mo - mx)
        den = jnp.sum(w * l_ref[...], axis=1, keepdims=True) + wo * lo
        num = wo * oo
        for jj in range(npast):
            num = num + w[:, jj:jj + 1] * o_scr[jj]
        res = jnp.where(diag, num / den, 0.0)
        out = res[0:n_q, :]
        for hh in range(1, n_heads):
            out = out + res[hh * n_q:(hh + 1) * n_q, :]
        o_ref[0] = out


def _moba_sample(q, k_new, v_new, cache_k, cache_v, page_table):
    nb, n_q, aw = q.shape
    n_heads = aw // HEAD_DIM
    n_pages = page_table.shape[1]
    ppb = MOBA_BLOCK // PAGE_SIZE
    npast = n_pages // ppb
    rows = n_heads * n_q
    tok = pl.BlockSpec((1, n_q, aw), lambda b, j, pt: (b, 0, 0))
    page = lambda o: pl.BlockSpec((1, PAGE_SIZE, aw), lambda b, j, pt: (pt[b, ppb * j + o], 0, 0))
    grid_spec = pltpu.PrefetchScalarGridSpec(
        num_scalar_prefetch=1,
        grid=(nb, npast),
        in_specs=[tok, tok, tok, page(0), page(1), page(0), page(1)],
        out_specs=tok,
        scratch_shapes=[pltpu.VMEM((rows, aw), F32), pltpu.VMEM((LANES, aw), F32),
                        pltpu.VMEM((rows, LANES), F32), pltpu.VMEM((rows, LANES), F32),
                        pltpu.VMEM((npast, rows, aw), F32)],
    )
    return pl.pallas_call(
        functools.partial(_moba_sample_kernel, n_heads=n_heads, n_q=n_q),
        grid_spec=grid_spec,
        out_shape=jax.ShapeDtypeStruct((nb, n_q, aw), F32),
        compiler_params=_cparams(("parallel", "arbitrary")),
        name="moba_sample",
    )(page_table, q, k_new, v_new, cache_k, cache_k, cache_v, cache_v)


def _lru_kernel(xl_ref, gl_ref, h0_ref, c0_ref, cw_ref, cb_ref, wa_ref, ba_ref, wx_ref, bx_ref,
                lam_ref, y_ref, hlast_ref, cbuf_ref, xbuf_ref, a_ref, b_ref, hcar_ref,
                *, ts, start_pos):
    i = pl.program_id(1)
    w = xl_ref.shape[2]
    hist = SUBLANES

    @pl.when(i == 0)
    def _():
        xbuf_ref[0:hist, :] = jnp.zeros((hist, w), F32)
        xbuf_ref[hist - (CONV_W - 1):hist, :] = c0_ref[0]
        hcar_ref[...] = jnp.broadcast_to(h0_ref[0], (SUBLANES, w))

    x = xl_ref[0]
    xbuf_ref[hist:hist + ts, :] = x
    conv = cb_ref[...] + cw_ref[0:1, :] * xbuf_ref[hist - 3:hist - 3 + ts, :]
    conv = conv + cw_ref[1:2, :] * xbuf_ref[hist - 2:hist - 2 + ts, :]
    conv = conv + cw_ref[2:3, :] * xbuf_ref[hist - 1:hist - 1 + ts, :]
    conv = conv + cw_ref[3:4, :] * x
    cbuf_ref[0] = xbuf_ref[hist + ts - (CONV_W - 1):hist + ts, :]
    xbuf_ref[0:hist, :] = xbuf_ref[ts:ts + hist, :]

    cbf = conv.astype(BF16)
    r = jax.nn.sigmoid(_dot(cbf, wa_ref[...]) + ba_ref[...])
    ig = jax.nn.sigmoid(_dot(cbf, wx_ref[...]) + bx_ref[...])
    nl = -lam_ref[...]
    softplus = jnp.maximum(nl, 0.0) + jnp.log1p(jnp.exp(-jnp.abs(nl)))
    log_a = (-LRU_C * r) * softplus
    a = jnp.exp(log_a)
    mult = jnp.sqrt(1.0 - jnp.exp(2.0 * log_a))
    if start_pos == 0:
        reset = (i * ts + lax.broadcasted_iota(jnp.int32, (ts, w), 0)) == 0
        a = jnp.where(reset, 0.0, a)
        mult = jnp.where(reset, 1.0, mult)
    a_ref[...] = a
    b_ref[...] = mult * ig * conv

    rowi = lax.broadcasted_iota(jnp.int32, (SUBLANES, w), 0)

    def group(gi, h):
        o = pl.multiple_of(gi * SUBLANES, SUBLANES)
        a8 = a_ref[pl.ds(o, SUBLANES), :]
        b8 = b_ref[pl.ds(o, SUBLANES), :]
        for d in (1, 2, 4):
            keep = rowi >= d
            a_sh = jnp.where(keep, pltpu.roll(a8, d, 0), 1.0)
            b_sh = jnp.where(keep, pltpu.roll(b8, d, 0), 0.0)
            b8 = a8 * b_sh + b8
            a8 = a8 * a_sh
        h8 = a8 * h + b8
        b_ref[pl.ds(o, SUBLANES), :] = h8
        return jnp.broadcast_to(h8[SUBLANES - 1:SUBLANES, :], (SUBLANES, w))

    h = lax.fori_loop(0, ts // SUBLANES, group, hcar_ref[...])
    hcar_ref[...] = h
    y_ref[0] = b_ref[...] * _gelu_tanh(gl_ref[0])
    hlast_ref[0] = h[0:1, :]


def _block_diag(wb):
    nb, d, _ = wb.shape
    eye = jnp.eye(nb, dtype=wb.dtype)
    return (wb[:, :, None, :] * eye[:, None, :, None]).reshape(nb * d, nb * d)


def _lru(xl, gl, h0, conv0, conv_w, conv_b, wa_bd, ba, wx_bd, bx, lam, ts, start_pos):
    nb, s, w = xl.shape
    tok = pl.BlockSpec((1, ts, w), lambda b, i: (b, i, 0))
    row = pl.BlockSpec((1, w), lambda b, i: (0, 0))
    mat = pl.BlockSpec((w, w), lambda b, i: (0, 0))
    return pl.pallas_call(
        functools.partial(_lru_kernel, ts=ts, start_pos=start_pos),
        grid=(nb, s // ts),
        in_specs=[tok, tok,
                  pl.BlockSpec((1, 1, w), lambda b, i: (b, 0, 0)),
                  pl.BlockSpec((1, CONV_W - 1, w), lambda b, i: (b, 0, 0)),
                  pl.BlockSpec((CONV_W, w), lambda b, i: (0, 0)),
                  row, mat, row, mat, row, row],
        out_specs=[tok,
                   pl.BlockSpec((1, 1, w), lambda b, i: (b, 0, 0)),
                   pl.BlockSpec((1, CONV_W - 1, w), lambda b, i: (b, 0, 0))],
        out_shape=[jax.ShapeDtypeStruct((nb, s, w), F32),
                   jax.ShapeDtypeStruct((nb, 1, w), F32),
                   jax.ShapeDtypeStruct((nb, CONV_W - 1, w), F32)],
        scratch_shapes=[pltpu.VMEM((ts + SUBLANES, w), F32), pltpu.VMEM((ts, w), F32),
                        pltpu.VMEM((ts, w), F32), pltpu.VMEM((SUBLANES, w), F32)],
        compiler_params=_cparams(("parallel", "arbitrary")),
        name="rg_lru",
    )(xl, gl, h0.reshape(nb, 1, w), conv0, conv_w, conv_b.reshape(1, w), wa_bd,
      ba.reshape(1, w), wx_bd, bx.reshape(1, w), lam.reshape(1, w))


def _route(logits_t, bias_col, n_exp, ts):
    scores = jax.nn.sigmoid(logits_t)
    biased = scores + bias_col
    gsz = n_exp // N_GROUPS
    gscore = []
    for g in range(N_GROUPS):
        blk = biased[g * gsz:(g + 1) * gsz, :]
        m1 = jnp.max(blk, axis=0, keepdims=True)
        cnt = jnp.sum((blk == m1).astype(F32), axis=0, keepdims=True)
        m2 = jnp.max(jnp.where(blk < m1, blk, -jnp.inf), axis=0, keepdims=True)
        gscore.append(m1 + jnp.where(cnt >= 2.0, m1, m2))
    cands = []
    for g in range(N_GROUPS):
        rank = jnp.zeros((1, ts), jnp.int32)
        for g2 in range(N_GROUPS):
            if g2 == g:
                continue
            beats = (gscore[g2] > gscore[g]) if g2 > g else (gscore[g2] >= gscore[g])
            rank = rank + beats.astype(jnp.int32)
        blk = biased[g * gsz:(g + 1) * gsz, :]
        cands.append(jnp.where(rank < TOPK_GROUPS, blk, -jnp.inf))
    cand = jnp.concatenate(cands, axis=0)
    e_iota = lax.broadcasted_iota(jnp.int32, (n_exp, ts), 0)
    rank = jnp.zeros((n_exp, ts), jnp.int32)
    for e2 in range(n_exp):
        row = cand[e2:e2 + 1, :]
        beats = (row > cand) | ((row == cand) & (e2 < e_iota))
        rank = rank + beats.astype(jnp.int32)
    w = jnp.where(rank < TOP_K, scores, 0.0)
    return ROUTED_SCALE * w / jnp.sum(w, axis=0, keepdims=True)


def _outproj_kernel(attn_ref, lru_ref, x_ref, gm_ref, sf_ref, cf_ref, ag_ref, lg_ref, wo_ref,
                    ng_ref, rw_ref, rb_ref, x1_ref, h2_ref, gates_ref, *, ts):
    aw = attn_ref.shape[2]
    an = _rms(attn_ref[0], ag_ref[...]).astype(BF16)
    ln = _rms(lru_ref[0], lg_ref[...]).astype(BF16)
    mixed = _dot(an, wo_ref[0:aw, :]) + _dot(ln, wo_ref[aw:, :])
    x1 = x_ref[0] + gm_ref[0] * mixed
    x1_ref[0] = x1
    h2 = _rms(x1, ng_ref[...]) * (1.0 + cf_ref[0]) + sf_ref[0]
    h2_ref[0] = h2.astype(BF16)
    n_exp = rw_ref.shape[0]
    gates_t = _route(_dot3(rw_ref[...], h2, NT), rb_ref[...], n_exp, ts)
    gates_t = jnp.concatenate([gates_t, jnp.zeros((LANES - n_exp, ts), F32)], axis=0)
    gates_ref[0] = gates_t.T


def _outproj(attn, lru, x, gate_m, shift_f, scale_f, ag, lg, wo_bf, ng, rw_t, rb_col, ts):
    nb, s, d = x.shape
    aw = attn.shape[2]
    n_exp = rw_t.shape[0]
    tok = lambda width: pl.BlockSpec((1, ts, width), lambda b, i: (b, i, 0))
    const = lambda a: pl.BlockSpec(a.shape, lambda b, i: (0,) * a.ndim)
    return pl.pallas_call(
        functools.partial(_outproj_kernel, ts=ts),
        grid=(nb, s // ts),
        in_specs=[tok(aw), tok(lru.shape[2]), tok(d), _mod_spec(gate_m, ts), _mod_spec(shift_f, ts),
                  _mod_spec(scale_f, ts), const(ag), const(lg), const(wo_bf), const(ng),
                  const(rw_t), const(rb_col)],
        out_specs=[tok(d), tok(d), tok(LANES)],
        out_shape=[jax.ShapeDtypeStruct((nb, s, d), F32), jax.ShapeDtypeStruct((nb, s, d), BF16),
                   jax.ShapeDtypeStruct((nb, s, LANES), F32)],
        compiler_params=_cparams(("parallel", "parallel")),
        name="outproj_router",
    )(attn, lru, x, gate_m, shift_f, scale_f, ag, lg, wo_bf, ng, rw_t, rb_col)


def _swiglu_bf(h, w_gu, w_down):
    gu = _dot(h, w_gu)
    de = gu.shape[1] // 2
    act = _silu(gu[:, :de]) * gu[:, de:]
    return _dot(act.astype(BF16), w_down)


def _moe_kernel(x1_ref, h2_ref, gates_ref, gf_ref, fg_ref, sgu_ref, sdn_ref, wgu_ref, wdn_ref,
                y_ref, acc_ref):
    e = pl.program_id(2)
    h = h2_ref[0]

    @pl.when(e == 0)
    def _():
        acc_ref[...] = _swiglu_bf(h, sgu_ref[...], sdn_ref[...])

    lane = lax.broadcasted_iota(jnp.int32, gates_ref.shape[1:], 1)
    gcol = jnp.sum(jnp.where(lane == e, gates_ref[0], 0.0), axis=1, keepdims=True)
    acc_ref[...] += gcol * _swiglu_bf(h, wgu_ref[0].astype(BF16), wdn_ref[0].astype(BF16))

    @pl.when(e == pl.num_programs(2) - 1)
    def _():
        y_ref[0] = _rms(x1_ref[0] + gf_ref[0] * acc_ref[...], fg_ref[...])


def _moe(x1, h2, gates, gate_f, final_g, sgu_bf, sdn_bf, w_gu, w_down, tm):
    nb, s, d = x1.shape
    n_exp = w_gu.shape[0]
    tok = lambda width: pl.BlockSpec((1, tm, width), lambda b, i, e: (b, i, 0))
    if gate_f.shape[1] == 1:
        gf_spec = pl.BlockSpec((1, 1, d), lambda b, i, e: (b, 0, 0))
    else:
        gf_spec = pl.BlockSpec((1, tm, d), lambda b, i, e: (b, i, 0))
    const = lambda a: pl.BlockSpec(a.shape, lambda b, i, e: (0,) * a.ndim)
    return pl.pallas_call(
        _moe_kernel,
        grid=(nb, s // tm, n_exp),
        in_specs=[tok(d), tok(d), tok(LANES), gf_spec, const(final_g), const(sgu_bf), const(sdn_bf),
                  pl.BlockSpec((1,) + w_gu.shape[1:], lambda b, i, e: (e, 0, 0)),
                  pl.BlockSpec((1,) + w_down.shape[1:], lambda b, i, e: (e, 0, 0))],
        out_specs=tok(d),
        out_shape=jax.ShapeDtypeStruct((nb, s, d), F32),
        scratch_shapes=[pltpu.VMEM((tm, d), F32)],
        compiler_params=_cparams(("parallel", "parallel", "arbitrary")),
        name="moe_final",
    )(x1, h2, gates, gate_f, final_g, sgu_bf, sdn_bf, w_gu, w_down)


def _rope_tables(start_pos, s):
    half = HEAD_DIM // 2
    inv_freq = ROPE_THETA ** (-jnp.arange(half, dtype=F32) / half)
    ang = (start_pos + jnp.arange(s, dtype=jnp.int32)).astype(F32)[:, None] * inv_freq[None, :]
    cos, sin = jnp.cos(ang), jnp.sin(ang)
    reps = LANES // HEAD_DIM
    return (jnp.tile(jnp.concatenate([cos, cos], axis=1), (1, reps)),
            jnp.tile(jnp.concatenate([-sin, sin], axis=1), (1, reps)))


def _layer(x, mods, attend, h0, conv0, start_pos, ts, tm, wts):
    shift_m, scale_m, gate_m, shift_f, scale_f, gate_f = mods
    s = x.shape[1]
    cos, sin = _rope_tables(start_pos, s)
    q, k, v, xl, gl = _inproj(x, shift_m, scale_m, cos, sin, wts["norm_mix_g"], wts["w_in"], ts)
    attn = attend(q, k, v)
    lru, h_last, cbuf = _lru(xl, gl, h0, conv0, wts["conv_w"], wts["conv_b"], wts["wa"],
                             wts["gate_a_b"], wts["wx"], wts["gate_x_b"], wts["lru_lambda"],
                             ts, start_pos)
    x1, h2, gates = _outproj(attn, lru, x, gate_m, shift_f, scale_f, wts["attn_out_g"],
                             wts["lru_out_g"], wts["w_out"], wts["norm_ffn_g"], wts["router_w_t"],
                             wts["router_bias"], ts)
    y = _moe(x1, h2, gates, gate_f, wts["final_g"], wts["shared_w_gu"], wts["shared_w_down"],
             wts["exp_w_gu"], wts["exp_w_down"], tm)
    return y, k, v, h_last, cbuf


def kernel(x_prompt, x_sample, c_prompt, c_sample, cache_k, cache_v, state_h, state_conv, page_table, ada_w, ada_b, norm_mix_g, w_in, conv_w, conv_b, gate_a_w, gate_a_b, gate_x_w, gate_x_b, lru_lambda, attn_out_g, lru_out_g, w_out, norm_ffn_g, router_w, router_bias, exp_w_gu, exp_w_down, shared_w_gu, shared_w_down, final_g):
    depth = ada_w.shape[0]
    assert depth == 1, "single-layer trunk"
    n_prm, seq, d = x_prompt.shape
    n_dec, dec_seq, _ = x_sample.shape
    n_heads = cache_k.shape[3]
    aw = n_heads * HEAD_DIM
    lw = state_h.shape[2]
    past_len = page_table.shape[1] * PAGE_SIZE
    l = 0
    row = lambda a: a.reshape(1, -1)
    wts = dict(
        norm_mix_g=row(norm_mix_g[l]), w_in=w_in[l].astype(BF16), conv_w=conv_w[l], conv_b=conv_b[l],
        wa=_block_diag(gate_a_w[l]).astype(BF16), gate_a_b=gate_a_b[l],
        wx=_block_diag(gate_x_w[l]).astype(BF16), gate_x_b=gate_x_b[l], lru_lambda=lru_lambda[l],
        attn_out_g=row(attn_out_g[l]), lru_out_g=row(lru_out_g[l]), w_out=w_out[l].astype(BF16),
        norm_ffn_g=row(norm_ffn_g[l]), router_w_t=router_w[l].T,
        router_bias=router_bias[l].reshape(-1, 1), final_g=row(final_g),
        shared_w_gu=shared_w_gu[l].astype(BF16), shared_w_down=shared_w_down[l].astype(BF16),
        exp_w_gu=exp_w_gu[l], exp_w_down=exp_w_down[l])

    n_c = n_prm + n_dec
    c_all = jnp.concatenate([c_prompt, c_sample, jnp.zeros((-n_c % SUBLANES, d), F32)], axis=0)
    mod = _ada_mod(c_all, ada_w[l], ada_b[l])
    mods_p = [m[:n_prm, None, :] for m in jnp.split(mod, 6, axis=-1)]
    mods_s = [jnp.repeat(m[n_prm:n_c], dec_seq, axis=0)[None] for m in jnp.split(mod, 6, axis=-1)]

    yp, k1, v1, h1, c1 = _layer(
        x_prompt, mods_p, _moba_prompt, jnp.zeros((n_prm, lw), F32),
        jnp.zeros((n_prm, CONV_W - 1, lw), F32), 0, 512, 1024, wts)

    ck = cache_k[l].reshape(cache_k.shape[1], PAGE_SIZE, aw)
    cv = cache_v[l].reshape(cache_v.shape[1], PAGE_SIZE, aw)

    def attend_sample(q, k, v):
        r = lambda a: a.reshape(n_dec, dec_seq, aw)
        return _moba_sample(r(q), r(k), r(v), ck, cv, page_table).reshape(1, n_dec * dec_seq, aw)

    xs = x_sample.reshape(1, n_dec * dec_seq, d)
    shift_m, scale_m, gate_m, shift_f, scale_f, gate_f = mods_s
    cos, sin = _rope_tables(past_len, dec_seq)
    cos, sin = jnp.tile(cos, (n_dec, 1)), jnp.tile(sin, (n_dec, 1))
    nt = n_dec * dec_seq
    q, k2, v2, xl, gl = _inproj(xs, shift_m, scale_m, cos, sin, wts["norm_mix_g"], wts["w_in"], nt)
    attn = attend_sample(q, k2, v2)
    lru, h2s, c2 = _lru(xl.reshape(n_dec, dec_seq, lw), gl.reshape(n_dec, dec_seq, lw), state_h[l],
                        state_conv[l], wts["conv_w"], wts["conv_b"], wts["wa"], wts["gate_a_b"],
                        wts["wx"], wts["gate_x_b"], wts["lru_lambda"], dec_seq, past_len)
    x1, hh, gates = _outproj(attn, lru.reshape(1, nt, lw), xs, gate_m, shift_f, scale_f,
                             wts["attn_out_g"], wts["lru_out_g"], wts["w_out"], wts["norm_ffn_g"],
                             wts["router_w_t"], wts["router_bias"], nt)
    ys = _moe(x1, hh, gates, gate_f, wts["final_g"], wts["shared_w_gu"], wts["shared_w_down"],
              wts["exp_w_gu"], wts["exp_w_down"], nt)

    return (yp, ys.reshape(n_dec, dec_seq, d),
            k1.reshape(1, n_prm, seq, n_heads, HEAD_DIM), v1.reshape(1, n_prm, seq, n_heads, HEAD_DIM),
            h1.reshape(1, n_prm, lw), c1[None],
            k2.reshape(1, n_dec, dec_seq, n_heads, HEAD_DIM), v2.reshape(1, n_dec, dec_seq, n_heads, HEAD_DIM),
            h2s.reshape(1, n_dec, lw), c2[None])
```

```python
import functools
import math

import jax
import jax.numpy as jnp
from jax import lax
from jax.experimental import pallas as pl
from jax.experimental.pallas import tpu as pltpu

F32 = jnp.float32
BF16 = jnp.bfloat16

HEAD_DIM = 64
LRU_BLOCKS = 8
CONV_W = 4
LRU_C = 8.0
MOBA_BLOCK = 256
MOBA_TOPK = 3
ROPE_THETA = 10000.0
N_GROUPS = 8
TOPK_GROUPS = 4
TOP_K = 8
ROUTED_SCALE = 2.5
EPS = 1e-6
PAGE_SIZE = 128

LANES = 128
SUBLANES = 8
NEG = -1e30
N_CHAINS = 2
VMEM_LIMIT = 56 * 1024 * 1024

NN = (((1,), (0,)), ((), ()))
NT = (((1,), (1,)), ((), ()))


def _cparams(sem):
    return pltpu.CompilerParams(dimension_semantics=sem, vmem_limit_bytes=VMEM_LIMIT)


def _split(x):
    hi = x.astype(BF16)
    lo = (x - hi.astype(F32)).astype(BF16)
    return hi, lo


def _dot3(a, b, dims):
    ah, al = _split(a)
    bh, bl = _split(b)
    d = lambda u, v: lax.dot_general(u, v, dims, preferred_element_type=F32)
    return d(ah, bh) + (d(ah, bl) + d(al, bh))


def _dot(a, b, dims=NN):
    return lax.dot_general(a, b, dims, preferred_element_type=F32)


def _rms(x, g):
    return x * lax.rsqrt(jnp.mean(x * x, axis=-1, keepdims=True) + EPS) * g


def _silu(x):
    return x * jax.nn.sigmoid(x)


def _gelu_tanh(x):
    c = math.sqrt(2.0 / math.pi)
    return 0.5 * x * (1.0 + jnp.tanh(c * (x + 0.044715 * (x * x * x))))


def _ada_kernel(c_ref, w_ref, b_ref, o_ref):
    o_ref[...] = _dot3(_silu(c_ref[...]), w_ref[...], NN) + b_ref[...]


def _ada_mod(c, ada_w, ada_b):
    r, d = c.shape
    n = ada_w.shape[1]
    tn = 1536
    return pl.pallas_call(
        _ada_kernel,
        grid=(n // tn,),
        in_specs=[pl.BlockSpec((r, d), lambda j: (0, 0)),
                  pl.BlockSpec((d, tn), lambda j: (0, j)),
                  pl.BlockSpec((1, tn), lambda j: (0, j))],
        out_specs=pl.BlockSpec((r, tn), lambda j: (0, j)),
        out_shape=jax.ShapeDtypeStruct((r, n), F32),
        compiler_params=_cparams(("parallel",)),
        name="ada_mod",
    )(c, ada_w, ada_b.reshape(1, n))


def _inproj_kernel(x_ref, shift_ref, scale_ref, cos_ref, sin_ref, g_ref, w_ref,
                   q_ref, k_ref, v_ref, xl_ref, gl_ref, *, aw):
    x = x_ref[0]
    h = _rms(x, g_ref[...]) * (1.0 + scale_ref[0]) + shift_ref[0]
    proj = _dot(h.astype(BF16), w_ref[...])
    cos = cos_ref[...]
    sin = sin_ref[...]
    lane = lax.broadcasted_iota(jnp.int32, cos.shape, 1)
    first = (lane % HEAD_DIM) < (HEAD_DIM // 2)

    def rope(t):
        outs = []
        for c in range(aw // LANES):
            tc = t[:, LANES * c:LANES * (c + 1)]
            partner = jnp.where(first, pltpu.roll(tc, LANES - HEAD_DIM // 2, 1),
                                pltpu.roll(tc, HEAD_DIM // 2, 1))
            outs.append(tc * cos + partner * sin)
        return jnp.concatenate(outs, axis=1)

    q_ref[0] = rope(proj[:, 0:aw])
    k_ref[0] = rope(proj[:, aw:2 * aw])
    v_ref[0] = proj[:, 2 * aw:3 * aw]
    lw = (proj.shape[1] - 3 * aw) // 2
    xl_ref[0] = proj[:, 3 * aw:3 * aw + lw]
    gl_ref[0] = proj[:, 3 * aw + lw:]


def _mod_spec(mod, ts):
    if mod.shape[1] == 1:
        return pl.BlockSpec((1, 1, mod.shape[2]), lambda b, i: (b, 0, 0))
    return pl.BlockSpec((1, ts, mod.shape[2]), lambda b, i: (b, i, 0))


def _inproj(x, shift, scale, cos, sin, g, w_bf, ts):
    nb, s, d = x.shape
    aw = 512
    lw = (w_bf.shape[1] - 3 * aw) // 2
    tok = lambda width: pl.BlockSpec((1, ts, width), lambda b, i: (b, i, 0))
    shp = lambda width: jax.ShapeDtypeStruct((nb, s, width), F32)
    return pl.pallas_call(
        functools.partial(_inproj_kernel, aw=aw),
        grid=(nb, s // ts),
        in_specs=[tok(d), _mod_spec(shift, ts), _mod_spec(scale, ts),
                  pl.BlockSpec((ts, LANES), lambda b, i: (i, 0)),
                  pl.BlockSpec((ts, LANES), lambda b, i: (i, 0)),
                  pl.BlockSpec((1, d), lambda b, i: (0, 0)),
                  pl.BlockSpec(w_bf.shape, lambda b, i: (0, 0))],
        out_specs=[tok(aw), tok(aw), tok(aw), tok(lw), tok(lw)],
        out_shape=[shp(aw), shp(aw), shp(aw), shp(lw), shp(lw)],
        compiler_params=_cparams(("parallel", "parallel")),
        name="inproj",
    )(x, shift, scale, cos, sin, g, w_bf)


def _moba_prompt_kernel(q_ref, k_ref, v_ref, o_ref, kbf_ref, vst_ref, km_ref, selb_ref):
    s_len = k_ref.shape[1]
    tq = MOBA_BLOCK
    nblk = s_len // MOBA_BLOCK
    half = lax.broadcasted_iota(jnp.int32, (LANES, tq), 0) < HEAD_DIM
    kbf_ref[...] = k_ref[0].astype(BF16)
    km_ref[...] = jnp.zeros(km_ref.shape, F32)
    for n in range(nblk):
        rows = slice(n * MOBA_BLOCK, (n + 1) * MOBA_BLOCK)
        km_ref[n:n + 1, :] = jnp.mean(k_ref[0, rows, :], axis=0, keepdims=True)
        vt = v_ref[0, rows, :].T
        vst_ref[n] = jnp.concatenate([jnp.where(half, vt, 0.0), jnp.where(half, 0.0, vt)],
                                     axis=1).astype(BF16)
    nrow = km_ref.shape[0]
    n_iota = lax.broadcasted_iota(jnp.int32, (nrow, 2 * tq), 0)
    keyi = lax.broadcasted_iota(jnp.int32, (MOBA_BLOCK, 2 * tq), 0)
    qi = lax.broadcasted_iota(jnp.int32, (MOBA_BLOCK, 2 * tq), 1) % tq

    def stack_heads(pb):
        return jnp.concatenate([pb[:, :tq], pb[:, tq:]], axis=0)

    def per_head(rowvec):
        return jnp.where(half, rowvec[:, :tq], rowvec[:, tq:])

    def qtile(t, carry):
        r0 = pl.multiple_of(t * tq, tq)
        qt = q_ref[0, pl.ds(r0, tq), :].T
        qt2 = jnp.concatenate([jnp.where(half, qt, 0.0), jnp.where(half, 0.0, qt)], axis=1)
        gm = jnp.where(n_iota < t, _dot3(km_ref[...], qt2, NN), -jnp.inf)
        rank = jnp.zeros(gm.shape, jnp.int32)
        for m_ in range(nblk):
            row = gm[m_:m_ + 1, :]
            beats = (row > gm) | ((row == gm) & (m_ < n_iota))
            rank = rank + beats.astype(jnp.int32)
        selb_ref[...] = jnp.where((n_iota < t) & (rank < MOBA_TOPK), 0.0, NEG)
        qb = (qt2 * (HEAD_DIM ** -0.5)).astype(BF16)
        st = jnp.where(keyi <= qi, _dot(kbf_ref[pl.ds(r0, MOBA_BLOCK), :], qb), NEG)
        m = jnp.max(st, axis=0, keepdims=True)
        p = jnp.exp(st - m)
        l = jnp.sum(p, axis=0, keepdims=True)
        acc = _dot(vst_ref[t], stack_heads(p.astype(BF16)))

        def step(j, m, l, acc):
            j0 = pl.multiple_of(j * MOBA_BLOCK, MOBA_BLOCK)
            st = _dot(kbf_ref[pl.ds(j0, MOBA_BLOCK), :], qb) + selb_ref[pl.ds(j, 1), :]
            m_new = jnp.maximum(m, jnp.max(st, axis=0, keepdims=True))
            alpha = jnp.exp(m - m_new)
            p = jnp.exp(st - m_new)
            l = alpha * l + jnp.sum(p, axis=0, keepdims=True)
            acc = per_head(alpha) * acc + _dot(vst_ref[j], stack_heads(p.astype(BF16)))
            return m_new, l, acc

        def body(i, chains):
            return tuple(step(N_CHAINS * i + c, *chains[c]) for c in range(N_CHAINS))

        idle = (jnp.full_like(m, NEG), jnp.zeros_like(l), jnp.zeros_like(acc))
        chains = lax.fori_loop(0, (t + N_CHAINS - 1) // N_CHAINS, body,
                               ((m, l, acc),) + (idle,) * (N_CHAINS - 1))
        m = functools.reduce(jnp.maximum, [c[0] for c in chains])
        w = [jnp.exp(c[0] - m) for c in chains]
        l = sum(wc * c[1] for wc, c in zip(w, chains))
        acc = sum(per_head(wc) * c[2] for wc, c in zip(w, chains))
        o_ref[0, pl.ds(r0, tq), :] = (acc / per_head(l)).T
        return carry

    lax.fori_loop(0, s_len // tq, qtile, 0)


def _moba_prompt(q, k, v):
    nb, s, aw = q.shape
    nblk = s // MOBA_BLOCK
    nrow = -(-nblk // 16) * 16
    spec = pl.BlockSpec((1, s, LANES), lambda b, h: (b, 0, h))
    return pl.pallas_call(
        _moba_prompt_kernel,
        grid=(nb, aw // LANES),
        in_specs=[spec, spec, spec],
        out_specs=spec,
        out_shape=jax.ShapeDtypeStruct((nb, s, aw), F32),
        scratch_shapes=[pltpu.VMEM((s, LANES), BF16),
                        pltpu.VMEM((nblk, LANES, 2 * MOBA_BLOCK), BF16),
                        pltpu.VMEM((nrow, LANES), F32),
                        pltpu.VMEM((nrow, 2 * MOBA_BLOCK), F32)],
        compiler_params=_cparams(("parallel", "parallel")),
        name="moba_prompt",
    )(q, k, v)


def _moba_sample_kernel(pt_ref, q_ref, kn_ref, vn_ref, *rest, n_heads, n_q, bps):
    ppb = MOBA_BLOCK // PAGE_SIZE
    npg = bps * ppb
    k_refs, v_refs = rest[:npg], rest[npg:2 * npg]
    o_ref, qbd_ref, km_ref, m_ref, l_ref, o_scr = rest[2 * npg:]
    step = pl.program_id(1)
    nstep = pl.num_programs(1)
    rows = n_heads * n_q
    aw = q_ref.shape[2]
    rowi = lax.broadcasted_iota(jnp.int32, (rows, aw), 0)
    coli = lax.broadcasted_iota(jnp.int32, (rows, aw), 1)
    diag = (rowi // n_q) == (coli // HEAD_DIM)
    lane = lax.broadcasted_iota(jnp.int32, (rows, LANES), 1)

    @pl.when(step == 0)
    def _():
        qt = jnp.concatenate([q_ref[0]] * n_heads, axis=0)
        qbd_ref[...] = jnp.where(diag, qt, 0.0)
        km_ref[...] = jnp.zeros(km_ref.shape, F32)
        m_ref[...] = jnp.full(m_ref.shape, NEG, F32)
        l_ref[...] = jnp.zeros(l_ref.shape, F32)

    qbd = qbd_ref[...]
    qb = (qbd * (HEAD_DIM ** -0.5)).astype(BF16)
    klane = lax.broadcasted_iota(jnp.int32, km_ref.shape, 1)
    km, mm_, ll_ = km_ref[...], m_ref[...], l_ref[...]
    for c in range(bps):
        j = step * bps + c
        kt = jnp.concatenate([r[0] for r in k_refs[c * ppb:(c + 1) * ppb]], axis=1)
        vt = jnp.concatenate([r[0] for r in v_refs[c * ppb:(c + 1) * ppb]], axis=1)
        km = jnp.where(klane == j, jnp.mean(kt, axis=1, keepdims=True), km)
        s = _dot(qb, kt.astype(BF16))
        mj = jnp.max(s, axis=1, keepdims=True)
        p = jnp.exp(s - mj)
        lj = jnp.sum(p, axis=1, keepdims=True)
        o_scr[j] = _dot(p.astype(BF16), vt.astype(BF16), NT)
        mm_ = jnp.where(lane == j, mj, mm_)
        ll_ = jnp.where(lane == j, lj, ll_)
    km_ref[...], m_ref[...], l_ref[...] = km, mm_, ll_

    @pl.when(step == nstep - 1)
    def _():
        npast = o_scr.shape[0]
        gate = _dot3(qbd, km_ref[...], NN)
        valid = lane < npast
        gm = jnp.where(valid, gate, -jnp.inf)
        rank = jnp.zeros((rows, LANES), jnp.int32)
        for mm in range(npast):
            colv = gm[:, mm:mm + 1]
            beats = (colv > gm) | ((colv == gm) & (mm < lane))
            rank = rank + beats.astype(jnp.int32)
        sel = valid & (rank < MOBA_TOPK)
        so = _dot(qb, kn_ref[0].astype(BF16), NT)
        qi = lax.broadcasted_iota(jnp.int32, (rows, n_q), 0) % n_q
        ki = lax.broadcasted_iota(jnp.int32, (rows, n_q), 1)
        so = jnp.where(ki <= qi, so, NEG)
        mo = jnp.max(so, axis=1, keepdims=True)
        po = jnp.exp(so - mo)
        lo = jnp.sum(po, axis=1, keepdims=True)
        oo = _dot(po.astype(BF16), vn_ref[0].astype(BF16))
        mall = jnp.where(sel, m_ref[...], NEG)
        mx = jnp.maximum(jnp.max(mall, axis=1, keepdims=True), mo)
        w = jnp.where(sel, jnp.exp(mall - mx), 0.0)
        wo = jnp.exp(mo - mx)
        den = jnp.sum(w * l_ref[...], axis=1, keepdims=True) + wo * lo
        num = wo * oo
        for jj in range(npast):
            num = num + w[:, jj:jj + 1] * o_scr[jj]
        res = jnp.where(diag, num / den, 0.0)
        out = res[0:n_q, :]
        for hh in range(1, n_heads):
            out = out + res[hh * n_q:(hh + 1) * n_q, :]
        o_ref[0] = out


def _moba_sample(q, k_new, v_new, cache_kt, cache_vt, page_table):
    nb, n_q, aw = q.shape
    n_heads = aw // HEAD_DIM
    n_pages = page_table.shape[1]
    ppb = MOBA_BLOCK // PAGE_SIZE
    npast = n_pages // ppb
    bps = 2 if npast % 2 == 0 else 1
    npg = bps * ppb
    rows = n_heads * n_q
    tok = pl.BlockSpec((1, n_q, aw), lambda b, j, pt: (b, 0, 0))
    page = lambda o: pl.BlockSpec((1, aw, PAGE_SIZE), lambda b, j, pt: (pt[b, npg * j + o], 0, 0))
    pages = [page(o) for o in range(npg)]
    grid_spec = pltpu.PrefetchScalarGridSpec(
        num_scalar_prefetch=1,
        grid=(nb, npast // bps),
        in_specs=[tok, tok, tok] + pages + pages,
        out_specs=tok,
        scratch_shapes=[pltpu.VMEM((rows, aw), F32), pltpu.VMEM((aw, LANES), F32),
                        pltpu.VMEM((rows, LANES), F32), pltpu.VMEM((rows, LANES), F32),
                        pltpu.VMEM((npast, rows, aw), F32)],
    )
    return pl.pallas_call(
        functools.partial(_moba_sample_kernel, n_heads=n_heads, n_q=n_q, bps=bps),
        grid_spec=grid_spec,
        out_shape=jax.ShapeDtypeStruct((nb, n_q, aw), F32),
        compiler_params=_cparams(("parallel", "arbitrary")),
        name="moba_sample",
    )(page_table, q, k_new, v_new, *([cache_kt] * npg), *([cache_vt] * npg))


def _lru_kernel(xl_ref, gl_ref, h0_ref, c0_ref, cw_ref, cb_ref, wa_ref, ba_ref, wx_ref, bx_ref,
                lam_ref, y_ref, hlast_ref, cbuf_ref, xbuf_ref, a_ref, b_ref, hcar_ref,
                *, ts, start_pos):
    i = pl.program_id(1)
    w = xl_ref.shape[2]
    hist = SUBLANES

    @pl.when(i == 0)
    def _():
        xbuf_ref[0:hist, :] = jnp.zeros((hist, w), F32)
        xbuf_ref[hist - (CONV_W - 1):hist, :] = c0_ref[0]
        hcar_ref[...] = jnp.broadcast_to(h0_ref[0], (SUBLANES, w))

    x = xl_ref[0]
    xbuf_ref[hist:hist + ts, :] = x
    conv = cb_ref[...] + cw_ref[0:1, :] * xbuf_ref[hist - 3:hist - 3 + ts, :]
    conv = conv + cw_ref[1:2, :] * xbuf_ref[hist - 2:hist - 2 + ts, :]
    conv = conv + cw_ref[2:3, :] * xbuf_ref[hist - 1:hist - 1 + ts, :]
    conv = conv + cw_ref[3:4, :] * x
    cbuf_ref[0] = xbuf_ref[hist + ts - (CONV_W - 1):hist + ts, :]
    xbuf_ref[0:hist, :] = xbuf_ref[ts:ts + hist, :]

    cbf = conv.astype(BF16)
    r = jax.nn.sigmoid(_dot(cbf, wa_ref[...]) + ba_ref[...])
    ig = jax.nn.sigmoid(_dot(cbf, wx_ref[...]) + bx_ref[...])
    nl = -lam_ref[...]
    softplus = jnp.maximum(nl, 0.0) + jnp.log1p(jnp.exp(-jnp.abs(nl)))
    log_a = (-LRU_C * r) * softplus
    a = jnp.exp(log_a)
    mult = jnp.sqrt(1.0 - jnp.exp(2.0 * log_a))
    if start_pos == 0:
        reset = (i * ts + lax.broadcasted_iota(jnp.int32, (ts, w), 0)) == 0
        a = jnp.where(reset, 0.0, a)
        mult = jnp.where(reset, 1.0, mult)
    a_ref[...] = a
    b_ref[...] = mult * ig * conv

    rowi = lax.broadcasted_iota(jnp.int32, (SUBLANES, w), 0)

    def group(gi, h):
        o = pl.multiple_of(gi * SUBLANES, SUBLANES)
        a8 = a_ref[pl.ds(o, SUBLANES), :]
        b8 = b_ref[pl.ds(o, SUBLANES), :]
        for d in (1, 2, 4):
            keep = rowi >= d
            a_sh = jnp.where(keep, pltpu.roll(a8, d, 0), 1.0)
            b_sh = jnp.where(keep, pltpu.roll(b8, d, 0), 0.0)
            b8 = a8 * b_sh + b8
            a8 = a8 * a_sh
        h8 = a8 * h + b8
        b_ref[pl.ds(o, SUBLANES), :] = h8
        return jnp.broadcast_to(h8[SUBLANES - 1:SUBLANES, :], (SUBLANES, w))

    h = lax.fori_loop(0, ts // SUBLANES, group, hcar_ref[...])
    hcar_ref[...] = h
    y_ref[0] = b_ref[...] * _gelu_tanh(gl_ref[0])
    hlast_ref[0] = h[0:1, :]


def _block_diag(wb):
    nb, d, _ = wb.shape
    eye = jnp.eye(nb, dtype=wb.dtype)
    return (wb[:, :, None, :] * eye[:, None, :, None]).reshape(nb * d, nb * d)


def _lru(xl, gl, h0, conv0, conv_w, conv_b, wa_bd, ba, wx_bd, bx, lam, ts, start_pos):
    nb, s, w = xl.shape
    tok = pl.BlockSpec((1, ts, w), lambda b, i: (b, i, 0))
    row = pl.BlockSpec((1, w), lambda b, i: (0, 0))
    mat = pl.BlockSpec((w, w), lambda b, i: (0, 0))
    return pl.pallas_call(
        functools.partial(_lru_kernel, ts=ts, start_pos=start_pos),
        grid=(nb, s // ts),
        in_specs=[tok, tok,
                  pl.BlockSpec((1, 1, w), lambda b, i: (b, 0, 0)),
                  pl.BlockSpec((1, CONV_W - 1, w), lambda b, i: (b, 0, 0)),
                  pl.BlockSpec((CONV_W, w), lambda b, i: (0, 0)),
                  row, mat, row, mat, row, row],
        out_specs=[tok,
                   pl.BlockSpec((1, 1, w), lambda b, i: (b, 0, 0)),
                   pl.BlockSpec((1, CONV_W - 1, w), lambda b, i: (b, 0, 0))],
        out_shape=[jax.ShapeDtypeStruct((nb, s, w), F32),
                   jax.ShapeDtypeStruct((nb, 1, w), F32),
                   jax.ShapeDtypeStruct((nb, CONV_W - 1, w), F32)],
        scratch_shapes=[pltpu.VMEM((ts + SUBLANES, w), F32), pltpu.VMEM((ts, w), F32),
                        pltpu.VMEM((ts, w), F32), pltpu.VMEM((SUBLANES, w), F32)],
        compiler_params=_cparams(("parallel", "arbitrary")),
        name="rg_lru",
    )(xl, gl, h0.reshape(nb, 1, w), conv0, conv_w, conv_b.reshape(1, w), wa_bd,
      ba.reshape(1, w), wx_bd, bx.reshape(1, w), lam.reshape(1, w))


def _route(logits_t, bias_col, n_exp, ts):
    scores = jax.nn.sigmoid(logits_t)
    biased = scores + bias_col
    gsz = n_exp // N_GROUPS
    gscore = []
    for g in range(N_GROUPS):
        blk = biased[g * gsz:(g + 1) * gsz, :]
        m1 = jnp.max(blk, axis=0, keepdims=True)
        cnt = jnp.sum((blk == m1).astype(F32), axis=0, keepdims=True)
        m2 = jnp.max(jnp.where(blk < m1, blk, -jnp.inf), axis=0, keepdims=True)
        gscore.append(m1 + jnp.where(cnt >= 2.0, m1, m2))
    cands = []
    for g in range(N_GROUPS):
        rank = jnp.zeros((1, ts), jnp.int32)
        for g2 in range(N_GROUPS):
            if g2 == g:
                continue
            beats = (gscore[g2] > gscore[g]) if g2 > g else (gscore[g2] >= gscore[g])
            rank = rank + beats.astype(jnp.int32)
        blk = biased[g * gsz:(g + 1) * gsz, :]
        cands.append(jnp.where(rank < TOPK_GROUPS, blk, -jnp.inf))
    cand = jnp.concatenate(cands, axis=0)
    e_iota = lax.broadcasted_iota(jnp.int32, (n_exp, ts), 0)
    rank = jnp.zeros((n_exp, ts), jnp.int32)
    for e2 in range(n_exp):
        row = cand[e2:e2 + 1, :]
        beats = (row > cand) | ((row == cand) & (e2 < e_iota))
        rank = rank + beats.astype(jnp.int32)
    w = jnp.where(rank < TOP_K, scores, 0.0)
    return ROUTED_SCALE * w / jnp.sum(w, axis=0, keepdims=True)


def _outproj_kernel(attn_ref, lru_ref, x_ref, gm_ref, sf_ref, cf_ref, ag_ref, lg_ref, wo_ref,
                    ng_ref, rw_ref, rb_ref, x1_ref, h2_ref, gates_ref, *, ts):
    aw = attn_ref.shape[2]
    an = _rms(attn_ref[0], ag_ref[...]).astype(BF16)
    ln = _rms(lru_ref[0], lg_ref[...]).astype(BF16)
    mixed = _dot(an, wo_ref[0:aw, :]) + _dot(ln, wo_ref[aw:, :])
    x1 = x_ref[0] + gm_ref[0] * mixed
    x1_ref[0] = x1
    h2 = _rms(x1, ng_ref[...]) * (1.0 + cf_ref[0]) + sf_ref[0]
    h2_ref[0] = h2.astype(BF16)
    n_exp = rw_ref.shape[0]
    gates_t = _route(_dot3(rw_ref[...], h2, NT), rb_ref[...], n_exp, ts)
    gates_t = jnp.concatenate([gates_t, jnp.zeros((LANES - n_exp, ts), F32)], axis=0)
    gates_ref[0] = gates_t.T


def _outproj(attn, lru, x, gate_m, shift_f, scale_f, ag, lg, wo_bf, ng, rw_t, rb_col, ts):
    nb, s, d = x.shape
    aw = attn.shape[2]
    n_exp = rw_t.shape[0]
    tok = lambda width: pl.BlockSpec((1, ts, width), lambda b, i: (b, i, 0))
    const = lambda a: pl.BlockSpec(a.shape, lambda b, i: (0,) * a.ndim)
    return pl.pallas_call(
        functools.partial(_outproj_kernel, ts=ts),
        grid=(nb, s // ts),
        in_specs=[tok(aw), tok(lru.shape[2]), tok(d), _mod_spec(gate_m, ts), _mod_spec(shift_f, ts),
                  _mod_spec(scale_f, ts), const(ag), const(lg), const(wo_bf), const(ng),
                  const(rw_t), const(rb_col)],
        out_specs=[tok(d), tok(d), tok(LANES)],
        out_shape=[jax.ShapeDtypeStruct((nb, s, d), F32), jax.ShapeDtypeStruct((nb, s, d), BF16),
                   jax.ShapeDtypeStruct((nb, s, LANES), F32)],
        compiler_params=_cparams(("parallel", "parallel")),
        name="outproj_router",
    )(attn, lru, x, gate_m, shift_f, scale_f, ag, lg, wo_bf, ng, rw_t, rb_col)


def _swiglu_bf(h, w_gu, w_down):
    gu = _dot(h, w_gu)
    de = gu.shape[1] // 2
    act = _silu(gu[:, :de]) * gu[:, de:]
    return _dot(act.astype(BF16), w_down)


def _moe_kernel(x1_ref, h2_ref, gates_ref, gf_ref, fg_ref, sgu_ref, sdn_ref, wgu_ref, wdn_ref,
                y_ref, acc_ref):
    e = pl.program_id(2)
    h = h2_ref[0]

    @pl.when(e == 0)
    def _():
        acc_ref[...] = _swiglu_bf(h, sgu_ref[...], sdn_ref[...])

    lane = lax.broadcasted_iota(jnp.int32, gates_ref.shape[1:], 1)
    gcol = jnp.sum(jnp.where(lane == e, gates_ref[0], 0.0), axis=1, keepdims=True)
    acc_ref[...] += gcol * _swiglu_bf(h, wgu_ref[0].astype(BF16), wdn_ref[0].astype(BF16))

    @pl.when(e == pl.num_programs(2) - 1)
    def _():
        y_ref[0] = _rms(x1_ref[0] + gf_ref[0] * acc_ref[...], fg_ref[...])


def _moe(x1, h2, gates, gate_f, final_g, sgu_bf, sdn_bf, w_gu, w_down, tm):
    nb, s, d = x1.shape
    n_exp = w_gu.shape[0]
    tok = lambda width: pl.BlockSpec((1, tm, width), lambda b, i, e: (b, i, 0))
    if gate_f.shape[1] == 1:
        gf_spec = pl.BlockSpec((1, 1, d), lambda b, i, e: (b, 0, 0))
    else:
        gf_spec = pl.BlockSpec((1, tm, d), lambda b, i, e: (b, i, 0))
    const = lambda a: pl.BlockSpec(a.shape, lambda b, i, e: (0,) * a.ndim)
    return pl.pallas_call(
        _moe_kernel,
        grid=(nb, s // tm, n_exp),
        in_specs=[tok(d), tok(d), tok(LANES), gf_spec, const(final_g), const(sgu_bf), const(sdn_bf),
                  pl.BlockSpec((1,) + w_gu.shape[1:], lambda b, i, e: (e, 0, 0)),
                  pl.BlockSpec((1,) + w_down.shape[1:], lambda b, i, e: (e, 0, 0))],
        out_specs=tok(d),
        out_shape=jax.ShapeDtypeStruct((nb, s, d), F32),
        scratch_shapes=[pltpu.VMEM((tm, d), F32)],
        compiler_params=_cparams(("parallel", "parallel", "arbitrary")),
        name="moe_final",
    )(x1, h2, gates, gate_f, final_g, sgu_bf, sdn_bf, w_gu, w_down)


def _rope_tables(start_pos, s):
    half = HEAD_DIM // 2
    inv_freq = ROPE_THETA ** (-jnp.arange(half, dtype=F32) / half)
    ang = (start_pos + jnp.arange(s, dtype=jnp.int32)).astype(F32)[:, None] * inv_freq[None, :]
    cos, sin = jnp.cos(ang), jnp.sin(ang)
    reps = LANES // HEAD_DIM
    return (jnp.tile(jnp.concatenate([cos, cos], axis=1), (1, reps)),
            jnp.tile(jnp.concatenate([-sin, sin], axis=1), (1, reps)))


def _layer(x, mods, attend, h0, conv0, start_pos, ts, tm, wts):
    shift_m, scale_m, gate_m, shift_f, scale_f, gate_f = mods
    s = x.shape[1]
    cos, sin = _rope_tables(start_pos, s)
    q, k, v, xl, gl = _inproj(x, shift_m, scale_m, cos, sin, wts["norm_mix_g"], wts["w_in"], ts)
    attn = attend(q, k, v)
    lru, h_last, cbuf = _lru(xl, gl, h0, conv0, wts["conv_w"], wts["conv_b"], wts["wa"],
                             wts["gate_a_b"], wts["wx"], wts["gate_x_b"], wts["lru_lambda"],
                             ts, start_pos)
    x1, h2, gates = _outproj(attn, lru, x, gate_m, shift_f, scale_f, wts["attn_out_g"],
                             wts["lru_out_g"], wts["w_out"], wts["norm_ffn_g"], wts["router_w_t"],
                             wts["router_bias"], ts)
    y = _moe(x1, h2, gates, gate_f, wts["final_g"], wts["shared_w_gu"], wts["shared_w_down"],
             wts["exp_w_gu"], wts["exp_w_down"], tm)
    return y, k, v, h_last, cbuf


def kernel(x_prompt, x_sample, c_prompt, c_sample, cache_k, cache_v, state_h, state_conv, page_table, ada_w, ada_b, norm_mix_g, w_in, conv_w, conv_b, gate_a_w, gate_a_b, gate_x_w, gate_x_b, lru_lambda, attn_out_g, lru_out_g, w_out, norm_ffn_g, router_w, router_bias, exp_w_gu, exp_w_down, shared_w_gu, shared_w_down, final_g):
    depth = ada_w.shape[0]
    assert depth == 1, "single-layer trunk"
    n_prm, seq, d = x_prompt.shape
    n_dec, dec_seq, _ = x_sample.shape
    n_heads = cache_k.shape[3]
    aw = n_heads * HEAD_DIM
    lw = state_h.shape[2]
    past_len = page_table.shape[1] * PAGE_SIZE
    l = 0
    row = lambda a: a.reshape(1, -1)
    wts = dict(
        norm_mix_g=row(norm_mix_g[l]), w_in=w_in[l].astype(BF16), conv_w=conv_w[l], conv_b=conv_b[l],
        wa=_block_diag(gate_a_w[l]).astype(BF16), gate_a_b=gate_a_b[l],
        wx=_block_diag(gate_x_w[l]).astype(BF16), gate_x_b=gate_x_b[l], lru_lambda=lru_lambda[l],
        attn_out_g=row(attn_out_g[l]), lru_out_g=row(lru_out_g[l]), w_out=w_out[l].astype(BF16),
        norm_ffn_g=row(norm_ffn_g[l]), router_w_t=router_w[l].T,
        router_bias=router_bias[l].reshape(-1, 1), final_g=row(final_g),
        shared_w_gu=shared_w_gu[l].astype(BF16), shared_w_down=shared_w_down[l].astype(BF16),
        exp_w_gu=exp_w_gu[l], exp_w_down=exp_w_down[l])

    n_c = n_prm + n_dec
    c_all = jnp.concatenate([c_prompt, c_sample, jnp.zeros((-n_c % SUBLANES, d), F32)], axis=0)
    mod = _ada_mod(c_all, ada_w[l], ada_b[l])
    mods_p = [m[:n_prm, None, :] for m in jnp.split(mod, 6, axis=-1)]
    mods_s = [jnp.repeat(m[n_prm:n_c], dec_seq, axis=0)[None] for m in jnp.split(mod, 6, axis=-1)]

    yp, k1, v1, h1, c1 = _layer(
        x_prompt, mods_p, _moba_prompt, jnp.zeros((n_prm, lw), F32),
        jnp.zeros((n_prm, CONV_W - 1, lw), F32), 0, 512, 1024, wts)

    ck = jnp.transpose(cache_k[l], (0, 2, 3, 1)).reshape(cache_k.shape[1], aw, PAGE_SIZE)
    cv = jnp.transpose(cache_v[l], (0, 2, 3, 1)).reshape(cache_v.shape[1], aw, PAGE_SIZE)

    def attend_sample(q, k, v):
        r = lambda a: a.reshape(n_dec, dec_seq, aw)
        return _moba_sample(r(q), r(k), r(v), ck, cv, page_table).reshape(1, n_dec * dec_seq, aw)

    xs = x_sample.reshape(1, n_dec * dec_seq, d)
    shift_m, scale_m, gate_m, shift_f, scale_f, gate_f = mods_s
    cos, sin = _rope_tables(past_len, dec_seq)
    cos, sin = jnp.tile(cos, (n_dec, 1)), jnp.tile(sin, (n_dec, 1))
    nt = n_dec * dec_seq
    q, k2, v2, xl, gl = _inproj(xs, shift_m, scale_m, cos, sin, wts["norm_mix_g"], wts["w_in"], nt)
    attn = attend_sample(q, k2, v2)
    lru, h2s, c2 = _lru(xl.reshape(n_dec, dec_seq, lw), gl.reshape(n_dec, dec_seq, lw), state_h[l],
                        state_conv[l], wts["conv_w"], wts["conv_b"], wts["wa"], wts["gate_a_b"],
                        wts["wx"], wts["gate_x_b"], wts["lru_lambda"], dec_seq, past_len)
    x1, hh, gates = _outproj(attn, lru.reshape(1, nt, lw), xs, gate_m, shift_f, scale_f,
                             wts["attn_out_g"], wts["lru_out_g"], wts["w_out"], wts["norm_ffn_g"],
                             wts["router_w_t"], wts["router_bias"], nt)
    ys = _moe(x1, hh, gates, gate_f, wts["final_g"], wts["shared_w_gu"], wts["shared_w_down"],
              wts["exp_w_gu"], wts["exp_w_down"], nt)

    return (yp, ys.reshape(n_dec, dec_seq, d),
            k1.reshape(1, n_prm, seq, n_heads, HEAD_DIM), v1.reshape(1, n_prm, seq, n_heads, HEAD_DIM),
            h1.reshape(1, n_prm, lw), c1[None],
            k2.reshape(1, n_dec, dec_seq, n_heads, HEAD_DIM), v2.reshape(1, n_dec, dec_seq, n_heads, HEAD_DIM),
            h2s.reshape(1, n_dec, lw), c2[None])
```

```python
import functools
import math

import jax
import jax.numpy as jnp
from jax import lax
from jax.experimental import pallas as pl
from jax.experimental.pallas import tpu as pltpu

F32 = jnp.float32
BF16 = jnp.bfloat16

HEAD_DIM = 64
LRU_BLOCKS = 8
CONV_W = 4
LRU_C = 8.0
MOBA_BLOCK = 256
MOBA_TOPK = 3
ROPE_THETA = 10000.0
N_GROUPS = 8
TOPK_GROUPS = 4
TOP_K = 8
ROUTED_SCALE = 2.5
EPS = 1e-6
PAGE_SIZE = 128

LANES = 128
SUBLANES = 8
NEG = -1e30
N_CHAINS = 2
VMEM_LIMIT = 56 * 1024 * 1024

NN = (((1,), (0,)), ((), ()))
NT = (((1,), (1,)), ((), ()))


def _cparams(sem):
    return pltpu.CompilerParams(dimension_semantics=sem, vmem_limit_bytes=VMEM_LIMIT)


def _split(x):
    hi = x.astype(BF16)
    lo = (x - hi.astype(F32)).astype(BF16)
    return hi, lo


def _dot3(a, b, dims):
    ah, al = _split(a)
    bh, bl = _split(b)
    d = lambda u, v: lax.dot_general(u, v, dims, preferred_element_type=F32)
    return d(ah, bh) + (d(ah, bl) + d(al, bh))


def _dot(a, b, dims=NN):
    return lax.dot_general(a, b, dims, preferred_element_type=F32)


def _rms(x, g):
    return x * lax.rsqrt(jnp.mean(x * x, axis=-1, keepdims=True) + EPS) * g


def _silu(x):
    return x * jax.nn.sigmoid(x)


def _gelu_tanh(x):
    c = math.sqrt(2.0 / math.pi)
    return 0.5 * x * (1.0 + jnp.tanh(c * (x + 0.044715 * (x * x * x))))


def _ada_kernel(c_ref, w_ref, b_ref, o_ref):
    o_ref[...] = _dot3(_silu(c_ref[...]), w_ref[...], NN) + b_ref[...]


def _ada_mod(c, ada_w, ada_b):
    r, d = c.shape
    n = ada_w.shape[1]
    tn = 1536
    return pl.pallas_call(
        _ada_kernel,
        grid=(n // tn,),
        in_specs=[pl.BlockSpec((r, d), lambda j: (0, 0)),
                  pl.BlockSpec((d, tn), lambda j: (0, j)),
                  pl.BlockSpec((1, tn), lambda j: (0, j))],
        out_specs=pl.BlockSpec((r, tn), lambda j: (0, j)),
        out_shape=jax.ShapeDtypeStruct((r, n), F32),
        compiler_params=_cparams(("parallel",)),
        name="ada_mod",
    )(c, ada_w, ada_b.reshape(1, n))


def _inproj_kernel(x_ref, shift_ref, scale_ref, cos_ref, sin_ref, g_ref, w_ref,
                   q_ref, k_ref, v_ref, xl_ref, gl_ref, *, aw):
    x = x_ref[0]
    h = _rms(x, g_ref[...]) * (1.0 + scale_ref[0]) + shift_ref[0]
    proj = _dot(h.astype(BF16), w_ref[...])
    cos = cos_ref[...]
    sin = sin_ref[...]
    lane = lax.broadcasted_iota(jnp.int32, cos.shape, 1)
    first = (lane % HEAD_DIM) < (HEAD_DIM // 2)

    def rope(t):
        outs = []
        for c in range(aw // LANES):
            tc = t[:, LANES * c:LANES * (c + 1)]
            partner = jnp.where(first, pltpu.roll(tc, LANES - HEAD_DIM // 2, 1),
                                pltpu.roll(tc, HEAD_DIM // 2, 1))
            outs.append(tc * cos + partner * sin)
        return jnp.concatenate(outs, axis=1)

    q_ref[0] = rope(proj[:, 0:aw])
    k_ref[0] = rope(proj[:, aw:2 * aw])
    v_ref[0] = proj[:, 2 * aw:3 * aw]
    lw = (proj.shape[1] - 3 * aw) // 2
    xl_ref[0] = proj[:, 3 * aw:3 * aw + lw]
    gl_ref[0] = proj[:, 3 * aw + lw:]


def _mod_spec(mod, ts):
    if mod.shape[1] == 1:
        return pl.BlockSpec((1, 1, mod.shape[2]), lambda b, i: (b, 0, 0))
    return pl.BlockSpec((1, ts, mod.shape[2]), lambda b, i: (b, i, 0))


def _inproj(x, shift, scale, cos, sin, g, w_bf, ts):
    nb, s, d = x.shape
    aw = 512
    lw = (w_bf.shape[1] - 3 * aw) // 2
    tok = lambda width: pl.BlockSpec((1, ts, width), lambda b, i: (b, i, 0))
    shp = lambda width: jax.ShapeDtypeStruct((nb, s, width), F32)
    return pl.pallas_call(
        functools.partial(_inproj_kernel, aw=aw),
        grid=(nb, s // ts),
        in_specs=[tok(d), _mod_spec(shift, ts), _mod_spec(scale, ts),
                  pl.BlockSpec((ts, LANES), lambda b, i: (i, 0)),
                  pl.BlockSpec((ts, LANES), lambda b, i: (i, 0)),
                  pl.BlockSpec((1, d), lambda b, i: (0, 0)),
                  pl.BlockSpec(w_bf.shape, lambda b, i: (0, 0))],
        out_specs=[tok(aw), tok(aw), tok(aw), tok(lw), tok(lw)],
        out_shape=[shp(aw), shp(aw), shp(aw), shp(lw), shp(lw)],
        compiler_params=_cparams(("parallel", "parallel")),
        name="inproj",
    )(x, shift, scale, cos, sin, g, w_bf)


def _moba_prompt_kernel(q_ref, k_ref, v_ref, o_ref, kbf_ref, vst_ref, km_ref, selb_ref):
    s_len = k_ref.shape[1]
    tq = MOBA_BLOCK
    nblk = s_len // MOBA_BLOCK
    half = lax.broadcasted_iota(jnp.int32, (LANES, tq), 0) < HEAD_DIM
    kbf_ref[...] = k_ref[0].astype(BF16)
    km_ref[...] = jnp.zeros(km_ref.shape, F32)
    for n in range(nblk):
        rows = slice(n * MOBA_BLOCK, (n + 1) * MOBA_BLOCK)
        km_ref[n:n + 1, :] = jnp.mean(k_ref[0, rows, :], axis=0, keepdims=True)
        vt = v_ref[0, rows, :].T
        vst_ref[n] = jnp.concatenate([jnp.where(half, vt, 0.0), jnp.where(half, 0.0, vt)],
                                     axis=1).astype(BF16)
    nrow = km_ref.shape[0]
    n_iota = lax.broadcasted_iota(jnp.int32, (nrow, 2 * tq), 0)
    keyi = lax.broadcasted_iota(jnp.int32, (MOBA_BLOCK, 2 * tq), 0)
    qi = lax.broadcasted_iota(jnp.int32, (MOBA_BLOCK, 2 * tq), 1) % tq

    def stack_heads(pb):
        return jnp.concatenate([pb[:, :tq], pb[:, tq:]], axis=0)

    def per_head(rowvec):
        return jnp.where(half, rowvec[:, :tq], rowvec[:, tq:])

    def qtile(t, carry):
        r0 = pl.multiple_of(t * tq, tq)
        qt = q_ref[0, pl.ds(r0, tq), :].T
        qt2 = jnp.concatenate([jnp.where(half, qt, 0.0), jnp.where(half, 0.0, qt)], axis=1)
        gm = jnp.where(n_iota < t, _dot3(km_ref[...], qt2, NN), -jnp.inf)
        rank = jnp.zeros(gm.shape, jnp.int32)
        for m_ in range(nblk):
            row = gm[m_:m_ + 1, :]
            beats = (row > gm) | ((row == gm) & (m_ < n_iota))
            rank = rank + beats.astype(jnp.int32)
        selb_ref[...] = jnp.where((n_iota < t) & (rank < MOBA_TOPK), 0.0, NEG)
        qb = (qt2 * (HEAD_DIM ** -0.5)).astype(BF16)
        st = jnp.where(keyi <= qi, _dot(kbf_ref[pl.ds(r0, MOBA_BLOCK), :], qb), NEG)
        m = jnp.max(st, axis=0, keepdims=True)
        p = jnp.exp(st - m)
        l = jnp.sum(p, axis=0, keepdims=True)
        acc = _dot(vst_ref[t], stack_heads(p.astype(BF16)))

        def step(j, m, l, acc):
            j0 = pl.multiple_of(j * MOBA_BLOCK, MOBA_BLOCK)
            st = _dot(kbf_ref[pl.ds(j0, MOBA_BLOCK), :], qb) + selb_ref[pl.ds(j, 1), :]
            m_new = jnp.maximum(m, jnp.max(st, axis=0, keepdims=True))
            alpha = jnp.exp(m - m_new)
            p = jnp.exp(st - m_new)
            l = alpha * l + jnp.sum(p, axis=0, keepdims=True)
            acc = per_head(alpha) * acc + _dot(vst_ref[j], stack_heads(p.astype(BF16)))
            return m_new, l, acc

        def body(i, chains):
            return tuple(step(N_CHAINS * i + c, *chains[c]) for c in range(N_CHAINS))

        idle = (jnp.full_like(m, NEG), jnp.zeros_like(l), jnp.zeros_like(acc))
        chains = lax.fori_loop(0, (t + N_CHAINS - 1) // N_CHAINS, body,
                               ((m, l, acc),) + (idle,) * (N_CHAINS - 1))
        m = functools.reduce(jnp.maximum, [c[0] for c in chains])
        w = [jnp.exp(c[0] - m) for c in chains]
        l = sum(wc * c[1] for wc, c in zip(w, chains))
        acc = sum(per_head(wc) * c[2] for wc, c in zip(w, chains))
        o_ref[0, pl.ds(r0, tq), :] = (acc / per_head(l)).T
        return carry

    lax.fori_loop(0, s_len // tq, qtile, 0)


def _moba_prompt(q, k, v):
    nb, s, aw = q.shape
    nblk = s // MOBA_BLOCK
    nrow = -(-nblk // 16) * 16
    spec = pl.BlockSpec((1, s, LANES), lambda b, h: (b, 0, h))
    return pl.pallas_call(
        _moba_prompt_kernel,
        grid=(nb, aw // LANES),
        in_specs=[spec, spec, spec],
        out_specs=spec,
        out_shape=jax.ShapeDtypeStruct((nb, s, aw), F32),
        scratch_shapes=[pltpu.VMEM((s, LANES), BF16),
                        pltpu.VMEM((nblk, LANES, 2 * MOBA_BLOCK), BF16),
                        pltpu.VMEM((nrow, LANES), F32),
                        pltpu.VMEM((nrow, 2 * MOBA_BLOCK), F32)],
        compiler_params=_cparams(("parallel", "parallel")),
        name="moba_prompt",
    )(q, k, v)


def _moba_sample_kernel(pt_ref, q_ref, kn_ref, vn_ref, *rest, n_heads, n_q, bps):
    ppb = MOBA_BLOCK // PAGE_SIZE
    npg = bps * ppb
    k_refs, v_refs = rest[:npg], rest[npg:2 * npg]
    o_ref, qbd_ref, km_ref, m_ref, l_ref, o_scr = rest[2 * npg:]
    step = pl.program_id(1)
    nstep = pl.num_programs(1)
    rows = n_heads * n_q
    aw = q_ref.shape[2]
    rowi = lax.broadcasted_iota(jnp.int32, (rows, aw), 0)
    coli = lax.broadcasted_iota(jnp.int32, (rows, aw), 1)
    diag = (rowi // n_q) == (coli // HEAD_DIM)
    lane = lax.broadcasted_iota(jnp.int32, (rows, LANES), 1)

    @pl.when(step == 0)
    def _():
        qt = jnp.concatenate([q_ref[0]] * n_heads, axis=0)
        qbd_ref[...] = jnp.where(diag, qt, 0.0)
        km_ref[...] = jnp.zeros(km_ref.shape, F32)
        m_ref[...] = jnp.full(m_ref.shape, NEG, F32)
        l_ref[...] = jnp.zeros(l_ref.shape, F32)

    qbd = qbd_ref[...]
    qb = (qbd * (HEAD_DIM ** -0.5)).astype(BF16)
    klane = lax.broadcasted_iota(jnp.int32, km_ref.shape, 1)
    km, mm_, ll_ = km_ref[...], m_ref[...], l_ref[...]
    for c in range(bps):
        j = step * bps + c
        kt = jnp.concatenate([r[0] for r in k_refs[c * ppb:(c + 1) * ppb]], axis=1)
        vt = jnp.concatenate([r[0] for r in v_refs[c * ppb:(c + 1) * ppb]], axis=1)
        km = jnp.where(klane == j, jnp.mean(kt, axis=1, keepdims=True), km)
        s = _dot(qb, kt.astype(BF16))
        mj = jnp.max(s, axis=1, keepdims=True)
        p = jnp.exp(s - mj)
        lj = jnp.sum(p, axis=1, keepdims=True)
        o_scr[j] = _dot(p.astype(BF16), vt.astype(BF16), NT)
        mm_ = jnp.where(lane == j, mj, mm_)
        ll_ = jnp.where(lane == j, lj, ll_)
    km_ref[...], m_ref[...], l_ref[...] = km, mm_, ll_

    @pl.when(step == nstep - 1)
    def _():
        npast = o_scr.shape[0]
        gate = _dot3(qbd, km_ref[...], NN)
        valid = lane < npast
        gm = jnp.where(valid, gate, -jnp.inf)
        rank = jnp.zeros((rows, LANES), jnp.int32)
        for mm in range(npast):
            colv = gm[:, mm:mm + 1]
            beats = (colv > gm) | ((colv == gm) & (mm < lane))
            rank = rank + beats.astype(jnp.int32)
        sel = valid & (rank < MOBA_TOPK)
        so = _dot(qb, kn_ref[0].astype(BF16), NT)
        qi = lax.broadcasted_iota(jnp.int32, (rows, n_q), 0) % n_q
        ki = lax.broadcasted_iota(jnp.int32, (rows, n_q), 1)
        so = jnp.where(ki <= qi, so, NEG)
        mo = jnp.max(so, axis=1, keepdims=True)
        po = jnp.exp(so - mo)
        lo = jnp.sum(po, axis=1, keepdims=True)
        oo = _dot(po.astype(BF16), vn_ref[0].astype(BF16))
        mall = jnp.where(sel, m_ref[...], NEG)
        mx = jnp.maximum(jnp.max(mall, axis=1, keepdims=True), mo)
        w = jnp.where(sel, jnp.exp(mall - mx), 0.0)
        wo = jnp.exp(mo - mx)
        den = jnp.sum(w * l_ref[...], axis=1, keepdims=True) + wo * lo
        num = wo * oo
        for jj in range(npast):
            num = num + w[:, jj:jj + 1] * o_scr[jj]
        res = jnp.where(diag, num / den, 0.0)
        out = res[0:n_q, :]
        for hh in range(1, n_heads):
            out = out + res[hh * n_q:(hh + 1) * n_q, :]
        o_ref[0] = out


def _moba_sample(q, k_new, v_new, cache_kt, cache_vt, page_table):
    nb, n_q, aw = q.shape
    n_heads = aw // HEAD_DIM
    n_pages = page_table.shape[1]
    ppb = MOBA_BLOCK // PAGE_SIZE
    npast = n_pages // ppb
    bps = 2 if npast % 2 == 0 else 1
    npg = bps * ppb
    rows = n_heads * n_q
    tok = pl.BlockSpec((1, n_q, aw), lambda b, j, pt: (b, 0, 0))
    page = lambda o: pl.BlockSpec((1, aw, PAGE_SIZE), lambda b, j, pt: (pt[b, npg * j + o], 0, 0))
    pages = [page(o) for o in range(npg)]
    grid_spec = pltpu.PrefetchScalarGridSpec(
        num_scalar_prefetch=1,
        grid=(nb, npast // bps),
        in_specs=[tok, tok, tok] + pages + pages,
        out_specs=tok,
        scratch_shapes=[pltpu.VMEM((rows, aw), F32), pltpu.VMEM((aw, LANES), F32),
                        pltpu.VMEM((rows, LANES), F32), pltpu.VMEM((rows, LANES), F32),
                        pltpu.VMEM((npast, rows, aw), F32)],
    )
    return pl.pallas_call(
        functools.partial(_moba_sample_kernel, n_heads=n_heads, n_q=n_q, bps=bps),
        grid_spec=grid_spec,
        out_shape=jax.ShapeDtypeStruct((nb, n_q, aw), F32),
        compiler_params=_cparams(("parallel", "arbitrary")),
        name="moba_sample",
    )(page_table, q, k_new, v_new, *([cache_kt] * npg), *([cache_vt] * npg))


def _lru_kernel(xl_ref, gl_ref, h0_ref, c0_ref, cw_ref, cb_ref, wa_ref, ba_ref, wx_ref, bx_ref,
                lam_ref, y_ref, hlast_ref, cbuf_ref, xbuf_ref, a_ref, b_ref, hcar_ref,
                *, ts, start_pos):
    i = pl.program_id(1)
    w = xl_ref.shape[2]
    hist = SUBLANES

    @pl.when(i == 0)
    def _():
        xbuf_ref[0:hist, :] = jnp.zeros((hist, w), F32)
        xbuf_ref[hist - (CONV_W - 1):hist, :] = c0_ref[0]
        hcar_ref[...] = jnp.broadcast_to(h0_ref[0], (SUBLANES, w))

    x = xl_ref[0]
    xbuf_ref[hist:hist + ts, :] = x
    conv = cb_ref[...] + cw_ref[0:1, :] * xbuf_ref[hist - 3:hist - 3 + ts, :]
    conv = conv + cw_ref[1:2, :] * xbuf_ref[hist - 2:hist - 2 + ts, :]
    conv = conv + cw_ref[2:3, :] * xbuf_ref[hist - 1:hist - 1 + ts, :]
    conv = conv + cw_ref[3:4, :] * x
    cbuf_ref[0] = xbuf_ref[hist + ts - (CONV_W - 1):hist + ts, :]
    xbuf_ref[0:hist, :] = xbuf_ref[ts:ts + hist, :]

    cbf = conv.astype(BF16)
    r = jax.nn.sigmoid(_dot(cbf, wa_ref[...]) + ba_ref[...])
    ig = jax.nn.sigmoid(_dot(cbf, wx_ref[...]) + bx_ref[...])
    nl = -lam_ref[...]
    softplus = jnp.maximum(nl, 0.0) + jnp.log1p(jnp.exp(-jnp.abs(nl)))
    log_a = (-LRU_C * r) * softplus
    a = jnp.exp(log_a)
    mult = jnp.sqrt(1.0 - jnp.exp(2.0 * log_a))
    if start_pos == 0:
        reset = (i * ts + lax.broadcasted_iota(jnp.int32, (ts, w), 0)) == 0
        a = jnp.where(reset, 0.0, a)
        mult = jnp.where(reset, 1.0, mult)
    a_ref[...] = a
    b_ref[...] = mult * ig * conv

    rowi = lax.broadcasted_iota(jnp.int32, (SUBLANES, w), 0)

    def group(gi, h):
        o = pl.multiple_of(gi * SUBLANES, SUBLANES)
        a8 = a_ref[pl.ds(o, SUBLANES), :]
        b8 = b_ref[pl.ds(o, SUBLANES), :]
        for d in (1, 2, 4):
            keep = rowi >= d
            a_sh = jnp.where(keep, pltpu.roll(a8, d, 0), 1.0)
            b_sh = jnp.where(keep, pltpu.roll(b8, d, 0), 0.0)
            b8 = a8 * b_sh + b8
            a8 = a8 * a_sh
        h8 = a8 * h + b8
        b_ref[pl.ds(o, SUBLANES), :] = h8
        return jnp.broadcast_to(h8[SUBLANES - 1:SUBLANES, :], (SUBLANES, w))

    h = lax.fori_loop(0, ts // SUBLANES, group, hcar_ref[...])
    hcar_ref[...] = h
    y_ref[0] = b_ref[...] * _gelu_tanh(gl_ref[0])
    hlast_ref[0] = h[0:1, :]


def _block_diag(wb):
    nb, d, _ = wb.shape
    eye = jnp.eye(nb, dtype=wb.dtype)
    return (wb[:, :, None, :] * eye[:, None, :, None]).reshape(nb * d, nb * d)


def _lru(xl, gl, h0, conv0, conv_w, conv_b, wa_bd, ba, wx_bd, bx, lam, ts, start_pos):
    nb, s, w = xl.shape
    tok = pl.BlockSpec((1, ts, w), lambda b, i: (b, i, 0))
    row = pl.BlockSpec((1, w), lambda b, i: (0, 0))
    mat = pl.BlockSpec((w, w), lambda b, i: (0, 0))
    return pl.pallas_call(
        functools.partial(_lru_kernel, ts=ts, start_pos=start_pos),
        grid=(nb, s // ts),
        in_specs=[tok, tok,
                  pl.BlockSpec((1, 1, w), lambda b, i: (b, 0, 0)),
                  pl.BlockSpec((1, CONV_W - 1, w), lambda b, i: (b, 0, 0)),
                  pl.BlockSpec((CONV_W, w), lambda b, i: (0, 0)),
                  row, mat, row, mat, row, row],
        out_specs=[tok,
                   pl.BlockSpec((1, 1, w), lambda b, i: (b, 0, 0)),
                   pl.BlockSpec((1, CONV_W - 1, w), lambda b, i: (b, 0, 0))],
        out_shape=[jax.ShapeDtypeStruct((nb, s, w), F32),
                   jax.ShapeDtypeStruct((nb, 1, w), F32),
                   jax.ShapeDtypeStruct((nb, CONV_W - 1, w), F32)],
        scratch_shapes=[pltpu.VMEM((ts + SUBLANES, w), F32), pltpu.VMEM((ts, w), F32),
                        pltpu.VMEM((ts, w), F32), pltpu.VMEM((SUBLANES, w), F32)],
        compiler_params=_cparams(("parallel", "arbitrary")),
        name="rg_lru",
    )(xl, gl, h0.reshape(nb, 1, w), conv0, conv_w, conv_b.reshape(1, w), wa_bd,
      ba.reshape(1, w), wx_bd, bx.reshape(1, w), lam.reshape(1, w))


def _route(logits_t, bias_col, n_exp, ts):
    scores = jax.nn.sigmoid(logits_t)
    biased = scores + bias_col
    gsz = n_exp // N_GROUPS
    gscore = []
    for g in range(N_GROUPS):
        blk = biased[g * gsz:(g + 1) * gsz, :]
        m1 = jnp.max(blk, axis=0, keepdims=True)
        cnt = jnp.sum((blk == m1).astype(F32), axis=0, keepdims=True)
        m2 = jnp.max(jnp.where(blk < m1, blk, -jnp.inf), axis=0, keepdims=True)
        gscore.append(m1 + jnp.where(cnt >= 2.0, m1, m2))
    cands = []
    for g in range(N_GROUPS):
        rank = jnp.zeros((1, ts), jnp.int32)
        for g2 in range(N_GROUPS):
            if g2 == g:
                continue
            beats = (gscore[g2] > gscore[g]) if g2 > g else (gscore[g2] >= gscore[g])
            rank = rank + beats.astype(jnp.int32)
        blk = biased[g * gsz:(g + 1) * gsz, :]
        cands.append(jnp.where(rank < TOPK_GROUPS, blk, -jnp.inf))
    cand = jnp.concatenate(cands, axis=0)
    e_iota = lax.broadcasted_iota(jnp.int32, (n_exp, ts), 0)
    rank = jnp.zeros((n_exp, ts), jnp.int32)
    for e2 in range(n_exp):
        row = cand[e2:e2 + 1, :]
        beats = (row > cand) | ((row == cand) & (e2 < e_iota))
        rank = rank + beats.astype(jnp.int32)
    w = jnp.where(rank < TOP_K, scores, 0.0)
    gates = ROUTED_SCALE * w / jnp.sum(w, axis=0, keepdims=True)
    e_f = e_iota.astype(F32)
    ids = [jnp.sum(jnp.where(rank == k, e_f, 0.0), axis=0, keepdims=True) for k in range(TOP_K)]
    wts = [jnp.sum(jnp.where(rank == k, gates, 0.0), axis=0, keepdims=True) for k in range(TOP_K)]
    return jnp.concatenate(ids + wts, axis=0)


def _outproj_kernel(attn_ref, lru_ref, x_ref, gm_ref, sf_ref, cf_ref, ag_ref, lg_ref, wo_ref,
                    ng_ref, rw_ref, rb_ref, x1_ref, h2_ref, gates_ref, *, ts):
    aw = attn_ref.shape[2]
    an = _rms(attn_ref[0], ag_ref[...]).astype(BF16)
    ln = _rms(lru_ref[0], lg_ref[...]).astype(BF16)
    mixed = _dot(an, wo_ref[0:aw, :]) + _dot(ln, wo_ref[aw:, :])
    x1 = x_ref[0] + gm_ref[0] * mixed
    x1_ref[0] = x1
    h2 = _rms(x1, ng_ref[...]) * (1.0 + cf_ref[0]) + sf_ref[0]
    h2_ref[0] = h2
    n_exp = rw_ref.shape[0]
    route_t = _route(_dot3(rw_ref[...], h2, NT), rb_ref[...], n_exp, ts)
    route_t = jnp.concatenate([route_t, jnp.zeros((LANES - 2 * TOP_K, ts), F32)], axis=0)
    gates_ref[0] = route_t.T


def _outproj(attn, lru, x, gate_m, shift_f, scale_f, ag, lg, wo_bf, ng, rw_t, rb_col, ts):
    nb, s, d = x.shape
    aw = attn.shape[2]
    n_exp = rw_t.shape[0]
    tok = lambda width: pl.BlockSpec((1, ts, width), lambda b, i: (b, i, 0))
    const = lambda a: pl.BlockSpec(a.shape, lambda b, i: (0,) * a.ndim)
    return pl.pallas_call(
        functools.partial(_outproj_kernel, ts=ts),
        grid=(nb, s // ts),
        in_specs=[tok(aw), tok(lru.shape[2]), tok(d), _mod_spec(gate_m, ts), _mod_spec(shift_f, ts),
                  _mod_spec(scale_f, ts), const(ag), const(lg), const(wo_bf), const(ng),
                  const(rw_t), const(rb_col)],
        out_specs=[tok(d), tok(d), tok(LANES)],
        out_shape=[jax.ShapeDtypeStruct((nb, s, d), F32), jax.ShapeDtypeStruct((nb, s, d), F32),
                   jax.ShapeDtypeStruct((nb, s, LANES), F32)],
        compiler_params=_cparams(("parallel", "parallel")),
        name="outproj_router",
    )(attn, lru, x, gate_m, shift_f, scale_f, ag, lg, wo_bf, ng, rw_t, rb_col)


def _swiglu_bf(h, w_gu, w_down):
    gu = _dot(h, w_gu)
    de = gu.shape[1] // 2
    act = _silu(gu[:, :de]) * gu[:, de:]
    return _dot(act.astype(BF16), w_down)


MOE_TILE = 128
MOE_STRIDE = MOE_TILE + 1
MOE_BATCH = 16
MOE_KEY_SHIFT = 12
SMEM_CHUNK = 1024


def _moe_lists(route, n_blocks, n_exp):
    t = route.shape[0]
    tb = t // n_blocks
    assert tb * n_blocks == t and tb < (1 << MOE_KEY_SHIFT)
    ids = route[:, 0:TOP_K].astype(jnp.int32).reshape(n_blocks, tb, TOP_K)
    wts = route[:, TOP_K:2 * TOP_K].reshape(n_blocks, tb * TOP_K)
    tloc = jnp.arange(tb, dtype=jnp.int32)[None, :, None]
    keys = ((ids << MOE_KEY_SHIFT) | tloc).reshape(n_blocks, tb * TOP_K)
    npad = MOE_TILE - 1
    dummy = (jnp.arange(n_exp, dtype=jnp.int32) << MOE_KEY_SHIFT) | tb
    dummy = jnp.broadcast_to(jnp.repeat(dummy, npad)[None], (n_blocks, n_exp * npad))
    keys = jnp.concatenate([keys, dummy], axis=1)
    wts = jnp.concatenate([wts, jnp.zeros(dummy.shape, F32)], axis=1)
    keys, wts = lax.sort((keys, wts), dimension=1, num_keys=1)
    n = keys.shape[1]
    lp = -(-n // SMEM_CHUNK) * SMEM_CHUNK
    tok8 = (keys & ((1 << MOE_KEY_SHIFT) - 1)) * SUBLANES
    tok8 = jnp.pad(tok8, ((0, 0), (0, lp - n))).reshape(-1)
    wts = jnp.pad(wts, ((0, 0), (0, lp - n))).reshape(-1)
    cnt = jnp.sum((ids[..., None] == jnp.arange(n_exp, dtype=jnp.int32)).astype(jnp.int32),
                  axis=(1, 2))
    start = jnp.cumsum(cnt + npad, axis=1) - (cnt + npad)
    ntile = (cnt + npad) // MOE_TILE
    return tok8, wts, start.astype(jnp.int32), ntile.astype(jnp.int32), lp


def _moe_sparse_kernel(start_ref, ntile_ref, tok_hbm, wts_hbm, x_ref, wgu_ref, wdn_ref, out_ref,
                       tok_smem, wts_smem, xt_ref, yt_ref, sem):
    b, e = pl.program_id(0), pl.program_id(1)
    lp = tok_smem.shape[0]
    m, s = MOE_TILE, MOE_STRIDE
    nch = SUBLANES

    @pl.when(e == 0)
    def _():
        off = pl.multiple_of(b * lp, SMEM_CHUNK)
        cp_t = pltpu.make_async_copy(tok_hbm.at[pl.ds(off, lp)], tok_smem, sem.at[0])
        cp_w = pltpu.make_async_copy(wts_hbm.at[pl.ds(off, lp)], wts_smem, sem.at[1])
        cp_t.start()
        cp_w.start()
        out_ref[...] = jnp.zeros(out_ref.shape, F32)
        cp_t.wait()
        cp_w.wait()

    base = start_ref[b, e]
    wgu = wgu_ref[0]
    wdn = wdn_ref[0]

    def tile(i, carry):
        r0 = base + i * m
        for mi in range(m):
            t8 = pl.multiple_of(tok_smem[r0 + mi], SUBLANES)
            xt_ref[pl.ds(mi, SUBLANES, stride=s), :] = x_ref[pl.ds(t8, SUBLANES), :]
        x = jnp.concatenate([xt_ref[pl.ds(j * s, m), :] for j in range(nch)], axis=1)
        y = _swiglu_bf(x.astype(BF16), wgu, wdn)
        for j in range(nch):
            yt_ref[pl.ds(j * s, m), :] = y[:, j * LANES:(j + 1) * LANES]
        for g0 in range(0, m, MOE_BATCH):
            upd = []
            for mi in range(g0, g0 + MOE_BATCH):
                t8 = pl.multiple_of(tok_smem[r0 + mi], SUBLANES)
                slab = yt_ref[pl.ds(mi, SUBLANES, stride=s), :]
                upd.append((t8, out_ref[pl.ds(t8, SUBLANES), :] + wts_smem[r0 + mi] * slab))
            for t8, v in upd:
                out_ref[pl.ds(t8, SUBLANES), :] = v
        return carry

    lax.fori_loop(0, ntile_ref[b, e], tile, 0)


def _moe_sparse(x_slab, tok8, wts, start, ntile, lp, w_gu_bf, w_down_bf):
    n_blocks, rows, _ = x_slab.shape
    n_exp = w_gu_bf.shape[0]
    blk = pl.BlockSpec((None, rows, LANES), lambda b, e, *_: (b, 0, 0))
    grid_spec = pltpu.PrefetchScalarGridSpec(
        num_scalar_prefetch=2,
        grid=(n_blocks, n_exp),
        in_specs=[pl.BlockSpec(memory_space=pl.ANY), pl.BlockSpec(memory_space=pl.ANY), blk,
                  pl.BlockSpec((1,) + w_gu_bf.shape[1:], lambda b, e, *_: (e, 0, 0)),
                  pl.BlockSpec((1,) + w_down_bf.shape[1:], lambda b, e, *_: (e, 0, 0))],
        out_specs=blk,
        scratch_shapes=[pltpu.SMEM((lp,), jnp.int32), pltpu.SMEM((lp,), F32),
                        pltpu.VMEM((SUBLANES * MOE_STRIDE, LANES), F32),
                        pltpu.VMEM((SUBLANES * MOE_STRIDE, LANES), F32),
                        pltpu.SemaphoreType.DMA((2,))],
    )
    return pl.pallas_call(
        _moe_sparse_kernel,
        grid_spec=grid_spec,
        out_shape=jax.ShapeDtypeStruct(x_slab.shape, F32),
        compiler_params=_cparams(("parallel", "arbitrary")),
        name="moe_routed",
    )(start, ntile, tok8, wts, x_slab, w_gu_bf, w_down_bf)


def _final_kernel(x1_ref, h2_ref, moe_ref, gf_ref, fg_ref, sgu_ref, sdn_ref, y_ref):
    f = _swiglu_bf(h2_ref[0].astype(BF16), sgu_ref[...], sdn_ref[...]) + moe_ref[...]
    y_ref[0] = _rms(x1_ref[0] + gf_ref[0] * f, fg_ref[...])


def _final(x1, h2, moe_all, row0, gate_f, final_g, sgu_bf, sdn_bf, tm):
    nb, s, d = x1.shape
    nt = s // tm
    tok = pl.BlockSpec((1, tm, d), lambda b, i: (b, i, 0))
    const = lambda a: pl.BlockSpec(a.shape, lambda b, i: (0,) * a.ndim)
    return pl.pallas_call(
        _final_kernel,
        grid=(nb, nt),
        in_specs=[tok, tok, pl.BlockSpec((tm, d), lambda b, i: (row0 // tm + b * nt + i, 0)),
                  _mod_spec(gate_f, tm), const(final_g), const(sgu_bf), const(sdn_bf)],
        out_specs=tok,
        out_shape=jax.ShapeDtypeStruct((nb, s, d), F32),
        compiler_params=_cparams(("parallel", "parallel")),
        name="shared_final",
    )(x1, h2, moe_all, gate_f, final_g, sgu_bf, sdn_bf)


def _rope_tables(start_pos, s):
    half = HEAD_DIM // 2
    inv_freq = ROPE_THETA ** (-jnp.arange(half, dtype=F32) / half)
    ang = (start_pos + jnp.arange(s, dtype=jnp.int32)).astype(F32)[:, None] * inv_freq[None, :]
    cos, sin = jnp.cos(ang), jnp.sin(ang)
    reps = LANES // HEAD_DIM
    return (jnp.tile(jnp.concatenate([cos, cos], axis=1), (1, reps)),
            jnp.tile(jnp.concatenate([-sin, sin], axis=1), (1, reps)))


def _mixer(x, mods, attend, h0, conv0, start_pos, ts, wts):
    shift_m, scale_m, gate_m, shift_f, scale_f, _ = mods
    s = x.shape[1]
    cos, sin = _rope_tables(start_pos, s)
    q, k, v, xl, gl = _inproj(x, shift_m, scale_m, cos, sin, wts["norm_mix_g"], wts["w_in"], ts)
    attn = attend(q, k, v)
    lru, h_last, cbuf = _lru(xl, gl, h0, conv0, wts["conv_w"], wts["conv_b"], wts["wa"],
                             wts["gate_a_b"], wts["wx"], wts["gate_x_b"], wts["lru_lambda"],
                             ts, start_pos)
    x1, h2, route = _outproj(attn, lru, x, gate_m, shift_f, scale_f, wts["attn_out_g"],
                             wts["lru_out_g"], wts["w_out"], wts["norm_ffn_g"], wts["router_w_t"],
                             wts["router_bias"], ts)
    return x1, h2, route, k, v, h_last, cbuf


def _moe_blocks(t):
    max_tokens = 2560
    return next(n for n in range(1, t + 1) if t % n == 0 and t // n <= max_tokens)


def kernel(x_prompt, x_sample, c_prompt, c_sample, cache_k, cache_v, state_h, state_conv, page_table, ada_w, ada_b, norm_mix_g, w_in, conv_w, conv_b, gate_a_w, gate_a_b, gate_x_w, gate_x_b, lru_lambda, attn_out_g, lru_out_g, w_out, norm_ffn_g, router_w, router_bias, exp_w_gu, exp_w_down, shared_w_gu, shared_w_down, final_g):
    depth = ada_w.shape[0]
    assert depth == 1, "single-layer trunk"
    n_prm, seq, d = x_prompt.shape
    n_dec, dec_seq, _ = x_sample.shape
    n_heads = cache_k.shape[3]
    aw = n_heads * HEAD_DIM
    lw = state_h.shape[2]
    past_len = page_table.shape[1] * PAGE_SIZE
    l = 0
    row = lambda a: a.reshape(1, -1)
    wts = dict(
        norm_mix_g=row(norm_mix_g[l]), w_in=w_in[l].astype(BF16), conv_w=conv_w[l], conv_b=conv_b[l],
        wa=_block_diag(gate_a_w[l]).astype(BF16), gate_a_b=gate_a_b[l],
        wx=_block_diag(gate_x_w[l]).astype(BF16), gate_x_b=gate_x_b[l], lru_lambda=lru_lambda[l],
        attn_out_g=row(attn_out_g[l]), lru_out_g=row(lru_out_g[l]), w_out=w_out[l].astype(BF16),
        norm_ffn_g=row(norm_ffn_g[l]), router_w_t=router_w[l].T,
        router_bias=router_bias[l].reshape(-1, 1), final_g=row(final_g),
        shared_w_gu=shared_w_gu[l].astype(BF16), shared_w_down=shared_w_down[l].astype(BF16))

    n_c = n_prm + n_dec
    c_all = jnp.concatenate([c_prompt, c_sample, jnp.zeros((-n_c % SUBLANES, d), F32)], axis=0)
    mod = _ada_mod(c_all, ada_w[l], ada_b[l])
    mods_p = [m[:n_prm, None, :] for m in jnp.split(mod, 6, axis=-1)]
    mods_s = [jnp.repeat(m[n_prm:n_c], dec_seq, axis=0)[None] for m in jnp.split(mod, 6, axis=-1)]

    x1p, h2p, route_p, k1, v1, h1, c1 = _mixer(
        x_prompt, mods_p, _moba_prompt, jnp.zeros((n_prm, lw), F32),
        jnp.zeros((n_prm, CONV_W - 1, lw), F32), 0, 512, wts)

    ck = jnp.transpose(cache_k[l], (0, 2, 3, 1)).reshape(cache_k.shape[1], aw, PAGE_SIZE)
    cv = jnp.transpose(cache_v[l], (0, 2, 3, 1)).reshape(cache_v.shape[1], aw, PAGE_SIZE)

    def attend_sample(q, k, v):
        r = lambda a: a.reshape(n_dec, dec_seq, aw)
        return _moba_sample(r(q), r(k), r(v), ck, cv, page_table).reshape(1, n_dec * dec_seq, aw)

    xs = x_sample.reshape(1, n_dec * dec_seq, d)
    shift_m, scale_m, gate_m, shift_f, scale_f, gate_f = mods_s
    cos, sin = _rope_tables(past_len, dec_seq)
    cos, sin = jnp.tile(cos, (n_dec, 1)), jnp.tile(sin, (n_dec, 1))
    nt = n_dec * dec_seq
    q, k2, v2, xl, gl = _inproj(xs, shift_m, scale_m, cos, sin, wts["norm_mix_g"], wts["w_in"], nt)
    attn = attend_sample(q, k2, v2)
    lru, h2s, c2 = _lru(xl.reshape(n_dec, dec_seq, lw), gl.reshape(n_dec, dec_seq, lw), state_h[l],
                        state_conv[l], wts["conv_w"], wts["conv_b"], wts["wa"], wts["gate_a_b"],
                        wts["wx"], wts["gate_x_b"], wts["lru_lambda"], dec_seq, past_len)
    x1s, h2s_, route_s = _outproj(attn, lru.reshape(1, nt, lw), xs, gate_m, shift_f, scale_f,
                                  wts["attn_out_g"], wts["lru_out_g"], wts["w_out"],
                                  wts["norm_ffn_g"], wts["router_w_t"], wts["router_bias"], nt)

    n_exp = exp_w_gu.shape[1]
    t_all = n_prm * seq + nt
    n_blk = _moe_blocks(t_all)
    tb = t_all // n_blk
    route = jnp.concatenate([route_p.reshape(-1, LANES), route_s.reshape(-1, LANES)], axis=0)
    tok8, gw, start, ntile, lp = _moe_lists(route, n_blk, n_exp)
    h_all = jnp.concatenate([h2p.reshape(-1, d), h2s_.reshape(-1, d)], axis=0).reshape(n_blk, tb, d)
    x_slab = jnp.pad(h_all, ((0, 0), (0, 1), (0, 0))).reshape(n_blk, (tb + 1) * d // LANES, LANES)
    moe_slab = _moe_sparse(x_slab, tok8, gw, start, ntile, lp,
                           exp_w_gu[l].astype(BF16), exp_w_down[l].astype(BF16))
    moe_all = moe_slab.reshape(n_blk, tb + 1, d)[:, :tb].reshape(t_all, d)
    fin = lambda x1, h2, row0, gf, tm: _final(x1, h2, moe_all, row0, gf, wts["final_g"],
                                              wts["shared_w_gu"], wts["shared_w_down"], tm)
    yp = fin(x1p, h2p, 0, mods_p[5], 512)
    ys = fin(x1s, h2s_, n_prm * seq, gate_f, nt)

    return (yp, ys.reshape(n_dec, dec_seq, d),
            k1.reshape(1, n_prm, seq, n_heads, HEAD_DIM), v1.reshape(1, n_prm, seq, n_heads, HEAD_DIM),
            h1.reshape(1, n_prm, lw), c1[None],
            k2.reshape(1, n_dec, dec_seq, n_heads, HEAD_DIM), v2.reshape(1, n_dec, dec_seq, n_heads, HEAD_DIM),
            h2s.reshape(1, n_dec, lw), c2[None])
```

```python
import functools
import math

import jax
import jax.numpy as jnp
from jax import lax
from jax.experimental import pallas as pl
from jax.experimental.pallas import tpu as pltpu

F32 = jnp.float32
BF16 = jnp.bfloat16

HEAD_DIM = 64
LRU_BLOCKS = 8
CONV_W = 4
LRU_C = 8.0
MOBA_BLOCK = 256
MOBA_TOPK = 3
ROPE_THETA = 10000.0
N_GROUPS = 8
TOPK_GROUPS = 4
TOP_K = 8
ROUTED_SCALE = 2.5
EPS = 1e-6
PAGE_SIZE = 128

LANES = 128
SUBLANES = 8
NEG = -1e30
VMEM_LIMIT = 56 * 1024 * 1024

NN = (((1,), (0,)), ((), ()))
NT = (((1,), (1,)), ((), ()))


def _cparams(sem):
    return pltpu.CompilerParams(dimension_semantics=sem, vmem_limit_bytes=VMEM_LIMIT)


def _split(x):
    hi = x.astype(BF16)
    lo = (x - hi.astype(F32)).astype(BF16)
    return hi, lo


def _dot3(a, b, dims):
    ah, al = _split(a)
    bh, bl = _split(b)
    d = lambda u, v: lax.dot_general(u, v, dims, preferred_element_type=F32)
    return d(ah, bh) + (d(ah, bl) + d(al, bh))


def _dot(a, b, dims=NN):
    return lax.dot_general(a, b, dims, preferred_element_type=F32)


def _rms(x, g):
    return x * lax.rsqrt(jnp.mean(x * x, axis=-1, keepdims=True) + EPS) * g


def _silu(x):
    return x * jax.nn.sigmoid(x)


def _gelu_tanh(x):
    c = math.sqrt(2.0 / math.pi)
    return 0.5 * x * (1.0 + jnp.tanh(c * (x + 0.044715 * (x * x * x))))


def _ada_kernel(c_ref, w_ref, b_ref, o_ref):
    o_ref[...] = _dot3(_silu(c_ref[...]), w_ref[...], NN) + b_ref[...]


def _ada_mod(c, ada_w, ada_b):
    r, d = c.shape
    n = ada_w.shape[1]
    tn = 1536
    return pl.pallas_call(
        _ada_kernel,
        grid=(n // tn,),
        in_specs=[pl.BlockSpec((r, d), lambda j: (0, 0)),
                  pl.BlockSpec((d, tn), lambda j: (0, j)),
                  pl.BlockSpec((1, tn), lambda j: (0, j))],
        out_specs=pl.BlockSpec((r, tn), lambda j: (0, j)),
        out_shape=jax.ShapeDtypeStruct((r, n), F32),
        compiler_params=_cparams(("parallel",)),
        name="ada_mod",
    )(c, ada_w, ada_b.reshape(1, n))


def _inproj_kernel(x_ref, shift_ref, scale_ref, cos_ref, sin_ref, g_ref, w_ref,
                   q_ref, k_ref, v_ref, xl_ref, gl_ref, *, aw):
    x = x_ref[0]
    h = _rms(x, g_ref[...]) * (1.0 + scale_ref[0]) + shift_ref[0]
    proj = _dot(h.astype(BF16), w_ref[...])
    cos = cos_ref[...]
    sin = sin_ref[...]
    lane = lax.broadcasted_iota(jnp.int32, cos.shape, 1)
    first = (lane % HEAD_DIM) < (HEAD_DIM // 2)

    def rope(t):
        outs = []
        for c in range(aw // LANES):
            tc = t[:, LANES * c:LANES * (c + 1)]
            partner = jnp.where(first, pltpu.roll(tc, LANES - HEAD_DIM // 2, 1),
                                pltpu.roll(tc, HEAD_DIM // 2, 1))
            outs.append(tc * cos + partner * sin)
        return jnp.concatenate(outs, axis=1)

    q_ref[0] = rope(proj[:, 0:aw])
    k_ref[0] = rope(proj[:, aw:2 * aw])
    v_ref[0] = proj[:, 2 * aw:3 * aw]
    lw = (proj.shape[1] - 3 * aw) // 2
    xl_ref[0] = proj[:, 3 * aw:3 * aw + lw]
    gl_ref[0] = proj[:, 3 * aw + lw:]


def _mod_spec(mod, ts):
    if mod.shape[1] == 1:
        return pl.BlockSpec((1, 1, mod.shape[2]), lambda b, i: (b, 0, 0))
    return pl.BlockSpec((1, ts, mod.shape[2]), lambda b, i: (b, i, 0))


def _inproj(x, shift, scale, cos, sin, g, w_bf, ts):
    nb, s, d = x.shape
    aw = 512
    lw = (w_bf.shape[1] - 3 * aw) // 2
    tok = lambda width: pl.BlockSpec((1, ts, width), lambda b, i: (b, i, 0))
    shp = lambda width: jax.ShapeDtypeStruct((nb, s, width), F32)
    return pl.pallas_call(
        functools.partial(_inproj_kernel, aw=aw),
        grid=(nb, s // ts),
        in_specs=[tok(d), _mod_spec(shift, ts), _mod_spec(scale, ts),
                  pl.BlockSpec((ts, LANES), lambda b, i: (i, 0)),
                  pl.BlockSpec((ts, LANES), lambda b, i: (i, 0)),
                  pl.BlockSpec((1, d), lambda b, i: (0, 0)),
                  pl.BlockSpec(w_bf.shape, lambda b, i: (0, 0))],
        out_specs=[tok(aw), tok(aw), tok(aw), tok(lw), tok(lw)],
        out_shape=[shp(aw), shp(aw), shp(aw), shp(lw), shp(lw)],
        compiler_params=_cparams(("parallel", "parallel")),
        name="inproj",
    )(x, shift, scale, cos, sin, g, w_bf)


def _moba_prompt_kernel(q_ref, k_ref, v_ref, o_ref, kbf_ref, vst_ref, km_ref, selb_ref,
                        sa_ref, sb_ref):
    s_len = k_ref.shape[1]
    tq = MOBA_BLOCK
    nblk = s_len // MOBA_BLOCK
    half = lax.broadcasted_iota(jnp.int32, (LANES, tq), 0) < HEAD_DIM
    kbf_ref[...] = k_ref[0].astype(BF16)
    km_ref[...] = jnp.zeros(km_ref.shape, F32)
    for n in range(nblk):
        rows = slice(n * MOBA_BLOCK, (n + 1) * MOBA_BLOCK)
        km_ref[n:n + 1, :] = jnp.mean(k_ref[0, rows, :], axis=0, keepdims=True)
        vt = v_ref[0, rows, :].T
        vst_ref[n] = jnp.concatenate([jnp.where(half, vt, 0.0), jnp.where(half, 0.0, vt)],
                                     axis=1).astype(BF16)
    nrow = km_ref.shape[0]
    n_iota = lax.broadcasted_iota(jnp.int32, (nrow, 2 * tq), 0)
    keyi = lax.broadcasted_iota(jnp.int32, (MOBA_BLOCK, 2 * tq), 0)
    qi = lax.broadcasted_iota(jnp.int32, (MOBA_BLOCK, 2 * tq), 1) % tq

    def stack_heads(pb):
        return jnp.concatenate([pb[:, :tq], pb[:, tq:]], axis=0)

    def per_head(rowvec):
        return jnp.where(half, rowvec[:, :tq], rowvec[:, tq:])

    def qtile(t, carry):
        r0 = pl.multiple_of(t * tq, tq)
        qt = q_ref[0, pl.ds(r0, tq), :].T
        qt2 = jnp.concatenate([jnp.where(half, qt, 0.0), jnp.where(half, 0.0, qt)], axis=1)
        gm = jnp.where(n_iota < t, _dot3(km_ref[...], qt2, NN), -jnp.inf)
        rank = jnp.zeros(gm.shape, jnp.int32)
        for m_ in range(nblk):
            row = gm[m_:m_ + 1, :]
            beats = (row > gm) | ((row == gm) & (m_ < n_iota))
            rank = rank + beats.astype(jnp.int32)
        selb_ref[...] = jnp.where((n_iota < t) & (rank < MOBA_TOPK), 0.0, NEG)
        qb = (qt2 * (HEAD_DIM ** -0.5)).astype(BF16)
        st = jnp.where(keyi <= qi, _dot(kbf_ref[pl.ds(r0, MOBA_BLOCK), :], qb), NEG)
        m = jnp.max(st, axis=0, keepdims=True)
        p = jnp.exp(st - m)
        l = jnp.sum(p, axis=0, keepdims=True)
        acc = _dot(vst_ref[t], stack_heads(p.astype(BF16)))

        def scores(j, dst):
            j = jnp.minimum(j, nblk - 1)
            j0 = pl.multiple_of(j * MOBA_BLOCK, MOBA_BLOCK)
            dst[...] = _dot(kbf_ref[pl.ds(j0, MOBA_BLOCK), :], qb) + selb_ref[pl.ds(j, 1), :]

        def attend(src, j, m, l, acc):
            st = src[...]
            m_new = jnp.maximum(m, jnp.max(st, axis=0, keepdims=True))
            alpha = jnp.exp(m - m_new)
            p = jnp.exp(st - m_new)
            l = alpha * l + jnp.sum(p, axis=0, keepdims=True)
            acc = per_head(alpha) * acc + _dot(vst_ref[j], stack_heads(p.astype(BF16)))
            return m_new, l, acc

        scores(0, sa_ref)

        def body(i, c):
            scores(2 * i + 1, sb_ref)
            c = attend(sa_ref, 2 * i, *c)
            scores(2 * i + 2, sa_ref)
            return attend(sb_ref, jnp.minimum(2 * i + 1, nblk - 1), *c)

        m, l, acc = lax.fori_loop(0, (t + 1) // 2, body, (m, l, acc))
        o_ref[0, pl.ds(r0, tq), :] = (acc / per_head(l)).T
        return carry

    lax.fori_loop(0, s_len // tq, qtile, 0)


def _moba_prompt(q, k, v):
    nb, s, aw = q.shape
    nblk = s // MOBA_BLOCK
    nrow = -(-nblk // 16) * 16
    spec = pl.BlockSpec((1, s, LANES), lambda b, h: (b, 0, h))
    return pl.pallas_call(
        _moba_prompt_kernel,
        grid=(nb, aw // LANES),
        in_specs=[spec, spec, spec],
        out_specs=spec,
        out_shape=jax.ShapeDtypeStruct((nb, s, aw), F32),
        scratch_shapes=[pltpu.VMEM((s, LANES), BF16),
                        pltpu.VMEM((nblk, LANES, 2 * MOBA_BLOCK), BF16),
                        pltpu.VMEM((nrow, LANES), F32),
                        pltpu.VMEM((nrow, 2 * MOBA_BLOCK), F32),
                        pltpu.VMEM((MOBA_BLOCK, 2 * MOBA_BLOCK), F32),
                        pltpu.VMEM((MOBA_BLOCK, 2 * MOBA_BLOCK), F32)],
        compiler_params=_cparams(("parallel", "parallel")),
        name="moba_prompt",
    )(q, k, v)


def _moba_sample_kernel(pt_ref, q_ref, kn_ref, vn_ref, *rest, n_heads, n_q, bps):
    ppb = MOBA_BLOCK // PAGE_SIZE
    npg = bps * ppb
    k_refs, v_refs = rest[:npg], rest[npg:2 * npg]
    o_ref, qbd_ref, km_ref, m_ref, l_ref, o_scr = rest[2 * npg:]
    step = pl.program_id(1)
    nstep = pl.num_programs(1)
    rows = n_heads * n_q
    aw = q_ref.shape[2]
    rowi = lax.broadcasted_iota(jnp.int32, (rows, aw), 0)
    coli = lax.broadcasted_iota(jnp.int32, (rows, aw), 1)
    diag = (rowi // n_q) == (coli // HEAD_DIM)
    lane = lax.broadcasted_iota(jnp.int32, (rows, LANES), 1)

    @pl.when(step == 0)
    def _():
        qt = jnp.concatenate([q_ref[0]] * n_heads, axis=0)
        qbd_ref[...] = jnp.where(diag, qt, 0.0)
        km_ref[...] = jnp.zeros(km_ref.shape, F32)
        m_ref[...] = jnp.full(m_ref.shape, NEG, F32)
        l_ref[...] = jnp.zeros(l_ref.shape, F32)

    qbd = qbd_ref[...]
    qb = (qbd * (HEAD_DIM ** -0.5)).astype(BF16)
    klane = lax.broadcasted_iota(jnp.int32, km_ref.shape, 1)
    km, mm_, ll_ = km_ref[...], m_ref[...], l_ref[...]
    for c in range(bps):
        j = step * bps + c
        kt = jnp.concatenate([r[0] for r in k_refs[c * ppb:(c + 1) * ppb]], axis=1)
        vt = jnp.concatenate([r[0] for r in v_refs[c * ppb:(c + 1) * ppb]], axis=1)
        km = jnp.where(klane == j, jnp.mean(kt, axis=1, keepdims=True), km)
        s = _dot(qb, kt.astype(BF16))
        mj = jnp.max(s, axis=1, keepdims=True)
        p = jnp.exp(s - mj)
        lj = jnp.sum(p, axis=1, keepdims=True)
        o_scr[j] = _dot(p.astype(BF16), vt.astype(BF16), NT)
        mm_ = jnp.where(lane == j, mj, mm_)
        ll_ = jnp.where(lane == j, lj, ll_)
    km_ref[...], m_ref[...], l_ref[...] = km, mm_, ll_

    @pl.when(step == nstep - 1)
    def _():
        npast = o_scr.shape[0]
        gate = _dot3(qbd, km_ref[...], NN)
        valid = lane < npast
        gm = jnp.where(valid, gate, -jnp.inf)
        rank = jnp.zeros((rows, LANES), jnp.int32)
        for mm in range(npast):
            colv = gm[:, mm:mm + 1]
            beats = (colv > gm) | ((colv == gm) & (mm < lane))
            rank = rank + beats.astype(jnp.int32)
        sel = valid & (rank < MOBA_TOPK)
        so = _dot(qb, kn_ref[0].astype(BF16), NT)
        qi = lax.broadcasted_iota(jnp.int32, (rows, n_q), 0) % n_q
        ki = lax.broadcasted_iota(jnp.int32, (rows, n_q), 1)
        so = jnp.where(ki <= qi, so, NEG)
        mo = jnp.max(so, axis=1, keepdims=True)
        po = jnp.exp(so - mo)
        lo = jnp.sum(po, axis=1, keepdims=True)
        oo = _dot(po.astype(BF16), vn_ref[0].astype(BF16))
        mall = jnp.where(sel, m_ref[...], NEG)
        mx = jnp.maximum(jnp.max(mall, axis=1, keepdims=True), mo)
        w = jnp.where(sel, jnp.exp(mall - mx), 0.0)
        wo = jnp.exp(mo - mx)
        den = jnp.sum(w * l_ref[...], axis=1, keepdims=True) + wo * lo
        num = wo * oo
        for jj in range(npast):
            num = num + w[:, jj:jj + 1] * o_scr[jj]
        res = jnp.where(diag, num / den, 0.0)
        out = res[0:n_q, :]
        for hh in range(1, n_heads):
            out = out + res[hh * n_q:(hh + 1) * n_q, :]
        o_ref[0] = out


def _moba_sample(q, k_new, v_new, cache_kt, cache_vt, page_table):
    nb, n_q, aw = q.shape
    n_heads = aw // HEAD_DIM
    n_pages = page_table.shape[1]
    ppb = MOBA_BLOCK // PAGE_SIZE
    npast = n_pages // ppb
    bps = 2 if npast % 2 == 0 else 1
    npg = bps * ppb
    rows = n_heads * n_q
    tok = pl.BlockSpec((1, n_q, aw), lambda b, j, pt: (b, 0, 0))
    page = lambda o: pl.BlockSpec((1, aw, PAGE_SIZE), lambda b, j, pt: (pt[b, npg * j + o], 0, 0))
    pages = [page(o) for o in range(npg)]
    grid_spec = pltpu.PrefetchScalarGridSpec(
        num_scalar_prefetch=1,
        grid=(nb, npast // bps),
        in_specs=[tok, tok, tok] + pages + pages,
        out_specs=tok,
        scratch_shapes=[pltpu.VMEM((rows, aw), F32), pltpu.VMEM((aw, LANES), F32),
                        pltpu.VMEM((rows, LANES), F32), pltpu.VMEM((rows, LANES), F32),
                        pltpu.VMEM((npast, rows, aw), F32)],
    )
    return pl.pallas_call(
        functools.partial(_moba_sample_kernel, n_heads=n_heads, n_q=n_q, bps=bps),
        grid_spec=grid_spec,
        out_shape=jax.ShapeDtypeStruct((nb, n_q, aw), F32),
        compiler_params=_cparams(("parallel", "arbitrary")),
        name="moba_sample",
    )(page_table, q, k_new, v_new, *([cache_kt] * npg), *([cache_vt] * npg))


def _lru_kernel(xl_ref, gl_ref, h0_ref, c0_ref, cw_ref, cb_ref, wa_ref, ba_ref, wx_ref, bx_ref,
                lam_ref, y_ref, hlast_ref, cbuf_ref, xbuf_ref, a_ref, b_ref, hcar_ref,
                *, ts, start_pos):
    i = pl.program_id(1)
    w = xl_ref.shape[2]
    hist = SUBLANES

    @pl.when(i == 0)
    def _():
        xbuf_ref[0:hist, :] = jnp.zeros((hist, w), F32)
        xbuf_ref[hist - (CONV_W - 1):hist, :] = c0_ref[0]
        hcar_ref[...] = jnp.broadcast_to(h0_ref[0], (SUBLANES, w))

    x = xl_ref[0]
    xbuf_ref[hist:hist + ts, :] = x
    conv = cb_ref[...] + cw_ref[0:1, :] * xbuf_ref[hist - 3:hist - 3 + ts, :]
    conv = conv + cw_ref[1:2, :] * xbuf_ref[hist - 2:hist - 2 + ts, :]
    conv = conv + cw_ref[2:3, :] * xbuf_ref[hist - 1:hist - 1 + ts, :]
    conv = conv + cw_ref[3:4, :] * x
    cbuf_ref[0] = xbuf_ref[hist + ts - (CONV_W - 1):hist + ts, :]
    xbuf_ref[0:hist, :] = xbuf_ref[ts:ts + hist, :]

    cbf = conv.astype(BF16)
    r = jax.nn.sigmoid(_dot(cbf, wa_ref[...]) + ba_ref[...])
    ig = jax.nn.sigmoid(_dot(cbf, wx_ref[...]) + bx_ref[...])
    nl = -lam_ref[...]
    softplus = jnp.maximum(nl, 0.0) + jnp.log1p(jnp.exp(-jnp.abs(nl)))
    log_a = (-LRU_C * r) * softplus
    a = jnp.exp(log_a)
    mult = jnp.sqrt(1.0 - jnp.exp(2.0 * log_a))
    if start_pos == 0:
        reset = (i * ts + lax.broadcasted_iota(jnp.int32, (ts, w), 0)) == 0
        a = jnp.where(reset, 0.0, a)
        mult = jnp.where(reset, 1.0, mult)
    a_ref[...] = a
    b_ref[...] = mult * ig * conv

    rowi = lax.broadcasted_iota(jnp.int32, (SUBLANES, w), 0)

    def group(gi, h):
        o = pl.multiple_of(gi * SUBLANES, SUBLANES)
        a8 = a_ref[pl.ds(o, SUBLANES), :]
        b8 = b_ref[pl.ds(o, SUBLANES), :]
        for d in (1, 2, 4):
            keep = rowi >= d
            a_sh = jnp.where(keep, pltpu.roll(a8, d, 0), 1.0)
            b_sh = jnp.where(keep, pltpu.roll(b8, d, 0), 0.0)
            b8 = a8 * b_sh + b8
            a8 = a8 * a_sh
        h8 = a8 * h + b8
        b_ref[pl.ds(o, SUBLANES), :] = h8
        return jnp.broadcast_to(h8[SUBLANES - 1:SUBLANES, :], (SUBLANES, w))

    h = lax.fori_loop(0, ts // SUBLANES, group, hcar_ref[...])
    hcar_ref[...] = h
    y_ref[0] = b_ref[...] * _gelu_tanh(gl_ref[0])
    hlast_ref[0] = h[0:1, :]


def _block_diag(wb):
    nb, d, _ = wb.shape
    eye = jnp.eye(nb, dtype=wb.dtype)
    return (wb[:, :, None, :] * eye[:, None, :, None]).reshape(nb * d, nb * d)


def _lru(xl, gl, h0, conv0, conv_w, conv_b, wa_bd, ba, wx_bd, bx, lam, ts, start_pos):
    nb, s, w = xl.shape
    tok = pl.BlockSpec((1, ts, w), lambda b, i: (b, i, 0))
    row = pl.BlockSpec((1, w), lambda b, i: (0, 0))
    mat = pl.BlockSpec((w, w), lambda b, i: (0, 0))
    return pl.pallas_call(
        functools.partial(_lru_kernel, ts=ts, start_pos=start_pos),
        grid=(nb, s // ts),
        in_specs=[tok, tok,
                  pl.BlockSpec((1, 1, w), lambda b, i: (b, 0, 0)),
                  pl.BlockSpec((1, CONV_W - 1, w), lambda b, i: (b, 0, 0)),
                  pl.BlockSpec((CONV_W, w), lambda b, i: (0, 0)),
                  row, mat, row, mat, row, row],
        out_specs=[tok,
                   pl.BlockSpec((1, 1, w), lambda b, i: (b, 0, 0)),
                   pl.BlockSpec((1, CONV_W - 1, w), lambda b, i: (b, 0, 0))],
        out_shape=[jax.ShapeDtypeStruct((nb, s, w), F32),
                   jax.ShapeDtypeStruct((nb, 1, w), F32),
                   jax.ShapeDtypeStruct((nb, CONV_W - 1, w), F32)],
        scratch_shapes=[pltpu.VMEM((ts + SUBLANES, w), F32), pltpu.VMEM((ts, w), F32),
                        pltpu.VMEM((ts, w), F32), pltpu.VMEM((SUBLANES, w), F32)],
        compiler_params=_cparams(("parallel", "arbitrary")),
        name="rg_lru",
    )(xl, gl, h0.reshape(nb, 1, w), conv0, conv_w, conv_b.reshape(1, w), wa_bd,
      ba.reshape(1, w), wx_bd, bx.reshape(1, w), lam.reshape(1, w))


def _route(logits_t, bias_col, n_exp, ts):
    scores = jax.nn.sigmoid(logits_t)
    biased = scores + bias_col
    gsz = n_exp // N_GROUPS
    gscore = []
    for g in range(N_GROUPS):
        blk = biased[g * gsz:(g + 1) * gsz, :]
        m1 = jnp.max(blk, axis=0, keepdims=True)
        cnt = jnp.sum((blk == m1).astype(F32), axis=0, keepdims=True)
        m2 = jnp.max(jnp.where(blk < m1, blk, -jnp.inf), axis=0, keepdims=True)
        gscore.append(m1 + jnp.where(cnt >= 2.0, m1, m2))
    cands = []
    for g in range(N_GROUPS):
        rank = jnp.zeros((1, ts), jnp.int32)
        for g2 in range(N_GROUPS):
            if g2 == g:
                continue
            beats = (gscore[g2] > gscore[g]) if g2 > g else (gscore[g2] >= gscore[g])
            rank = rank + beats.astype(jnp.int32)
        blk = biased[g * gsz:(g + 1) * gsz, :]
        cands.append(jnp.where(rank < TOPK_GROUPS, blk, -jnp.inf))
    cand = jnp.concatenate(cands, axis=0)
    e_iota = lax.broadcasted_iota(jnp.int32, (n_exp, ts), 0)
    rank = jnp.zeros((n_exp, ts), jnp.int32)
    for e2 in range(n_exp):
        row = cand[e2:e2 + 1, :]
        beats = (row > cand) | ((row == cand) & (e2 < e_iota))
        rank = rank + beats.astype(jnp.int32)
    w = jnp.where(rank < TOP_K, scores, 0.0)
    gates = ROUTED_SCALE * w / jnp.sum(w, axis=0, keepdims=True)
    e_f = e_iota.astype(F32)
    ids = [jnp.sum(jnp.where(rank == k, e_f, 0.0), axis=0, keepdims=True) for k in range(TOP_K)]
    wts = [jnp.sum(jnp.where(rank == k, gates, 0.0), axis=0, keepdims=True) for k in range(TOP_K)]
    return jnp.concatenate(ids + wts, axis=0)


def _outproj_kernel(attn_ref, lru_ref, x_ref, gm_ref, sf_ref, cf_ref, ag_ref, lg_ref, wo_ref,
                    ng_ref, rw_ref, rb_ref, x1_ref, h2_ref, gates_ref, *, ts):
    aw = attn_ref.shape[2]
    an = _rms(attn_ref[0], ag_ref[...]).astype(BF16)
    ln = _rms(lru_ref[0], lg_ref[...]).astype(BF16)
    mixed = _dot(an, wo_ref[0:aw, :]) + _dot(ln, wo_ref[aw:, :])
    x1 = x_ref[0] + gm_ref[0] * mixed
    x1_ref[0] = x1
    h2 = _rms(x1, ng_ref[...]) * (1.0 + cf_ref[0]) + sf_ref[0]
    h2_ref[0] = h2
    n_exp = rw_ref.shape[0]
    route_t = _route(_dot3(rw_ref[...], h2, NT), rb_ref[...], n_exp, ts)
    route_t = jnp.concatenate([route_t, jnp.zeros((LANES - 2 * TOP_K, ts), F32)], axis=0)
    gates_ref[0] = route_t.T


def _outproj(attn, lru, x, gate_m, shift_f, scale_f, ag, lg, wo_bf, ng, rw_t, rb_col, ts):
    nb, s, d = x.shape
    aw = attn.shape[2]
    n_exp = rw_t.shape[0]
    tok = lambda width: pl.BlockSpec((1, ts, width), lambda b, i: (b, i, 0))
    const = lambda a: pl.BlockSpec(a.shape, lambda b, i: (0,) * a.ndim)
    return pl.pallas_call(
        functools.partial(_outproj_kernel, ts=ts),
        grid=(nb, s // ts),
        in_specs=[tok(aw), tok(lru.shape[2]), tok(d), _mod_spec(gate_m, ts), _mod_spec(shift_f, ts),
                  _mod_spec(scale_f, ts), const(ag), const(lg), const(wo_bf), const(ng),
                  const(rw_t), const(rb_col)],
        out_specs=[tok(d), tok(d), tok(LANES)],
        out_shape=[jax.ShapeDtypeStruct((nb, s, d), F32), jax.ShapeDtypeStruct((nb, s, d), F32),
                   jax.ShapeDtypeStruct((nb, s, LANES), F32)],
        compiler_params=_cparams(("parallel", "parallel")),
        name="outproj_router",
    )(attn, lru, x, gate_m, shift_f, scale_f, ag, lg, wo_bf, ng, rw_t, rb_col)


def _swiglu_bf(h, w_gu, w_down):
    gu = _dot(h, w_gu)
    de = gu.shape[1] // 2
    act = _silu(gu[:, :de]) * gu[:, de:]
    return _dot(act.astype(BF16), w_down)


MOE_TILE = 128
MOE_STRIDE = MOE_TILE + 1
MOE_BATCH = 16
MOE_GROUP = 2
MOE_KEY_SHIFT = 12
SMEM_CHUNK = 1024


def _moe_lists(route, n_blocks, n_exp):
    t = route.shape[0]
    tb = t // n_blocks
    assert tb * n_blocks == t and tb < (1 << MOE_KEY_SHIFT)
    ids = route[:, 0:TOP_K].astype(jnp.int32).reshape(n_blocks, tb, TOP_K)
    wts = route[:, TOP_K:2 * TOP_K].reshape(n_blocks, tb * TOP_K)
    tloc = jnp.arange(tb, dtype=jnp.int32)[None, :, None]
    keys = ((ids << MOE_KEY_SHIFT) | tloc).reshape(n_blocks, tb * TOP_K)
    npad = MOE_TILE - 1
    dummy = (jnp.arange(n_exp, dtype=jnp.int32) << MOE_KEY_SHIFT) | tb
    dummy = jnp.broadcast_to(jnp.repeat(dummy, npad)[None], (n_blocks, n_exp * npad))
    keys = jnp.concatenate([keys, dummy], axis=1)
    wts = jnp.concatenate([wts, jnp.zeros(dummy.shape, F32)], axis=1)
    keys, wts = lax.sort((keys, wts), dimension=1, num_keys=1)
    n = keys.shape[1]
    lp = -(-n // SMEM_CHUNK) * SMEM_CHUNK
    tok8 = (keys & ((1 << MOE_KEY_SHIFT) - 1)) * SUBLANES
    tok8 = jnp.pad(tok8, ((0, 0), (0, lp - n))).reshape(-1)
    wts = jnp.pad(wts, ((0, 0), (0, lp - n))).reshape(-1)
    cnt = jnp.sum((ids[..., None] == jnp.arange(n_exp, dtype=jnp.int32)).astype(jnp.int32),
                  axis=(1, 2))
    start = jnp.cumsum(cnt + npad, axis=1) - (cnt + npad)
    ntile = (cnt + npad) // MOE_TILE
    return tok8, wts, start.astype(jnp.int32), ntile.astype(jnp.int32), lp


def _moe_sparse_kernel(start_ref, ntile_ref, tok_hbm, wts_hbm, x_ref, wgu_ref, wdn_ref, out_ref,
                       tok_smem, wts_smem, xt_ref, yt_ref, sem):
    b, e = pl.program_id(0), pl.program_id(1)
    lp = tok_smem.shape[0]
    m, s = MOE_TILE, MOE_STRIDE
    nch = SUBLANES

    @pl.when(e == 0)
    def _():
        off = pl.multiple_of(b * lp, SMEM_CHUNK)
        cp_t = pltpu.make_async_copy(tok_hbm.at[pl.ds(off, lp)], tok_smem, sem.at[0])
        cp_w = pltpu.make_async_copy(wts_hbm.at[pl.ds(off, lp)], wts_smem, sem.at[1])
        cp_t.start()
        cp_w.start()
        out_ref[...] = jnp.zeros(out_ref.shape, F32)
        cp_t.wait()
        cp_w.wait()

    base = start_ref[b, e]
    wgu = wgu_ref[0]
    wdn = wdn_ref[0]

    def gather(r0, xt):
        for mi in range(m):
            t8 = pl.multiple_of(tok_smem[r0 + mi], SUBLANES)
            xt[pl.ds(mi, SUBLANES, stride=s), :] = x_ref[pl.ds(t8, SUBLANES), :]

    def expert(xt, yt):
        x = jnp.concatenate([xt[pl.ds(j * s, m), :] for j in range(nch)], axis=1)
        y = _swiglu_bf(x.astype(BF16), wgu, wdn)
        for j in range(nch):
            yt[pl.ds(j * s, m), :] = y[:, j * LANES:(j + 1) * LANES]

    def combine(r0, yt):
        for g0 in range(0, m, MOE_BATCH):
            upd = []
            for mi in range(g0, g0 + MOE_BATCH):
                t8 = pl.multiple_of(tok_smem[r0 + mi], SUBLANES)
                slab = yt[pl.ds(mi, SUBLANES, stride=s), :]
                upd.append((t8, out_ref[pl.ds(t8, SUBLANES), :] + wts_smem[r0 + mi] * slab))
            for t8, v in upd:
                out_ref[pl.ds(t8, SUBLANES), :] = v

    def tiles(r0, n):
        bufs = [(xt_ref.at[c], yt_ref.at[c]) for c in range(n)]
        for c, (xt, yt) in enumerate(bufs):
            gather(r0 + c * m, xt)
        for xt, yt in bufs:
            expert(xt, yt)
        for c, (xt, yt) in enumerate(bufs):
            combine(r0 + c * m, yt)

    nt = ntile_ref[b, e]

    def pair(i, carry):
        tiles(base + i * (MOE_GROUP * m), MOE_GROUP)
        return carry

    lax.fori_loop(0, nt // MOE_GROUP, pair, 0)
    for rem in range(1, MOE_GROUP):
        @pl.when(nt % MOE_GROUP == rem)
        def _():
            tiles(base + (nt - rem) * m, rem)


def _moe_sparse(x_slab, tok8, wts, start, ntile, lp, w_gu_bf, w_down_bf):
    n_blocks, rows, _ = x_slab.shape
    n_exp = w_gu_bf.shape[0]
    blk = pl.BlockSpec((None, rows, LANES), lambda b, e, *_: (b, 0, 0))
    grid_spec = pltpu.PrefetchScalarGridSpec(
        num_scalar_prefetch=2,
        grid=(n_blocks, n_exp),
        in_specs=[pl.BlockSpec(memory_space=pl.ANY), pl.BlockSpec(memory_space=pl.ANY), blk,
                  pl.BlockSpec((1,) + w_gu_bf.shape[1:], lambda b, e, *_: (e, 0, 0)),
                  pl.BlockSpec((1,) + w_down_bf.shape[1:], lambda b, e, *_: (e, 0, 0))],
        out_specs=blk,
        scratch_shapes=[pltpu.SMEM((lp,), jnp.int32), pltpu.SMEM((lp,), F32),
                        pltpu.VMEM((MOE_GROUP, SUBLANES * MOE_STRIDE, LANES), F32),
                        pltpu.VMEM((MOE_GROUP, SUBLANES * MOE_STRIDE, LANES), F32),
                        pltpu.SemaphoreType.DMA((2,))],
    )
    return pl.pallas_call(
        _moe_sparse_kernel,
        grid_spec=grid_spec,
        out_shape=jax.ShapeDtypeStruct(x_slab.shape, F32),
        compiler_params=_cparams(("parallel", "arbitrary")),
        name="moe_routed",
    )(start, ntile, tok8, wts, x_slab, w_gu_bf, w_down_bf)


def _final_kernel(x1_ref, h2_ref, moe_ref, gf_ref, fg_ref, sgu_ref, sdn_ref, y_ref):
    f = _swiglu_bf(h2_ref[0].astype(BF16), sgu_ref[...], sdn_ref[...]) + moe_ref[...]
    y_ref[0] = _rms(x1_ref[0] + gf_ref[0] * f, fg_ref[...])


def _final(x1, h2, moe_all, row0, gate_f, final_g, sgu_bf, sdn_bf, tm):
    nb, s, d = x1.shape
    nt = s // tm
    tok = pl.BlockSpec((1, tm, d), lambda b, i: (b, i, 0))
    const = lambda a: pl.BlockSpec(a.shape, lambda b, i: (0,) * a.ndim)
    return pl.pallas_call(
        _final_kernel,
        grid=(nb, nt),
        in_specs=[tok, tok, pl.BlockSpec((tm, d), lambda b, i: (row0 // tm + b * nt + i, 0)),
                  _mod_spec(gate_f, tm), const(final_g), const(sgu_bf), const(sdn_bf)],
        out_specs=tok,
        out_shape=jax.ShapeDtypeStruct((nb, s, d), F32),
        compiler_params=_cparams(("parallel", "parallel")),
        name="shared_final",
    )(x1, h2, moe_all, gate_f, final_g, sgu_bf, sdn_bf)


def _rope_tables(start_pos, s):
    half = HEAD_DIM // 2
    inv_freq = ROPE_THETA ** (-jnp.arange(half, dtype=F32) / half)
    ang = (start_pos + jnp.arange(s, dtype=jnp.int32)).astype(F32)[:, None] * inv_freq[None, :]
    cos, sin = jnp.cos(ang), jnp.sin(ang)
    reps = LANES // HEAD_DIM
    return (jnp.tile(jnp.concatenate([cos, cos], axis=1), (1, reps)),
            jnp.tile(jnp.concatenate([-sin, sin], axis=1), (1, reps)))


def _mixer(x, mods, attend, h0, conv0, start_pos, ts, wts):
    shift_m, scale_m, gate_m, shift_f, scale_f, _ = mods
    s = x.shape[1]
    cos, sin = _rope_tables(start_pos, s)
    q, k, v, xl, gl = _inproj(x, shift_m, scale_m, cos, sin, wts["norm_mix_g"], wts["w_in"], ts)
    attn = attend(q, k, v)
    lru, h_last, cbuf = _lru(xl, gl, h0, conv0, wts["conv_w"], wts["conv_b"], wts["wa"],
                             wts["gate_a_b"], wts["wx"], wts["gate_x_b"], wts["lru_lambda"],
                             ts, start_pos)
    x1, h2, route = _outproj(attn, lru, x, gate_m, shift_f, scale_f, wts["attn_out_g"],
                             wts["lru_out_g"], wts["w_out"], wts["norm_ffn_g"], wts["router_w_t"],
                             wts["router_bias"], ts)
    return x1, h2, route, k, v, h_last, cbuf


def _moe_blocks(t):
    max_tokens = 2560
    return next(n for n in range(1, t + 1) if t % n == 0 and t // n <= max_tokens)


def kernel(x_prompt, x_sample, c_prompt, c_sample, cache_k, cache_v, state_h, state_conv, page_table, ada_w, ada_b, norm_mix_g, w_in, conv_w, conv_b, gate_a_w, gate_a_b, gate_x_w, gate_x_b, lru_lambda, attn_out_g, lru_out_g, w_out, norm_ffn_g, router_w, router_bias, exp_w_gu, exp_w_down, shared_w_gu, shared_w_down, final_g):
    depth = ada_w.shape[0]
    assert depth == 1, "single-layer trunk"
    n_prm, seq, d = x_prompt.shape
    n_dec, dec_seq, _ = x_sample.shape
    n_heads = cache_k.shape[3]
    aw = n_heads * HEAD_DIM
    lw = state_h.shape[2]
    past_len = page_table.shape[1] * PAGE_SIZE
    l = 0
    row = lambda a: a.reshape(1, -1)
    wts = dict(
        norm_mix_g=row(norm_mix_g[l]), w_in=w_in[l].astype(BF16), conv_w=conv_w[l], conv_b=conv_b[l],
        wa=_block_diag(gate_a_w[l]).astype(BF16), gate_a_b=gate_a_b[l],
        wx=_block_diag(gate_x_w[l]).astype(BF16), gate_x_b=gate_x_b[l], lru_lambda=lru_lambda[l],
        attn_out_g=row(attn_out_g[l]), lru_out_g=row(lru_out_g[l]), w_out=w_out[l].astype(BF16),
        norm_ffn_g=row(norm_ffn_g[l]), router_w_t=router_w[l].T,
        router_bias=router_bias[l].reshape(-1, 1), final_g=row(final_g),
        shared_w_gu=shared_w_gu[l].astype(BF16), shared_w_down=shared_w_down[l].astype(BF16))

    n_c = n_prm + n_dec
    c_all = jnp.concatenate([c_prompt, c_sample, jnp.zeros((-n_c % SUBLANES, d), F32)], axis=0)
    mod = _ada_mod(c_all, ada_w[l], ada_b[l])
    mods_p = [m[:n_prm, None, :] for m in jnp.split(mod, 6, axis=-1)]
    mods_s = [jnp.repeat(m[n_prm:n_c], dec_seq, axis=0)[None] for m in jnp.split(mod, 6, axis=-1)]

    x1p, h2p, route_p, k1, v1, h1, c1 = _mixer(
        x_prompt, mods_p, _moba_prompt, jnp.zeros((n_prm, lw), F32),
        jnp.zeros((n_prm, CONV_W - 1, lw), F32), 0, 512, wts)

    ck = jnp.transpose(cache_k[l], (0, 2, 3, 1)).reshape(cache_k.shape[1], aw, PAGE_SIZE)
    cv = jnp.transpose(cache_v[l], (0, 2, 3, 1)).reshape(cache_v.shape[1], aw, PAGE_SIZE)

    def attend_sample(q, k, v):
        r = lambda a: a.reshape(n_dec, dec_seq, aw)
        return _moba_sample(r(q), r(k), r(v), ck, cv, page_table).reshape(1, n_dec * dec_seq, aw)

    xs = x_sample.reshape(1, n_dec * dec_seq, d)
    shift_m, scale_m, gate_m, shift_f, scale_f, gate_f = mods_s
    cos, sin = _rope_tables(past_len, dec_seq)
    cos, sin = jnp.tile(cos, (n_dec, 1)), jnp.tile(sin, (n_dec, 1))
    nt = n_dec * dec_seq
    q, k2, v2, xl, gl = _inproj(xs, shift_m, scale_m, cos, sin, wts["norm_mix_g"], wts["w_in"], nt)
    attn = attend_sample(q, k2, v2)
    lru, h2s, c2 = _lru(xl.reshape(n_dec, dec_seq, lw), gl.reshape(n_dec, dec_seq, lw), state_h[l],
                        state_conv[l], wts["conv_w"], wts["conv_b"], wts["wa"], wts["gate_a_b"],
                        wts["wx"], wts["gate_x_b"], wts["lru_lambda"], dec_seq, past_len)
    x1s, h2s_, route_s = _outproj(attn, lru.reshape(1, nt, lw), xs, gate_m, shift_f, scale_f,
                                  wts["attn_out_g"], wts["lru_out_g"], wts["w_out"],
                                  wts["norm_ffn_g"], wts["router_w_t"], wts["router_bias"], nt)

    n_exp = exp_w_gu.shape[1]
    t_all = n_prm * seq + nt
    n_blk = _moe_blocks(t_all)
    tb = t_all // n_blk
    route = jnp.concatenate([route_p.reshape(-1, LANES), route_s.reshape(-1, LANES)], axis=0)
    tok8, gw, start, ntile, lp = _moe_lists(route, n_blk, n_exp)
    h_all = jnp.concatenate([h2p.reshape(-1, d), h2s_.reshape(-1, d)], axis=0).reshape(n_blk, tb, d)
    x_slab = jnp.pad(h_all, ((0, 0), (0, 1), (0, 0))).reshape(n_blk, (tb + 1) * d // LANES, LANES)
    moe_slab = _moe_sparse(x_slab, tok8, gw, start, ntile, lp,
                           exp_w_gu[l].astype(BF16), exp_w_down[l].astype(BF16))
    moe_all = moe_slab.reshape(n_blk, tb + 1, d)[:, :tb].reshape(t_all, d)
    fin = lambda x1, h2, row0, gf, tm: _final(x1, h2, moe_all, row0, gf, wts["final_g"],
                                              wts["shared_w_gu"], wts["shared_w_down"], tm)
    yp = fin(x1p, h2p, 0, mods_p[5], 512)
    ys = fin(x1s, h2s_, n_prm * seq, gate_f, nt)

    return (yp, ys.reshape(n_dec, dec_seq, d),
            k1.reshape(1, n_prm, seq, n_heads, HEAD_DIM), v1.reshape(1, n_prm, seq, n_heads, HEAD_DIM),
            h1.reshape(1, n_prm, lw), c1[None],
            k2.reshape(1, n_dec, dec_seq, n_heads, HEAD_DIM), v2.reshape(1, n_dec, dec_seq, n_heads, HEAD_DIM),
            h2s.reshape(1, n_dec, lw), c2[None])
```

```python
import functools
import math

import jax
import jax.numpy as jnp
from jax import lax
from jax.experimental import pallas as pl
from jax.experimental.pallas import tpu as pltpu

F32 = jnp.float32
BF16 = jnp.bfloat16

HEAD_DIM = 64
LRU_BLOCKS = 8
CONV_W = 4
LRU_C = 8.0
MOBA_BLOCK = 256
MOBA_TOPK = 3
ROPE_THETA = 10000.0
N_GROUPS = 8
TOPK_GROUPS = 4
TOP_K = 8
ROUTED_SCALE = 2.5
EPS = 1e-6
PAGE_SIZE = 128

LANES = 128
SUBLANES = 8
NEG = -1e30
VMEM_LIMIT = 56 * 1024 * 1024

NN = (((1,), (0,)), ((), ()))
NT = (((1,), (1,)), ((), ()))


def _cparams(sem):
    return pltpu.CompilerParams(dimension_semantics=sem, vmem_limit_bytes=VMEM_LIMIT)


def _split(x):
    hi = x.astype(BF16)
    lo = (x - hi.astype(F32)).astype(BF16)
    return hi, lo


def _dot3(a, b, dims):
    ah, al = _split(a)
    bh, bl = _split(b)
    d = lambda u, v: lax.dot_general(u, v, dims, preferred_element_type=F32)
    return d(ah, bh) + (d(ah, bl) + d(al, bh))


def _dot(a, b, dims=NN):
    return lax.dot_general(a, b, dims, preferred_element_type=F32)


def _rms(x, g):
    return x * lax.rsqrt(jnp.mean(x * x, axis=-1, keepdims=True) + EPS) * g


def _silu(x):
    return x * jax.nn.sigmoid(x)


def _gelu_tanh(x):
    c = math.sqrt(2.0 / math.pi)
    return 0.5 * x * (1.0 + jnp.tanh(c * (x + 0.044715 * (x * x * x))))


def _ada_kernel(c_ref, w_ref, b_ref, o_ref):
    o_ref[...] = _dot3(_silu(c_ref[...]), w_ref[...], NN) + b_ref[...]


def _ada_mod(c, ada_w, ada_b):
    r, d = c.shape
    n = ada_w.shape[1]
    tn = 1536
    return pl.pallas_call(
        _ada_kernel,
        grid=(n // tn,),
        in_specs=[pl.BlockSpec((r, d), lambda j: (0, 0)),
                  pl.BlockSpec((d, tn), lambda j: (0, j)),
                  pl.BlockSpec((1, tn), lambda j: (0, j))],
        out_specs=pl.BlockSpec((r, tn), lambda j: (0, j)),
        out_shape=jax.ShapeDtypeStruct((r, n), F32),
        compiler_params=_cparams(("parallel",)),
        name="ada_mod",
    )(c, ada_w, ada_b.reshape(1, n))


def _inproj_kernel(x_ref, shift_ref, scale_ref, cos_ref, sin_ref, g_ref, w_ref,
                   q_ref, k_ref, v_ref, xl_ref, gl_ref, *, aw):
    x = x_ref[0]
    h = _rms(x, g_ref[...]) * (1.0 + scale_ref[0]) + shift_ref[0]
    proj = _dot(h.astype(BF16), w_ref[...])
    cos = cos_ref[...]
    sin = sin_ref[...]
    lane = lax.broadcasted_iota(jnp.int32, cos.shape, 1)
    first = (lane % HEAD_DIM) < (HEAD_DIM // 2)

    def rope(t):
        outs = []
        for c in range(aw // LANES):
            tc = t[:, LANES * c:LANES * (c + 1)]
            partner = jnp.where(first, pltpu.roll(tc, LANES - HEAD_DIM // 2, 1),
                                pltpu.roll(tc, HEAD_DIM // 2, 1))
            outs.append(tc * cos + partner * sin)
        return jnp.concatenate(outs, axis=1)

    q_ref[0] = rope(proj[:, 0:aw])
    k_ref[0] = rope(proj[:, aw:2 * aw])
    v_ref[0] = proj[:, 2 * aw:3 * aw]
    lw = (proj.shape[1] - 3 * aw) // 2
    xl_ref[0] = proj[:, 3 * aw:3 * aw + lw]
    gl_ref[0] = proj[:, 3 * aw + lw:]


def _mod_spec(mod, ts):
    if mod.shape[1] == 1:
        return pl.BlockSpec((1, 1, mod.shape[2]), lambda b, i: (b, 0, 0))
    return pl.BlockSpec((1, ts, mod.shape[2]), lambda b, i: (b, i, 0))


def _inproj(x, shift, scale, cos, sin, g, w_bf, ts):
    nb, s, d = x.shape
    aw = 512
    lw = (w_bf.shape[1] - 3 * aw) // 2
    tok = lambda width: pl.BlockSpec((1, ts, width), lambda b, i: (b, i, 0))
    shp = lambda width: jax.ShapeDtypeStruct((nb, s, width), F32)
    return pl.pallas_call(
        functools.partial(_inproj_kernel, aw=aw),
        grid=(nb, s // ts),
        in_specs=[tok(d), _mod_spec(shift, ts), _mod_spec(scale, ts),
                  pl.BlockSpec((ts, LANES), lambda b, i: (i, 0)),
                  pl.BlockSpec((ts, LANES), lambda b, i: (i, 0)),
                  pl.BlockSpec((1, d), lambda b, i: (0, 0)),
                  pl.BlockSpec(w_bf.shape, lambda b, i: (0, 0))],
        out_specs=[tok(aw), tok(aw), tok(aw), tok(lw), tok(lw)],
        out_shape=[shp(aw), shp(aw), shp(aw), shp(lw), shp(lw)],
        compiler_params=_cparams(("parallel", "parallel")),
        name="inproj",
    )(x, shift, scale, cos, sin, g, w_bf)


def _moba_prompt_kernel(q_ref, k_ref, v_ref, o_ref, kbf_ref, vst_ref, km_ref, selb_ref,
                        sa_ref, sb_ref):
    s_len = k_ref.shape[1]
    tq = MOBA_BLOCK
    nblk = s_len // MOBA_BLOCK
    half = lax.broadcasted_iota(jnp.int32, (LANES, tq), 0) < HEAD_DIM
    kbf_ref[...] = k_ref[0].astype(BF16)
    km_ref[...] = jnp.zeros(km_ref.shape, F32)
    for n in range(nblk):
        rows = slice(n * MOBA_BLOCK, (n + 1) * MOBA_BLOCK)
        km_ref[n:n + 1, :] = jnp.mean(k_ref[0, rows, :], axis=0, keepdims=True)
        vt = v_ref[0, rows, :].T
        vst_ref[n] = jnp.concatenate([jnp.where(half, vt, 0.0), jnp.where(half, 0.0, vt)],
                                     axis=1).astype(BF16)
    nrow = km_ref.shape[0]
    n_iota = lax.broadcasted_iota(jnp.int32, (nrow, 2 * tq), 0)
    keyi = lax.broadcasted_iota(jnp.int32, (MOBA_BLOCK, 2 * tq), 0)
    qi = lax.broadcasted_iota(jnp.int32, (MOBA_BLOCK, 2 * tq), 1) % tq

    def stack_heads(pb):
        return jnp.concatenate([pb[:, :tq], pb[:, tq:]], axis=0)

    def per_head(rowvec):
        return jnp.where(half, rowvec[:, :tq], rowvec[:, tq:])

    def qtile(t, carry):
        r0 = pl.multiple_of(t * tq, tq)
        qt = q_ref[0, pl.ds(r0, tq), :].T
        qt2 = jnp.concatenate([jnp.where(half, qt, 0.0), jnp.where(half, 0.0, qt)], axis=1)
        gm = jnp.where(n_iota < t, _dot3(km_ref[...], qt2, NN), -jnp.inf)
        rank = jnp.zeros(gm.shape, jnp.int32)
        for m_ in range(nblk):
            row = gm[m_:m_ + 1, :]
            beats = (row > gm) | ((row == gm) & (m_ < n_iota))
            rank = rank + beats.astype(jnp.int32)
        selb_ref[...] = jnp.where((n_iota < t) & (rank < MOBA_TOPK), 0.0, NEG)
        qb = (qt2 * (HEAD_DIM ** -0.5)).astype(BF16)
        st = jnp.where(keyi <= qi, _dot(kbf_ref[pl.ds(r0, MOBA_BLOCK), :], qb), NEG)
        m = jnp.max(st, axis=0, keepdims=True)
        p = jnp.exp(st - m)
        l = jnp.sum(p, axis=0, keepdims=True)
        acc = _dot(vst_ref[t], stack_heads(p.astype(BF16)))

        def scores(j, dst):
            j = jnp.minimum(j, nblk - 1)
            j0 = pl.multiple_of(j * MOBA_BLOCK, MOBA_BLOCK)
            dst[...] = _dot(kbf_ref[pl.ds(j0, MOBA_BLOCK), :], qb) + selb_ref[pl.ds(j, 1), :]

        def attend(src, j, m, l, acc):
            st = src[...]
            m_new = jnp.maximum(m, jnp.max(st, axis=0, keepdims=True))
            alpha = jnp.exp(m - m_new)
            p = jnp.exp(st - m_new)
            l = alpha * l + jnp.sum(p, axis=0, keepdims=True)
            acc = per_head(alpha) * acc + _dot(vst_ref[j], stack_heads(p.astype(BF16)))
            return m_new, l, acc

        scores(0, sa_ref)

        def body(i, c):
            scores(2 * i + 1, sb_ref)
            c = attend(sa_ref, 2 * i, *c)
            scores(2 * i + 2, sa_ref)
            return attend(sb_ref, jnp.minimum(2 * i + 1, nblk - 1), *c)

        m, l, acc = lax.fori_loop(0, (t + 1) // 2, body, (m, l, acc))
        o_ref[0, pl.ds(r0, tq), :] = (acc / per_head(l)).T
        return carry

    lax.fori_loop(0, s_len // tq, qtile, 0)


def _moba_prompt(q, k, v):
    nb, s, aw = q.shape
    nblk = s // MOBA_BLOCK
    nrow = -(-nblk // 16) * 16
    spec = pl.BlockSpec((1, s, LANES), lambda b, h: (b, 0, h))
    return pl.pallas_call(
        _moba_prompt_kernel,
        grid=(nb, aw // LANES),
        in_specs=[spec, spec, spec],
        out_specs=spec,
        out_shape=jax.ShapeDtypeStruct((nb, s, aw), F32),
        scratch_shapes=[pltpu.VMEM((s, LANES), BF16),
                        pltpu.VMEM((nblk, LANES, 2 * MOBA_BLOCK), BF16),
                        pltpu.VMEM((nrow, LANES), F32),
                        pltpu.VMEM((nrow, 2 * MOBA_BLOCK), F32),
                        pltpu.VMEM((MOBA_BLOCK, 2 * MOBA_BLOCK), F32),
                        pltpu.VMEM((MOBA_BLOCK, 2 * MOBA_BLOCK), F32)],
        compiler_params=_cparams(("parallel", "parallel")),
        name="moba_prompt",
    )(q, k, v)


def _moba_sample_kernel(pt_ref, q_ref, kn_ref, vn_ref, *rest, n_heads, n_q, bps):
    ppb = MOBA_BLOCK // PAGE_SIZE
    npg = bps * ppb
    k_refs, v_refs = rest[:npg], rest[npg:2 * npg]
    o_ref, qbd_ref, km_ref, m_ref, l_ref, o_scr = rest[2 * npg:]
    step = pl.program_id(1)
    nstep = pl.num_programs(1)
    rows = n_heads * n_q
    aw = q_ref.shape[2]
    rowi = lax.broadcasted_iota(jnp.int32, (rows, aw), 0)
    coli = lax.broadcasted_iota(jnp.int32, (rows, aw), 1)
    diag = (rowi // n_q) == (coli // HEAD_DIM)
    lane = lax.broadcasted_iota(jnp.int32, (rows, LANES), 1)

    @pl.when(step == 0)
    def _():
        qt = jnp.concatenate([q_ref[0]] * n_heads, axis=0)
        qbd_ref[...] = jnp.where(diag, qt, 0.0)
        km_ref[...] = jnp.zeros(km_ref.shape, F32)
        m_ref[...] = jnp.full(m_ref.shape, NEG, F32)
        l_ref[...] = jnp.zeros(l_ref.shape, F32)

    qbd = qbd_ref[...]
    qb = (qbd * (HEAD_DIM ** -0.5)).astype(BF16)
    klane = lax.broadcasted_iota(jnp.int32, km_ref.shape, 1)
    km, mm_, ll_ = km_ref[...], m_ref[...], l_ref[...]
    blocks = range(bps)
    kts = [jnp.concatenate([r[0] for r in k_refs[c * ppb:(c + 1) * ppb]], axis=1) for c in blocks]
    ss = [_dot(qb, kt.astype(BF16)) for kt in kts]
    ms = [jnp.max(s, axis=1, keepdims=True) for s in ss]
    ps = [jnp.exp(s - mj) for s, mj in zip(ss, ms)]
    ls = [jnp.sum(p, axis=1, keepdims=True) for p in ps]
    for c in blocks:
        j = step * bps + c
        vt = jnp.concatenate([r[0] for r in v_refs[c * ppb:(c + 1) * ppb]], axis=1)
        o_scr[j] = _dot(ps[c].astype(BF16), vt.astype(BF16), NT)
        km = jnp.where(klane == j, jnp.mean(kts[c], axis=1, keepdims=True), km)
        mm_ = jnp.where(lane == j, ms[c], mm_)
        ll_ = jnp.where(lane == j, ls[c], ll_)
    km_ref[...], m_ref[...], l_ref[...] = km, mm_, ll_

    @pl.when(step == nstep - 1)
    def _():
        npast = o_scr.shape[0]
        gate = _dot3(qbd, km_ref[...], NN)
        valid = lane < npast
        gm = jnp.where(valid, gate, -jnp.inf)
        rank = jnp.zeros((rows, LANES), jnp.int32)
        for mm in range(npast):
            colv = gm[:, mm:mm + 1]
            beats = (colv > gm) | ((colv == gm) & (mm < lane))
            rank = rank + beats.astype(jnp.int32)
        sel = valid & (rank < MOBA_TOPK)
        so = _dot(qb, kn_ref[0].astype(BF16), NT)
        qi = lax.broadcasted_iota(jnp.int32, (rows, n_q), 0) % n_q
        ki = lax.broadcasted_iota(jnp.int32, (rows, n_q), 1)
        so = jnp.where(ki <= qi, so, NEG)
        mo = jnp.max(so, axis=1, keepdims=True)
        po = jnp.exp(so - mo)
        lo = jnp.sum(po, axis=1, keepdims=True)
        oo = _dot(po.astype(BF16), vn_ref[0].astype(BF16))
        mall = jnp.where(sel, m_ref[...], NEG)
        mx = jnp.maximum(jnp.max(mall, axis=1, keepdims=True), mo)
        w = jnp.where(sel, jnp.exp(mall - mx), 0.0)
        wo = jnp.exp(mo - mx)
        den = jnp.sum(w * l_ref[...], axis=1, keepdims=True) + wo * lo
        num = wo * oo
        for jj in range(npast):
            num = num + w[:, jj:jj + 1] * o_scr[jj]
        res = jnp.where(diag, num / den, 0.0)
        out = res[0:n_q, :]
        for hh in range(1, n_heads):
            out = out + res[hh * n_q:(hh + 1) * n_q, :]
        o_ref[0] = out


def _moba_sample(q, k_new, v_new, cache_kt, cache_vt, page_table):
    nb, n_q, aw = q.shape
    n_heads = aw // HEAD_DIM
    n_pages = page_table.shape[1]
    ppb = MOBA_BLOCK // PAGE_SIZE
    npast = n_pages // ppb
    bps = next(n for n in (4, 2, 1) if npast % n == 0)
    npg = bps * ppb
    rows = n_heads * n_q
    tok = pl.BlockSpec((1, n_q, aw), lambda b, j, pt: (b, 0, 0))
    page = lambda o: pl.BlockSpec((1, aw, PAGE_SIZE), lambda b, j, pt: (pt[b, npg * j + o], 0, 0))
    pages = [page(o) for o in range(npg)]
    grid_spec = pltpu.PrefetchScalarGridSpec(
        num_scalar_prefetch=1,
        grid=(nb, npast // bps),
        in_specs=[tok, tok, tok] + pages + pages,
        out_specs=tok,
        scratch_shapes=[pltpu.VMEM((rows, aw), F32), pltpu.VMEM((aw, LANES), F32),
                        pltpu.VMEM((rows, LANES), F32), pltpu.VMEM((rows, LANES), F32),
                        pltpu.VMEM((npast, rows, aw), F32)],
    )
    return pl.pallas_call(
        functools.partial(_moba_sample_kernel, n_heads=n_heads, n_q=n_q, bps=bps),
        grid_spec=grid_spec,
        out_shape=jax.ShapeDtypeStruct((nb, n_q, aw), F32),
        compiler_params=_cparams(("parallel", "arbitrary")),
        name="moba_sample",
    )(page_table, q, k_new, v_new, *([cache_kt] * npg), *([cache_vt] * npg))


def _lru_kernel(xl_ref, gl_ref, h0_ref, c0_ref, cw_ref, cb_ref, wa_ref, ba_ref, wx_ref, bx_ref,
                lam_ref, y_ref, hlast_ref, cbuf_ref, xbuf_ref, a_ref, b_ref, hcar_ref,
                *, ts, start_pos):
    i = pl.program_id(1)
    w = xl_ref.shape[2]
    hist = SUBLANES

    @pl.when(i == 0)
    def _():
        xbuf_ref[0:hist, :] = jnp.zeros((hist, w), F32)
        xbuf_ref[hist - (CONV_W - 1):hist, :] = c0_ref[0]
        hcar_ref[...] = jnp.broadcast_to(h0_ref[0], (SUBLANES, w))

    x = xl_ref[0]
    xbuf_ref[hist:hist + ts, :] = x
    conv = cb_ref[...] + cw_ref[0:1, :] * xbuf_ref[hist - 3:hist - 3 + ts, :]
    conv = conv + cw_ref[1:2, :] * xbuf_ref[hist - 2:hist - 2 + ts, :]
    conv = conv + cw_ref[2:3, :] * xbuf_ref[hist - 1:hist - 1 + ts, :]
    conv = conv + cw_ref[3:4, :] * x
    cbuf_ref[0] = xbuf_ref[hist + ts - (CONV_W - 1):hist + ts, :]
    xbuf_ref[0:hist, :] = xbuf_ref[ts:ts + hist, :]

    cbf = conv.astype(BF16)
    r = jax.nn.sigmoid(_dot(cbf, wa_ref[...]) + ba_ref[...])
    ig = jax.nn.sigmoid(_dot(cbf, wx_ref[...]) + bx_ref[...])
    nl = -lam_ref[...]
    softplus = jnp.maximum(nl, 0.0) + jnp.log1p(jnp.exp(-jnp.abs(nl)))
    log_a = (-LRU_C * r) * softplus
    a = jnp.exp(log_a)
    mult = jnp.sqrt(1.0 - jnp.exp(2.0 * log_a))
    if start_pos == 0:
        reset = (i * ts + lax.broadcasted_iota(jnp.int32, (ts, w), 0)) == 0
        a = jnp.where(reset, 0.0, a)
        mult = jnp.where(reset, 1.0, mult)
    a_ref[...] = a
    b_ref[...] = mult * ig * conv

    rowi = lax.broadcasted_iota(jnp.int32, (SUBLANES, w), 0)

    def group(gi, h):
        o = pl.multiple_of(gi * SUBLANES, SUBLANES)
        a8 = a_ref[pl.ds(o, SUBLANES), :]
        b8 = b_ref[pl.ds(o, SUBLANES), :]
        for d in (1, 2, 4):
            keep = rowi >= d
            a_sh = jnp.where(keep, pltpu.roll(a8, d, 0), 1.0)
            b_sh = jnp.where(keep, pltpu.roll(b8, d, 0), 0.0)
            b8 = a8 * b_sh + b8
            a8 = a8 * a_sh
        h8 = a8 * h + b8
        b_ref[pl.ds(o, SUBLANES), :] = h8
        return jnp.broadcast_to(h8[SUBLANES - 1:SUBLANES, :], (SUBLANES, w))

    h = lax.fori_loop(0, ts // SUBLANES, group, hcar_ref[...])
    hcar_ref[...] = h
    y_ref[0] = b_ref[...] * _gelu_tanh(gl_ref[0])
    hlast_ref[0] = h[0:1, :]


def _block_diag(wb):
    nb, d, _ = wb.shape
    eye = jnp.eye(nb, dtype=wb.dtype)
    return (wb[:, :, None, :] * eye[:, None, :, None]).reshape(nb * d, nb * d)


def _lru(xl, gl, h0, conv0, conv_w, conv_b, wa_bd, ba, wx_bd, bx, lam, ts, start_pos):
    nb, s, w = xl.shape
    tok = pl.BlockSpec((1, ts, w), lambda b, i: (b, i, 0))
    row = pl.BlockSpec((1, w), lambda b, i: (0, 0))
    mat = pl.BlockSpec((w, w), lambda b, i: (0, 0))
    return pl.pallas_call(
        functools.partial(_lru_kernel, ts=ts, start_pos=start_pos),
        grid=(nb, s // ts),
        in_specs=[tok, tok,
                  pl.BlockSpec((1, 1, w), lambda b, i: (b, 0, 0)),
                  pl.BlockSpec((1, CONV_W - 1, w), lambda b, i: (b, 0, 0)),
                  pl.BlockSpec((CONV_W, w), lambda b, i: (0, 0)),
                  row, mat, row, mat, row, row],
        out_specs=[tok,
                   pl.BlockSpec((1, 1, w), lambda b, i: (b, 0, 0)),
                   pl.BlockSpec((1, CONV_W - 1, w), lambda b, i: (b, 0, 0))],
        out_shape=[jax.ShapeDtypeStruct((nb, s, w), F32),
                   jax.ShapeDtypeStruct((nb, 1, w), F32),
                   jax.ShapeDtypeStruct((nb, CONV_W - 1, w), F32)],
        scratch_shapes=[pltpu.VMEM((ts + SUBLANES, w), F32), pltpu.VMEM((ts, w), F32),
                        pltpu.VMEM((ts, w), F32), pltpu.VMEM((SUBLANES, w), F32)],
        compiler_params=_cparams(("parallel", "arbitrary")),
        name="rg_lru",
    )(xl, gl, h0.reshape(nb, 1, w), conv0, conv_w, conv_b.reshape(1, w), wa_bd,
      ba.reshape(1, w), wx_bd, bx.reshape(1, w), lam.reshape(1, w))


def _route(logits_t, bias_col, n_exp, ts):
    scores = jax.nn.sigmoid(logits_t)
    biased = scores + bias_col
    gsz = n_exp // N_GROUPS
    gscore = []
    for g in range(N_GROUPS):
        blk = biased[g * gsz:(g + 1) * gsz, :]
        m1 = jnp.max(blk, axis=0, keepdims=True)
        cnt = jnp.sum((blk == m1).astype(F32), axis=0, keepdims=True)
        m2 = jnp.max(jnp.where(blk < m1, blk, -jnp.inf), axis=0, keepdims=True)
        gscore.append(m1 + jnp.where(cnt >= 2.0, m1, m2))
    cands = []
    for g in range(N_GROUPS):
        rank = jnp.zeros((1, ts), jnp.int32)
        for g2 in range(N_GROUPS):
            if g2 == g:
                continue
            beats = (gscore[g2] > gscore[g]) if g2 > g else (gscore[g2] >= gscore[g])
            rank = rank + beats.astype(jnp.int32)
        blk = biased[g * gsz:(g + 1) * gsz, :]
        cands.append(jnp.where(rank < TOPK_GROUPS, blk, -jnp.inf))
    cand = jnp.concatenate(cands, axis=0)
    e_iota = lax.broadcasted_iota(jnp.int32, (n_exp, ts), 0)
    rank = jnp.zeros((n_exp, ts), jnp.int32)
    for e2 in range(n_exp):
        row = cand[e2:e2 + 1, :]
        beats = (row > cand) | ((row == cand) & (e2 < e_iota))
        rank = rank + beats.astype(jnp.int32)
    w = jnp.where(rank < TOP_K, scores, 0.0)
    gates = ROUTED_SCALE * w / jnp.sum(w, axis=0, keepdims=True)
    e_f = e_iota.astype(F32)
    ids = [jnp.sum(jnp.where(rank == k, e_f, 0.0), axis=0, keepdims=True) for k in range(TOP_K)]
    wts = [jnp.sum(jnp.where(rank == k, gates, 0.0), axis=0, keepdims=True) for k in range(TOP_K)]
    return jnp.concatenate(ids + wts, axis=0)


def _rows_to_slabs(x, cv_ref, slab_ref):
    n = x.shape[0]
    s = n + 1
    for j in range(SUBLANES):
        cv_ref[pl.ds(j * s, n), :] = x[:, j * LANES:(j + 1) * LANES]
    for r in range(n):
        slab_ref[pl.ds(SUBLANES * r, SUBLANES), :] = cv_ref[pl.ds(r, SUBLANES, stride=s), :]


def _slabs_to_rows(slab_ref, cv_ref, n):
    s = n + 1
    for r in range(n):
        cv_ref[pl.ds(r, SUBLANES, stride=s), :] = slab_ref[pl.ds(SUBLANES * r, SUBLANES), :]
    return jnp.concatenate([cv_ref[pl.ds(j * s, n), :] for j in range(SUBLANES)], axis=1)


def _slab_spec(s, tile, tb, blk0):
    per_seq = -(-s // tb)
    tiles = min(tb, s) // tile
    return pl.BlockSpec((None, tile * SUBLANES, LANES),
                        lambda b, i: (blk0 + b * per_seq + i // tiles, i % tiles, 0))


def _outproj_kernel(attn_ref, lru_ref, x_ref, gm_ref, sf_ref, cf_ref, ag_ref, lg_ref, wo_ref,
                    ng_ref, rw_ref, rb_ref, *rest, ts):
    x1_ref, h2_ref, gates_ref, hs_ref, cv_ref = rest[-5:]
    aw = attn_ref.shape[2]
    an = _rms(attn_ref[0], ag_ref[...]).astype(BF16)
    ln = _rms(lru_ref[0], lg_ref[...]).astype(BF16)
    mixed = _dot(an, wo_ref[0:aw, :]) + _dot(ln, wo_ref[aw:, :])
    x1 = x_ref[0] + gm_ref[0] * mixed
    x1_ref[0] = x1
    h2 = _rms(x1, ng_ref[...]) * (1.0 + cf_ref[0]) + sf_ref[0]
    h2_ref[0] = h2.astype(BF16)
    _rows_to_slabs(h2, cv_ref, hs_ref)
    n_exp = rw_ref.shape[0]
    route_t = _route(_dot3(rw_ref[...], h2, NT), rb_ref[...], n_exp, ts)
    route_t = jnp.concatenate([route_t, jnp.zeros((LANES - 2 * TOP_K, ts), F32)], axis=0)
    gates_ref[0] = route_t.T


def _outproj(attn, lru, x, gate_m, shift_f, scale_f, ag, lg, wo_bf, ng, rw_t, rb_col, ts,
             slab_shape, tb, blk0, slab_in):
    nb, s, d = x.shape
    aw = attn.shape[2]
    tok = lambda width: pl.BlockSpec((1, ts, width), lambda b, i: (b, i, 0))
    const = lambda a: pl.BlockSpec(a.shape, lambda b, i: (0,) * a.ndim)
    args = [attn, lru, x, gate_m, shift_f, scale_f, ag, lg, wo_bf, ng, rw_t, rb_col]
    in_specs = [tok(aw), tok(lru.shape[2]), tok(d), _mod_spec(gate_m, ts), _mod_spec(shift_f, ts),
                _mod_spec(scale_f, ts), const(ag), const(lg), const(wo_bf), const(ng),
                const(rw_t), const(rb_col)]
    aliases = {}
    if slab_in is not None:
        aliases = {len(args): 3}
        args.append(slab_in)
        in_specs.append(pl.BlockSpec(memory_space=pl.ANY))
    return pl.pallas_call(
        functools.partial(_outproj_kernel, ts=ts),
        grid=(nb, s // ts),
        in_specs=in_specs,
        out_specs=[tok(d), tok(d), tok(LANES), _slab_spec(s, ts, tb, blk0)],
        out_shape=[jax.ShapeDtypeStruct((nb, s, d), F32), jax.ShapeDtypeStruct((nb, s, d), BF16),
                   jax.ShapeDtypeStruct((nb, s, LANES), F32),
                   jax.ShapeDtypeStruct(slab_shape, F32)],
        scratch_shapes=[pltpu.VMEM((SUBLANES * (ts + 1), LANES), F32)],
        input_output_aliases=aliases,
        compiler_params=_cparams(("parallel", "parallel")),
        name="outproj_router",
    )(*args)


def _swiglu_bf(h, w_gu, w_down):
    gu = _dot(h, w_gu)
    de = gu.shape[1] // 2
    act = _silu(gu[:, :de]) * gu[:, de:]
    return _dot(act.astype(BF16), w_down)


MOE_TILE = 128
MOE_STRIDE = MOE_TILE + 1
MOE_BATCH = 16
MOE_GROUP = 2
MOE_KEY_SHIFT = 12
SMEM_CHUNK = 1024


def _moe_lists(routes, tb, n_exp):
    assert tb < (1 << MOE_KEY_SHIFT)
    ids, wts = [], []
    for r in routes:
        fill = -r.shape[0] % tb
        i = jnp.pad(r[:, 0:TOP_K].astype(jnp.int32), ((0, fill), (0, 0)), constant_values=n_exp)
        ids.append(i.reshape(-1, tb, TOP_K))
        wts.append(jnp.pad(r[:, TOP_K:2 * TOP_K], ((0, fill), (0, 0))).reshape(-1, tb * TOP_K))
    ids, wts = jnp.concatenate(ids, axis=0), jnp.concatenate(wts, axis=0)
    n_blocks = ids.shape[0]
    tloc = jnp.arange(tb, dtype=jnp.int32)[None, :, None]
    keys = ((ids << MOE_KEY_SHIFT) | tloc).reshape(n_blocks, tb * TOP_K)
    npad = MOE_TILE - 1
    dummy = (jnp.arange(n_exp, dtype=jnp.int32) << MOE_KEY_SHIFT) | tb
    dummy = jnp.broadcast_to(jnp.repeat(dummy, npad)[None], (n_blocks, n_exp * npad))
    keys = jnp.concatenate([keys, dummy], axis=1)
    wts = jnp.concatenate([wts, jnp.zeros(dummy.shape, F32)], axis=1)
    keys, wts = lax.sort((keys, wts), dimension=1, num_keys=1)
    n = keys.shape[1]
    lp = -(-n // SMEM_CHUNK) * SMEM_CHUNK
    tok = keys & ((1 << MOE_KEY_SHIFT) - 1)
    flat = lambda a: jnp.pad(a, ((0, 0), (0, lp - n))).reshape(-1)
    src = flat(jnp.where(tok == tb, 0, tok) * SUBLANES)
    dst = flat(tok * SUBLANES)
    cnt = jnp.sum((ids[..., None] == jnp.arange(n_exp, dtype=jnp.int32)).astype(jnp.int32),
                  axis=(1, 2))
    start = jnp.cumsum(cnt + npad, axis=1) - (cnt + npad)
    ntile = (cnt + npad) // MOE_TILE
    return src, dst, flat(wts), start.astype(jnp.int32), ntile.astype(jnp.int32), lp


def _moe_sparse_kernel(start_ref, ntile_ref, src_hbm, dst_hbm, wts_hbm, x_ref, wgu_ref, wdn_ref,
                       out_ref, src_smem, dst_smem, wts_smem, xt_ref, yt_ref, sem):
    b, e = pl.program_id(0), pl.program_id(1)
    lp = src_smem.shape[0]
    m, s = MOE_TILE, MOE_STRIDE
    nch = SUBLANES

    @pl.when(e == 0)
    def _():
        off = pl.multiple_of(b * lp, SMEM_CHUNK)
        copies = [pltpu.make_async_copy(h.at[pl.ds(off, lp)], d, sem.at[k]) for k, (h, d) in
                  enumerate(((src_hbm, src_smem), (dst_hbm, dst_smem), (wts_hbm, wts_smem)))]
        for cp in copies:
            cp.start()
        out_ref[...] = jnp.zeros(out_ref.shape, F32)
        for cp in copies:
            cp.wait()

    base = start_ref[b, e]
    wgu = wgu_ref[0]
    wdn = wdn_ref[0]

    def gather(r0, xt):
        for mi in range(m):
            t8 = pl.multiple_of(src_smem[r0 + mi], SUBLANES)
            xt[pl.ds(mi, SUBLANES, stride=s), :] = x_ref[pl.ds(t8, SUBLANES), :]

    def expert(xt, yt):
        x = jnp.concatenate([xt[pl.ds(j * s, m), :] for j in range(nch)], axis=1)
        y = _swiglu_bf(x.astype(BF16), wgu, wdn)
        for j in range(nch):
            yt[pl.ds(j * s, m), :] = y[:, j * LANES:(j + 1) * LANES]

    def combine(r0, yt):
        for g0 in range(0, m, MOE_BATCH):
            upd = []
            for mi in range(g0, g0 + MOE_BATCH):
                t8 = pl.multiple_of(dst_smem[r0 + mi], SUBLANES)
                slab = yt[pl.ds(mi, SUBLANES, stride=s), :]
                upd.append((t8, out_ref[pl.ds(t8, SUBLANES), :] + wts_smem[r0 + mi] * slab))
            for t8, v in upd:
                out_ref[pl.ds(t8, SUBLANES), :] = v

    def tiles(r0, n):
        bufs = [(xt_ref.at[c], yt_ref.at[c]) for c in range(n)]
        for c, (xt, yt) in enumerate(bufs):
            gather(r0 + c * m, xt)
        for xt, yt in bufs:
            expert(xt, yt)
        for c, (xt, yt) in enumerate(bufs):
            combine(r0 + c * m, yt)

    nt = ntile_ref[b, e]

    def pair(i, carry):
        tiles(base + i * (MOE_GROUP * m), MOE_GROUP)
        return carry

    lax.fori_loop(0, nt // MOE_GROUP, pair, 0)
    for rem in range(1, MOE_GROUP):
        @pl.when(nt % MOE_GROUP == rem)
        def _():
            tiles(base + (nt - rem) * m, rem)


def _moe_sparse(x_slab, src, dst, wts, start, ntile, lp, w_gu_bf, w_down_bf):
    n_blocks, rows, _ = x_slab.shape
    n_exp = w_gu_bf.shape[0]
    any_spec = pl.BlockSpec(memory_space=pl.ANY)
    grid_spec = pltpu.PrefetchScalarGridSpec(
        num_scalar_prefetch=2,
        grid=(n_blocks, n_exp),
        in_specs=[any_spec, any_spec, any_spec,
                  pl.BlockSpec((None, rows, LANES), lambda b, e, *_: (b, 0, 0)),
                  pl.BlockSpec((1,) + w_gu_bf.shape[1:], lambda b, e, *_: (e, 0, 0)),
                  pl.BlockSpec((1,) + w_down_bf.shape[1:], lambda b, e, *_: (e, 0, 0))],
        out_specs=pl.BlockSpec((None, rows + SUBLANES, LANES), lambda b, e, *_: (b, 0, 0)),
        scratch_shapes=[pltpu.SMEM((lp,), jnp.int32), pltpu.SMEM((lp,), jnp.int32),
                        pltpu.SMEM((lp,), F32),
                        pltpu.VMEM((MOE_GROUP, SUBLANES * MOE_STRIDE, LANES), F32),
                        pltpu.VMEM((MOE_GROUP, SUBLANES * MOE_STRIDE, LANES), F32),
                        pltpu.SemaphoreType.DMA((3,))],
    )
    return pl.pallas_call(
        _moe_sparse_kernel,
        grid_spec=grid_spec,
        out_shape=jax.ShapeDtypeStruct((n_blocks, rows + SUBLANES, LANES), F32),
        compiler_params=_cparams(("parallel", "arbitrary")),
        name="moe_routed",
    )(start, ntile, src, dst, wts, x_slab, w_gu_bf, w_down_bf)


def _final_kernel(x1_ref, h2_ref, moe_ref, gf_ref, fg_ref, sgu_ref, sdn_ref, y_ref, cv_ref):
    moe = _slabs_to_rows(moe_ref, cv_ref, x1_ref.shape[1])
    f = _swiglu_bf(h2_ref[0], sgu_ref[...], sdn_ref[...]) + moe
    y_ref[0] = _rms(x1_ref[0] + gf_ref[0] * f, fg_ref[...])


def _final(x1, h2, moe_slab, tb, blk0, gate_f, final_g, sgu_bf, sdn_bf, tm):
    nb, s, d = x1.shape
    tok = pl.BlockSpec((1, tm, d), lambda b, i: (b, i, 0))
    const = lambda a: pl.BlockSpec(a.shape, lambda b, i: (0,) * a.ndim)
    return pl.pallas_call(
        _final_kernel,
        grid=(nb, s // tm),
        in_specs=[tok, tok, _slab_spec(s, tm, tb, blk0),
                  _mod_spec(gate_f, tm), const(final_g), const(sgu_bf), const(sdn_bf)],
        out_specs=tok,
        out_shape=jax.ShapeDtypeStruct((nb, s, d), F32),
        scratch_shapes=[pltpu.VMEM((SUBLANES * (tm + 1), LANES), F32)],
        compiler_params=_cparams(("parallel", "parallel")),
        name="shared_final",
    )(x1, h2, moe_slab, gate_f, final_g, sgu_bf, sdn_bf)


def _rope_tables(start_pos, s):
    half = HEAD_DIM // 2
    inv_freq = ROPE_THETA ** (-jnp.arange(half, dtype=F32) / half)
    ang = (start_pos + jnp.arange(s, dtype=jnp.int32)).astype(F32)[:, None] * inv_freq[None, :]
    cos, sin = jnp.cos(ang), jnp.sin(ang)
    reps = LANES // HEAD_DIM
    return (jnp.tile(jnp.concatenate([cos, cos], axis=1), (1, reps)),
            jnp.tile(jnp.concatenate([-sin, sin], axis=1), (1, reps)))


def _mixer(x, mods, attend, h0, conv0, start_pos, ts, wts, slab):
    shift_m, scale_m, gate_m, shift_f, scale_f, _ = mods
    s = x.shape[1]
    cos, sin = _rope_tables(start_pos, s)
    q, k, v, xl, gl = _inproj(x, shift_m, scale_m, cos, sin, wts["norm_mix_g"], wts["w_in"], ts)
    attn = attend(q, k, v)
    lru, h_last, cbuf = _lru(xl, gl, h0, conv0, wts["conv_w"], wts["conv_b"], wts["wa"],
                             wts["gate_a_b"], wts["wx"], wts["gate_x_b"], wts["lru_lambda"],
                             ts, start_pos)
    x1, h2, route, hs = _outproj(attn, lru, x, gate_m, shift_f, scale_f, wts["attn_out_g"],
                                 wts["lru_out_g"], wts["w_out"], wts["norm_ffn_g"],
                                 wts["router_w_t"], wts["router_bias"], ts, *slab)
    return x1, h2, route, hs, k, v, h_last, cbuf


MOE_BLOCK_TOKENS = 2048


def kernel(x_prompt, x_sample, c_prompt, c_sample, cache_k, cache_v, state_h, state_conv, page_table, ada_w, ada_b, norm_mix_g, w_in, conv_w, conv_b, gate_a_w, gate_a_b, gate_x_w, gate_x_b, lru_lambda, attn_out_g, lru_out_g, w_out, norm_ffn_g, router_w, router_bias, exp_w_gu, exp_w_down, shared_w_gu, shared_w_down, final_g):
    depth = ada_w.shape[0]
    assert depth == 1, "single-layer trunk"
    n_prm, seq, d = x_prompt.shape
    n_dec, dec_seq, _ = x_sample.shape
    n_heads = cache_k.shape[3]
    aw = n_heads * HEAD_DIM
    lw = state_h.shape[2]
    past_len = page_table.shape[1] * PAGE_SIZE
    l = 0
    row = lambda a: a.reshape(1, -1)
    wts = dict(
        norm_mix_g=row(norm_mix_g[l]), w_in=w_in[l].astype(BF16), conv_w=conv_w[l], conv_b=conv_b[l],
        wa=_block_diag(gate_a_w[l]).astype(BF16), gate_a_b=gate_a_b[l],
        wx=_block_diag(gate_x_w[l]).astype(BF16), gate_x_b=gate_x_b[l], lru_lambda=lru_lambda[l],
        attn_out_g=row(attn_out_g[l]), lru_out_g=row(lru_out_g[l]), w_out=w_out[l].astype(BF16),
        norm_ffn_g=row(norm_ffn_g[l]), router_w_t=router_w[l].T,
        router_bias=router_bias[l].reshape(-1, 1), final_g=row(final_g),
        shared_w_gu=shared_w_gu[l].astype(BF16), shared_w_down=shared_w_down[l].astype(BF16))

    n_c = n_prm + n_dec
    c_all = jnp.concatenate([c_prompt, c_sample, jnp.zeros((-n_c % SUBLANES, d), F32)], axis=0)
    mod = _ada_mod(c_all, ada_w[l], ada_b[l])
    mods_p = [m[:n_prm, None, :] for m in jnp.split(mod, 6, axis=-1)]
    mods_s = [jnp.repeat(m[n_prm:n_c], dec_seq, axis=0)[None] for m in jnp.split(mod, 6, axis=-1)]

    tb = MOE_BLOCK_TOKENS
    nt = n_dec * dec_seq
    assert seq % tb == 0 and nt <= tb
    blk_s = n_prm * (seq // tb)
    slab_shape = (blk_s + 1, tb * SUBLANES, LANES)
    x1p, h2p, route_p, slab, k1, v1, h1, c1 = _mixer(
        x_prompt, mods_p, _moba_prompt, jnp.zeros((n_prm, lw), F32),
        jnp.zeros((n_prm, CONV_W - 1, lw), F32), 0, 512, wts,
        (slab_shape, tb, 0, jnp.zeros(slab_shape, F32)))

    ck = jnp.transpose(cache_k[l], (0, 2, 3, 1)).reshape(cache_k.shape[1], aw, PAGE_SIZE)
    cv = jnp.transpose(cache_v[l], (0, 2, 3, 1)).reshape(cache_v.shape[1], aw, PAGE_SIZE)

    def attend_sample(q, k, v):
        r = lambda a: a.reshape(n_dec, dec_seq, aw)
        return _moba_sample(r(q), r(k), r(v), ck, cv, page_table).reshape(1, n_dec * dec_seq, aw)

    xs = x_sample.reshape(1, n_dec * dec_seq, d)
    shift_m, scale_m, gate_m, shift_f, scale_f, gate_f = mods_s
    cos, sin = _rope_tables(past_len, dec_seq)
    cos, sin = jnp.tile(cos, (n_dec, 1)), jnp.tile(sin, (n_dec, 1))
    nt = n_dec * dec_seq
    q, k2, v2, xl, gl = _inproj(xs, shift_m, scale_m, cos, sin, wts["norm_mix_g"], wts["w_in"], nt)
    attn = attend_sample(q, k2, v2)
    lru, h2s, c2 = _lru(xl.reshape(n_dec, dec_seq, lw), gl.reshape(n_dec, dec_seq, lw), state_h[l],
                        state_conv[l], wts["conv_w"], wts["conv_b"], wts["wa"], wts["gate_a_b"],
                        wts["wx"], wts["gate_x_b"], wts["lru_lambda"], dec_seq, past_len)
    x1s, h2s_, route_s, slab = _outproj(
        attn, lru.reshape(1, nt, lw), xs, gate_m, shift_f, scale_f, wts["attn_out_g"],
        wts["lru_out_g"], wts["w_out"], wts["norm_ffn_g"], wts["router_w_t"], wts["router_bias"],
        nt, slab_shape, tb, blk_s, slab)

    n_exp = exp_w_gu.shape[1]
    src, dst, gw, start, ntile, lp = _moe_lists(
        [route_p.reshape(-1, LANES), route_s.reshape(-1, LANES)], tb, n_exp)
    moe_slab = _moe_sparse(slab, src, dst, gw, start, ntile, lp,
                           exp_w_gu[l].astype(BF16), exp_w_down[l].astype(BF16))
    fin = lambda x1, h2, blk0, gf, tm: _final(x1, h2, moe_slab, tb, blk0, gf, wts["final_g"],
                                              wts["shared_w_gu"], wts["shared_w_down"], tm)
    yp = fin(x1p, h2p, 0, mods_p[5], 512)
    ys = fin(x1s, h2s_, blk_s, gate_f, nt)

    return (yp, ys.reshape(n_dec, dec_seq, d),
            k1.reshape(1, n_prm, seq, n_heads, HEAD_DIM), v1.reshape(1, n_prm, seq, n_heads, HEAD_DIM),
            h1.reshape(1, n_prm, lw), c1[None],
            k2.reshape(1, n_dec, dec_seq, n_heads, HEAD_DIM), v2.reshape(1, n_dec, dec_seq, n_heads, HEAD_DIM),
            h2s.reshape(1, n_dec, lw), c2[None])
```

```python
import functools
import math

import jax
import jax.numpy as jnp
from jax import lax
from jax.experimental import pallas as pl
from jax.experimental.pallas import tpu as pltpu

F32 = jnp.float32
BF16 = jnp.bfloat16

HEAD_DIM = 64
LRU_BLOCKS = 8
CONV_W = 4
LRU_C = 8.0
MOBA_BLOCK = 256
MOBA_TOPK = 3
ROPE_THETA = 10000.0
N_GROUPS = 8
TOPK_GROUPS = 4
TOP_K = 8
ROUTED_SCALE = 2.5
EPS = 1e-6
PAGE_SIZE = 128

LANES = 128
SUBLANES = 8
NEG = -1e30
VMEM_LIMIT = 56 * 1024 * 1024

NN = (((1,), (0,)), ((), ()))
NT = (((1,), (1,)), ((), ()))


def _cparams(sem):
    return pltpu.CompilerParams(dimension_semantics=sem, vmem_limit_bytes=VMEM_LIMIT)


def _split(x):
    hi = x.astype(BF16)
    lo = (x - hi.astype(F32)).astype(BF16)
    return hi, lo


def _dot3(a, b, dims):
    ah, al = _split(a)
    bh, bl = _split(b)
    d = lambda u, v: lax.dot_general(u, v, dims, preferred_element_type=F32)
    return d(ah, bh) + (d(ah, bl) + d(al, bh))


def _dot(a, b, dims=NN):
    return lax.dot_general(a, b, dims, preferred_element_type=F32)


def _rms(x, g):
    return x * lax.rsqrt(jnp.mean(x * x, axis=-1, keepdims=True) + EPS) * g


def _silu(x):
    return x * jax.nn.sigmoid(x)


def _gelu_tanh(x):
    c = math.sqrt(2.0 / math.pi)
    return 0.5 * x * (1.0 + jnp.tanh(c * (x + 0.044715 * (x * x * x))))


def _ada_kernel(c_ref, w_ref, b_ref, o_ref):
    o_ref[...] = _dot3(_silu(c_ref[...]), w_ref[...], NN) + b_ref[...]


def _ada_mod(c, ada_w, ada_b):
    r, d = c.shape
    n = ada_w.shape[1]
    tn = 1536
    return pl.pallas_call(
        _ada_kernel,
        grid=(n // tn,),
        in_specs=[pl.BlockSpec((r, d), lambda j: (0, 0)),
                  pl.BlockSpec((d, tn), lambda j: (0, j)),
                  pl.BlockSpec((1, tn), lambda j: (0, j))],
        out_specs=pl.BlockSpec((r, tn), lambda j: (0, j)),
        out_shape=jax.ShapeDtypeStruct((r, n), F32),
        compiler_params=_cparams(("parallel",)),
        name="ada_mod",
    )(c, ada_w, ada_b.reshape(1, n))


def _inproj_kernel(x_ref, shift_ref, scale_ref, cos_ref, sin_ref, g_ref, w_ref,
                   q_ref, k_ref, v_ref, xl_ref, gl_ref, *, aw):
    x = x_ref[0]
    h = _rms(x, g_ref[...]) * (1.0 + scale_ref[0]) + shift_ref[0]
    proj = _dot(h.astype(BF16), w_ref[...])
    cos = cos_ref[...]
    sin = sin_ref[...]
    lane = lax.broadcasted_iota(jnp.int32, cos.shape, 1)
    first = (lane % HEAD_DIM) < (HEAD_DIM // 2)

    def rope(t):
        outs = []
        for c in range(aw // LANES):
            tc = t[:, LANES * c:LANES * (c + 1)]
            partner = jnp.where(first, pltpu.roll(tc, LANES - HEAD_DIM // 2, 1),
                                pltpu.roll(tc, HEAD_DIM // 2, 1))
            outs.append(tc * cos + partner * sin)
        return jnp.concatenate(outs, axis=1)

    q_ref[0] = rope(proj[:, 0:aw])
    k_ref[0] = rope(proj[:, aw:2 * aw])
    v_ref[0] = proj[:, 2 * aw:3 * aw]
    lw = (proj.shape[1] - 3 * aw) // 2
    xl_ref[0] = proj[:, 3 * aw:3 * aw + lw]
    gl_ref[0] = proj[:, 3 * aw + lw:]


def _mod_spec(mod, ts):
    if mod.shape[1] == 1:
        return pl.BlockSpec((1, 1, mod.shape[2]), lambda b, i: (b, 0, 0))
    return pl.BlockSpec((1, ts, mod.shape[2]), lambda b, i: (b, i, 0))


def _inproj(x, shift, scale, cos, sin, g, w_bf, ts):
    nb, s, d = x.shape
    aw = 512
    lw = (w_bf.shape[1] - 3 * aw) // 2
    tok = lambda width: pl.BlockSpec((1, ts, width), lambda b, i: (b, i, 0))
    shp = lambda width: jax.ShapeDtypeStruct((nb, s, width), F32)
    return pl.pallas_call(
        functools.partial(_inproj_kernel, aw=aw),
        grid=(nb, s // ts),
        in_specs=[tok(d), _mod_spec(shift, ts), _mod_spec(scale, ts),
                  pl.BlockSpec((ts, LANES), lambda b, i: (i, 0)),
                  pl.BlockSpec((ts, LANES), lambda b, i: (i, 0)),
                  pl.BlockSpec((1, d), lambda b, i: (0, 0)),
                  pl.BlockSpec(w_bf.shape, lambda b, i: (0, 0))],
        out_specs=[tok(aw), tok(aw), tok(aw), tok(lw), tok(lw)],
        out_shape=[shp(aw), shp(aw), shp(aw), shp(lw), shp(lw)],
        compiler_params=_cparams(("parallel", "parallel")),
        name="inproj",
    )(x, shift, scale, cos, sin, g, w_bf)


def _moba_prompt_kernel(q_ref, k_ref, v_ref, o_ref, kbf_ref, vst_ref, km_ref, selb_ref,
                        sa_ref, sb_ref):
    s_len = k_ref.shape[1]
    tq = MOBA_BLOCK
    nblk = s_len // MOBA_BLOCK
    half = lax.broadcasted_iota(jnp.int32, (LANES, tq), 0) < HEAD_DIM
    kbf_ref[...] = k_ref[0].astype(BF16)
    km_ref[...] = jnp.zeros(km_ref.shape, F32)
    for n in range(nblk):
        rows = slice(n * MOBA_BLOCK, (n + 1) * MOBA_BLOCK)
        km_ref[n:n + 1, :] = jnp.mean(k_ref[0, rows, :], axis=0, keepdims=True)
        vt = v_ref[0, rows, :].T
        vst_ref[n] = jnp.concatenate([jnp.where(half, vt, 0.0), jnp.where(half, 0.0, vt)],
                                     axis=1).astype(BF16)
    nrow = km_ref.shape[0]
    n_iota = lax.broadcasted_iota(jnp.int32, (nrow, 2 * tq), 0)
    keyi = lax.broadcasted_iota(jnp.int32, (MOBA_BLOCK, 2 * tq), 0)
    qi = lax.broadcasted_iota(jnp.int32, (MOBA_BLOCK, 2 * tq), 1) % tq

    def stack_heads(pb):
        return jnp.concatenate([pb[:, :tq], pb[:, tq:]], axis=0)

    def per_head(rowvec):
        return jnp.where(half, rowvec[:, :tq], rowvec[:, tq:])

    def qtile(t, carry):
        r0 = pl.multiple_of(t * tq, tq)
        qt = q_ref[0, pl.ds(r0, tq), :].T
        qt2 = jnp.concatenate([jnp.where(half, qt, 0.0), jnp.where(half, 0.0, qt)], axis=1)
        gm = jnp.where(n_iota < t, _dot3(km_ref[...], qt2, NN), -jnp.inf)
        rank = jnp.zeros(gm.shape, jnp.int32)
        for m_ in range(nblk):
            row = gm[m_:m_ + 1, :]
            beats = (row > gm) | ((row == gm) & (m_ < n_iota))
            rank = rank + beats.astype(jnp.int32)
        selb_ref[...] = jnp.where((n_iota < t) & (rank < MOBA_TOPK), 0.0, NEG)
        qb = (qt2 * (HEAD_DIM ** -0.5)).astype(BF16)
        st = jnp.where(keyi <= qi, _dot(kbf_ref[pl.ds(r0, MOBA_BLOCK), :], qb), NEG)
        m = jnp.max(st, axis=0, keepdims=True)
        p = jnp.exp(st - m)
        l = jnp.sum(p, axis=0, keepdims=True)
        acc = _dot(vst_ref[t], stack_heads(p.astype(BF16)))

        def scores(j, dst):
            j = jnp.minimum(j, nblk - 1)
            j0 = pl.multiple_of(j * MOBA_BLOCK, MOBA_BLOCK)
            dst[...] = _dot(kbf_ref[pl.ds(j0, MOBA_BLOCK), :], qb) + selb_ref[pl.ds(j, 1), :]

        def attend(src, j, m, l, acc):
            st = src[...]
            m_new = jnp.maximum(m, jnp.max(st, axis=0, keepdims=True))
            alpha = jnp.exp(m - m_new)
            p = jnp.exp(st - m_new)
            l = alpha * l + jnp.sum(p, axis=0, keepdims=True)
            acc = per_head(alpha) * acc + _dot(vst_ref[j], stack_heads(p.astype(BF16)))
            return m_new, l, acc

        scores(0, sa_ref)

        def body(i, c):
            scores(2 * i + 1, sb_ref)
            c = attend(sa_ref, 2 * i, *c)
            scores(2 * i + 2, sa_ref)
            return attend(sb_ref, jnp.minimum(2 * i + 1, nblk - 1), *c)

        m, l, acc = lax.fori_loop(0, (t + 1) // 2, body, (m, l, acc))
        o_ref[0, pl.ds(r0, tq), :] = (acc / per_head(l)).T
        return carry

    lax.fori_loop(0, s_len // tq, qtile, 0)


def _moba_prompt(q, k, v):
    nb, s, aw = q.shape
    nblk = s // MOBA_BLOCK
    nrow = -(-nblk // 16) * 16
    spec = pl.BlockSpec((1, s, LANES), lambda b, h: (b, 0, h))
    return pl.pallas_call(
        _moba_prompt_kernel,
        grid=(nb, aw // LANES),
        in_specs=[spec, spec, spec],
        out_specs=spec,
        out_shape=jax.ShapeDtypeStruct((nb, s, aw), F32),
        scratch_shapes=[pltpu.VMEM((s, LANES), BF16),
                        pltpu.VMEM((nblk, LANES, 2 * MOBA_BLOCK), BF16),
                        pltpu.VMEM((nrow, LANES), F32),
                        pltpu.VMEM((nrow, 2 * MOBA_BLOCK), F32),
                        pltpu.VMEM((MOBA_BLOCK, 2 * MOBA_BLOCK), F32),
                        pltpu.VMEM((MOBA_BLOCK, 2 * MOBA_BLOCK), F32)],
        compiler_params=_cparams(("parallel", "parallel")),
        name="moba_prompt",
    )(q, k, v)


def _moba_sample_kernel(pt_ref, q_ref, kn_ref, vn_ref, *rest, n_heads, n_q, bps):
    ppb = MOBA_BLOCK // PAGE_SIZE
    npg = bps * ppb
    k_refs, v_refs = rest[:npg], rest[npg:2 * npg]
    o_ref, qbd_ref, km_ref, m_ref, l_ref, o_scr = rest[2 * npg:]
    step = pl.program_id(1)
    nstep = pl.num_programs(1)
    rows = n_heads * n_q
    aw = q_ref.shape[2]
    rowi = lax.broadcasted_iota(jnp.int32, (rows, aw), 0)
    coli = lax.broadcasted_iota(jnp.int32, (rows, aw), 1)
    diag = (rowi // n_q) == (coli // HEAD_DIM)
    lane = lax.broadcasted_iota(jnp.int32, (rows, LANES), 1)

    @pl.when(step == 0)
    def _():
        qt = jnp.concatenate([q_ref[0]] * n_heads, axis=0)
        qbd_ref[...] = jnp.where(diag, qt, 0.0)
        km_ref[...] = jnp.zeros(km_ref.shape, F32)
        m_ref[...] = jnp.full(m_ref.shape, NEG, F32)
        l_ref[...] = jnp.zeros(l_ref.shape, F32)

    qbd = qbd_ref[...]
    qb = (qbd * (HEAD_DIM ** -0.5)).astype(BF16)
    klane = lax.broadcasted_iota(jnp.int32, km_ref.shape, 1)
    km, mm_, ll_ = km_ref[...], m_ref[...], l_ref[...]
    blocks = range(bps)
    kts = [jnp.concatenate([r[0] for r in k_refs[c * ppb:(c + 1) * ppb]], axis=1) for c in blocks]
    ss = [_dot(qb, kt.astype(BF16)) for kt in kts]
    ms = [jnp.max(s, axis=1, keepdims=True) for s in ss]
    ps = [jnp.exp(s - mj) for s, mj in zip(ss, ms)]
    ls = [jnp.sum(p, axis=1, keepdims=True) for p in ps]
    for c in blocks:
        j = step * bps + c
        vt = jnp.concatenate([r[0] for r in v_refs[c * ppb:(c + 1) * ppb]], axis=1)
        o_scr[j] = _dot(ps[c].astype(BF16), vt.astype(BF16), NT)
        km = jnp.where(klane == j, jnp.mean(kts[c], axis=1, keepdims=True), km)
        mm_ = jnp.where(lane == j, ms[c], mm_)
        ll_ = jnp.where(lane == j, ls[c], ll_)
    km_ref[...], m_ref[...], l_ref[...] = km, mm_, ll_

    @pl.when(step == nstep - 1)
    def _():
        npast = o_scr.shape[0]
        gate = _dot3(qbd, km_ref[...], NN)
        valid = lane < npast
        gm = jnp.where(valid, gate, -jnp.inf)
        rank = jnp.zeros((rows, LANES), jnp.int32)
        for mm in range(npast):
            colv = gm[:, mm:mm + 1]
            beats = (colv > gm) | ((colv == gm) & (mm < lane))
            rank = rank + beats.astype(jnp.int32)
        sel = valid & (rank < MOBA_TOPK)
        so = _dot(qb, kn_ref[0].astype(BF16), NT)
        qi = lax.broadcasted_iota(jnp.int32, (rows, n_q), 0) % n_q
        ki = lax.broadcasted_iota(jnp.int32, (rows, n_q), 1)
        so = jnp.where(ki <= qi, so, NEG)
        mo = jnp.max(so, axis=1, keepdims=True)
        po = jnp.exp(so - mo)
        lo = jnp.sum(po, axis=1, keepdims=True)
        oo = _dot(po.astype(BF16), vn_ref[0].astype(BF16))
        mall = jnp.where(sel, m_ref[...], NEG)
        mx = jnp.maximum(jnp.max(mall, axis=1, keepdims=True), mo)
        w = jnp.where(sel, jnp.exp(mall - mx), 0.0)
        wo = jnp.exp(mo - mx)
        den = jnp.sum(w * l_ref[...], axis=1, keepdims=True) + wo * lo
        num = wo * oo
        for jj in range(npast):
            num = num + w[:, jj:jj + 1] * o_scr[jj]
        res = jnp.where(diag, num / den, 0.0)
        out = res[0:n_q, :]
        for hh in range(1, n_heads):
            out = out + res[hh * n_q:(hh + 1) * n_q, :]
        o_ref[0] = out


def _moba_sample(q, k_new, v_new, cache_kt, cache_vt, page_table):
    nb, n_q, aw = q.shape
    n_heads = aw // HEAD_DIM
    n_pages = page_table.shape[1]
    ppb = MOBA_BLOCK // PAGE_SIZE
    npast = n_pages // ppb
    bps = next(n for n in (4, 2, 1) if npast % n == 0)
    npg = bps * ppb
    rows = n_heads * n_q
    tok = pl.BlockSpec((1, n_q, aw), lambda b, j, pt: (b, 0, 0))
    page = lambda o: pl.BlockSpec((1, aw, PAGE_SIZE), lambda b, j, pt: (pt[b, npg * j + o], 0, 0))
    pages = [page(o) for o in range(npg)]
    grid_spec = pltpu.PrefetchScalarGridSpec(
        num_scalar_prefetch=1,
        grid=(nb, npast // bps),
        in_specs=[tok, tok, tok] + pages + pages,
        out_specs=tok,
        scratch_shapes=[pltpu.VMEM((rows, aw), F32), pltpu.VMEM((aw, LANES), F32),
                        pltpu.VMEM((rows, LANES), F32), pltpu.VMEM((rows, LANES), F32),
                        pltpu.VMEM((npast, rows, aw), F32)],
    )
    return pl.pallas_call(
        functools.partial(_moba_sample_kernel, n_heads=n_heads, n_q=n_q, bps=bps),
        grid_spec=grid_spec,
        out_shape=jax.ShapeDtypeStruct((nb, n_q, aw), F32),
        compiler_params=_cparams(("parallel", "arbitrary")),
        name="moba_sample",
    )(page_table, q, k_new, v_new, *([cache_kt] * npg), *([cache_vt] * npg))


def _lru_kernel(xl_ref, gl_ref, h0_ref, c0_ref, cw_ref, cb_ref, wa_ref, ba_ref, wx_ref, bx_ref,
                lam_ref, y_ref, hlast_ref, cbuf_ref, xbuf_ref, a_ref, b_ref, hcar_ref,
                *, ts, start_pos):
    i = pl.program_id(1)
    w = xl_ref.shape[2]
    hist = SUBLANES

    @pl.when(i == 0)
    def _():
        xbuf_ref[0:hist, :] = jnp.zeros((hist, w), F32)
        xbuf_ref[hist - (CONV_W - 1):hist, :] = c0_ref[0]
        hcar_ref[...] = jnp.broadcast_to(h0_ref[0], (SUBLANES, w))

    x = xl_ref[0]
    xbuf_ref[hist:hist + ts, :] = x
    conv = cb_ref[...] + cw_ref[0:1, :] * xbuf_ref[hist - 3:hist - 3 + ts, :]
    conv = conv + cw_ref[1:2, :] * xbuf_ref[hist - 2:hist - 2 + ts, :]
    conv = conv + cw_ref[2:3, :] * xbuf_ref[hist - 1:hist - 1 + ts, :]
    conv = conv + cw_ref[3:4, :] * x
    cbuf_ref[0] = xbuf_ref[hist + ts - (CONV_W - 1):hist + ts, :]
    xbuf_ref[0:hist, :] = xbuf_ref[ts:ts + hist, :]

    cbf = conv.astype(BF16)
    r = jax.nn.sigmoid(_dot(cbf, wa_ref[...]) + ba_ref[...])
    ig = jax.nn.sigmoid(_dot(cbf, wx_ref[...]) + bx_ref[...])
    nl = -lam_ref[...]
    softplus = jnp.maximum(nl, 0.0) + jnp.log1p(jnp.exp(-jnp.abs(nl)))
    log_a = (-LRU_C * r) * softplus
    a = jnp.exp(log_a)
    mult = jnp.sqrt(1.0 - jnp.exp(2.0 * log_a))
    if start_pos == 0:
        reset = (i * ts + lax.broadcasted_iota(jnp.int32, (ts, w), 0)) == 0
        a = jnp.where(reset, 0.0, a)
        mult = jnp.where(reset, 1.0, mult)
    a_ref[...] = a
    b_ref[...] = mult * ig * conv

    rowi = lax.broadcasted_iota(jnp.int32, (SUBLANES, w), 0)

    def group(gi, h):
        o = pl.multiple_of(gi * SUBLANES, SUBLANES)
        a8 = a_ref[pl.ds(o, SUBLANES), :]
        b8 = b_ref[pl.ds(o, SUBLANES), :]
        for d in (1, 2, 4):
            keep = rowi >= d
            a_sh = jnp.where(keep, pltpu.roll(a8, d, 0), 1.0)
            b_sh = jnp.where(keep, pltpu.roll(b8, d, 0), 0.0)
            b8 = a8 * b_sh + b8
            a8 = a8 * a_sh
        h8 = a8 * h + b8
        b_ref[pl.ds(o, SUBLANES), :] = h8
        return jnp.broadcast_to(h8[SUBLANES - 1:SUBLANES, :], (SUBLANES, w))

    h = lax.fori_loop(0, ts // SUBLANES, group, hcar_ref[...])
    hcar_ref[...] = h
    y_ref[0] = b_ref[...] * _gelu_tanh(gl_ref[0])
    hlast_ref[0] = h[0:1, :]


def _block_diag(wb):
    nb, d, _ = wb.shape
    eye = jnp.eye(nb, dtype=wb.dtype)
    return (wb[:, :, None, :] * eye[:, None, :, None]).reshape(nb * d, nb * d)


def _lru(xl, gl, h0, conv0, conv_w, conv_b, wa_bd, ba, wx_bd, bx, lam, ts, start_pos):
    nb, s, w = xl.shape
    tok = pl.BlockSpec((1, ts, w), lambda b, i: (b, i, 0))
    row = pl.BlockSpec((1, w), lambda b, i: (0, 0))
    mat = pl.BlockSpec((w, w), lambda b, i: (0, 0))
    return pl.pallas_call(
        functools.partial(_lru_kernel, ts=ts, start_pos=start_pos),
        grid=(nb, s // ts),
        in_specs=[tok, tok,
                  pl.BlockSpec((1, 1, w), lambda b, i: (b, 0, 0)),
                  pl.BlockSpec((1, CONV_W - 1, w), lambda b, i: (b, 0, 0)),
                  pl.BlockSpec((CONV_W, w), lambda b, i: (0, 0)),
                  row, mat, row, mat, row, row],
        out_specs=[tok,
                   pl.BlockSpec((1, 1, w), lambda b, i: (b, 0, 0)),
                   pl.BlockSpec((1, CONV_W - 1, w), lambda b, i: (b, 0, 0))],
        out_shape=[jax.ShapeDtypeStruct((nb, s, w), F32),
                   jax.ShapeDtypeStruct((nb, 1, w), F32),
                   jax.ShapeDtypeStruct((nb, CONV_W - 1, w), F32)],
        scratch_shapes=[pltpu.VMEM((ts + SUBLANES, w), F32), pltpu.VMEM((ts, w), F32),
                        pltpu.VMEM((ts, w), F32), pltpu.VMEM((SUBLANES, w), F32)],
        compiler_params=_cparams(("parallel", "arbitrary")),
        name="rg_lru",
    )(xl, gl, h0.reshape(nb, 1, w), conv0, conv_w, conv_b.reshape(1, w), wa_bd,
      ba.reshape(1, w), wx_bd, bx.reshape(1, w), lam.reshape(1, w))


def _route(logits_t, bias_col, n_exp, ts):
    scores = jax.nn.sigmoid(logits_t)
    biased = scores + bias_col
    gsz = n_exp // N_GROUPS
    gscore = []
    for g in range(N_GROUPS):
        blk = biased[g * gsz:(g + 1) * gsz, :]
        m1 = jnp.max(blk, axis=0, keepdims=True)
        cnt = jnp.sum((blk == m1).astype(F32), axis=0, keepdims=True)
        m2 = jnp.max(jnp.where(blk < m1, blk, -jnp.inf), axis=0, keepdims=True)
        gscore.append(m1 + jnp.where(cnt >= 2.0, m1, m2))
    cands = []
    for g in range(N_GROUPS):
        rank = jnp.zeros((1, ts), jnp.int32)
        for g2 in range(N_GROUPS):
            if g2 == g:
                continue
            beats = (gscore[g2] > gscore[g]) if g2 > g else (gscore[g2] >= gscore[g])
            rank = rank + beats.astype(jnp.int32)
        blk = biased[g * gsz:(g + 1) * gsz, :]
        cands.append(jnp.where(rank < TOPK_GROUPS, blk, -jnp.inf))
    cand = jnp.concatenate(cands, axis=0)
    e_f = lax.broadcasted_iota(jnp.int32, (n_exp, ts), 0).astype(F32)
    ids, sel = [], []
    for _ in range(TOP_K):
        top = jnp.max(cand, axis=0, keepdims=True)
        idx = jnp.min(jnp.where(cand == top, e_f, float(n_exp)), axis=0, keepdims=True)
        hit = e_f == idx
        ids.append(idx)
        sel.append(jnp.sum(jnp.where(hit, scores, 0.0), axis=0, keepdims=True))
        cand = jnp.where(hit, -jnp.inf, cand)
    total = functools.reduce(lambda a, c: a + c, sel)
    wts = [ROUTED_SCALE * w / total for w in sel]
    return jnp.concatenate(ids + wts, axis=0)


def _rows_to_slabs(x, cv_ref, slab_ref):
    n = x.shape[0]
    s = n + 1
    for j in range(SUBLANES):
        cv_ref[pl.ds(j * s, n), :] = x[:, j * LANES:(j + 1) * LANES]
    for r in range(n):
        slab_ref[pl.ds(SUBLANES * r, SUBLANES), :] = cv_ref[pl.ds(r, SUBLANES, stride=s), :]


def _slabs_to_rows(slab_ref, cv_ref, n):
    s = n + 1
    for r in range(n):
        cv_ref[pl.ds(r, SUBLANES, stride=s), :] = slab_ref[pl.ds(SUBLANES * r, SUBLANES), :]
    return jnp.concatenate([cv_ref[pl.ds(j * s, n), :] for j in range(SUBLANES)], axis=1)


def _slab_spec(s, tile, tb, blk0):
    per_seq = -(-s // tb)
    tiles = min(tb, s) // tile
    return pl.BlockSpec((None, tile * SUBLANES, LANES),
                        lambda b, i: (blk0 + b * per_seq + i // tiles, i % tiles, 0))


def _outproj_kernel(attn_ref, lru_ref, x_ref, gm_ref, sf_ref, cf_ref, ag_ref, lg_ref, wo_ref,
                    ng_ref, rw_ref, rb_ref, *rest, ts):
    x1_ref, h2_ref, gates_ref, hs_ref, cv_ref = rest[-5:]
    aw = attn_ref.shape[2]
    an = _rms(attn_ref[0], ag_ref[...]).astype(BF16)
    ln = _rms(lru_ref[0], lg_ref[...]).astype(BF16)
    mixed = _dot(an, wo_ref[0:aw, :]) + _dot(ln, wo_ref[aw:, :])
    x1 = x_ref[0] + gm_ref[0] * mixed
    x1_ref[0] = x1
    h2 = _rms(x1, ng_ref[...]) * (1.0 + cf_ref[0]) + sf_ref[0]
    h2_ref[0] = h2.astype(BF16)
    _rows_to_slabs(h2, cv_ref, hs_ref)
    n_exp = rw_ref.shape[0]
    route_t = _route(_dot3(rw_ref[...], h2, NT), rb_ref[...], n_exp, ts)
    route_t = jnp.concatenate([route_t, jnp.zeros((LANES - 2 * TOP_K, ts), F32)], axis=0)
    gates_ref[0] = route_t.T


def _outproj(attn, lru, x, gate_m, shift_f, scale_f, ag, lg, wo_bf, ng, rw_t, rb_col, ts,
             slab_shape, tb, blk0, slab_in):
    nb, s, d = x.shape
    aw = attn.shape[2]
    tok = lambda width: pl.BlockSpec((1, ts, width), lambda b, i: (b, i, 0))
    const = lambda a: pl.BlockSpec(a.shape, lambda b, i: (0,) * a.ndim)
    args = [attn, lru, x, gate_m, shift_f, scale_f, ag, lg, wo_bf, ng, rw_t, rb_col]
    in_specs = [tok(aw), tok(lru.shape[2]), tok(d), _mod_spec(gate_m, ts), _mod_spec(shift_f, ts),
                _mod_spec(scale_f, ts), const(ag), const(lg), const(wo_bf), const(ng),
                const(rw_t), const(rb_col)]
    aliases = {}
    if slab_in is not None:
        aliases = {len(args): 3}
        args.append(slab_in)
        in_specs.append(pl.BlockSpec(memory_space=pl.ANY))
    return pl.pallas_call(
        functools.partial(_outproj_kernel, ts=ts),
        grid=(nb, s // ts),
        in_specs=in_specs,
        out_specs=[tok(d), tok(d), tok(LANES), _slab_spec(s, ts, tb, blk0)],
        out_shape=[jax.ShapeDtypeStruct((nb, s, d), F32), jax.ShapeDtypeStruct((nb, s, d), BF16),
                   jax.ShapeDtypeStruct((nb, s, LANES), F32),
                   jax.ShapeDtypeStruct(slab_shape, F32)],
        scratch_shapes=[pltpu.VMEM((SUBLANES * (ts + 1), LANES), F32)],
        input_output_aliases=aliases,
        compiler_params=_cparams(("parallel", "parallel")),
        name="outproj_router",
    )(*args)


def _swiglu_bf(h, w_gu, w_down):
    gu = _dot(h, w_gu)
    de = gu.shape[1] // 2
    act = _silu(gu[:, :de]) * gu[:, de:]
    return _dot(act.astype(BF16), w_down)


MOE_TILE = 128
MOE_STRIDE = MOE_TILE + 1
MOE_BATCH = 16
MOE_GROUP = 2
MOE_EXPERTS_PER_STEP = 4
MOE_KEY_SHIFT = 12
SMEM_CHUNK = 1024


def _moe_lists(routes, tb, n_exp):
    assert tb < (1 << MOE_KEY_SHIFT)
    ids, wts = [], []
    for r in routes:
        fill = -r.shape[0] % tb
        i = jnp.pad(r[:, 0:TOP_K].astype(jnp.int32), ((0, fill), (0, 0)), constant_values=n_exp)
        ids.append(i.reshape(-1, tb, TOP_K))
        wts.append(jnp.pad(r[:, TOP_K:2 * TOP_K], ((0, fill), (0, 0))).reshape(-1, tb * TOP_K))
    ids, wts = jnp.concatenate(ids, axis=0), jnp.concatenate(wts, axis=0)
    n_blocks = ids.shape[0]
    tloc = jnp.arange(tb, dtype=jnp.int32)[None, :, None]
    keys = ((ids << MOE_KEY_SHIFT) | tloc).reshape(n_blocks, tb * TOP_K)
    npad = MOE_TILE - 1
    dummy = (jnp.arange(n_exp, dtype=jnp.int32) << MOE_KEY_SHIFT) | tb
    dummy = jnp.broadcast_to(jnp.repeat(dummy, npad)[None], (n_blocks, n_exp * npad))
    keys = jnp.concatenate([keys, dummy], axis=1)
    wts = jnp.concatenate([wts, jnp.zeros(dummy.shape, F32)], axis=1)
    keys, wts = lax.sort((keys, wts), dimension=1, num_keys=1)
    n = keys.shape[1]
    lp = -(-n // SMEM_CHUNK) * SMEM_CHUNK
    tok = keys & ((1 << MOE_KEY_SHIFT) - 1)
    flat = lambda a: jnp.pad(a, ((0, 0), (0, lp - n))).reshape(-1)
    src = flat(jnp.where(tok == tb, 0, tok) * SUBLANES)
    dst = flat(tok * SUBLANES)
    cnt = jnp.sum((ids[..., None] == jnp.arange(n_exp, dtype=jnp.int32)).astype(jnp.int32),
                  axis=(1, 2))
    start = jnp.cumsum(cnt + npad, axis=1) - (cnt + npad)
    ntile = (cnt + npad) // MOE_TILE
    tend = jnp.cumsum(ntile, axis=1)
    tfirst = tend - ntile
    nt_max = (tb * TOP_K + n_exp * npad) // MOE_TILE
    ti = jnp.arange(nt_max, dtype=jnp.int32)[None, :]
    e_i = jnp.minimum(jnp.sum((ti[:, :, None] >= tend[:, None, :]).astype(jnp.int32), axis=2),
                      n_exp - 1)
    row0 = (jnp.take_along_axis(start, e_i, axis=1)
            + (ti - jnp.take_along_axis(tfirst, e_i, axis=1)) * MOE_TILE)
    grp = jnp.concatenate([tfirst[:, ::MOE_EXPERTS_PER_STEP], tend[:, -1:]], axis=1)
    i32 = lambda a: a.astype(jnp.int32)
    return src, dst, flat(wts), i32(grp), i32(row0), i32(e_i % MOE_EXPERTS_PER_STEP), lp


def _moe_sparse_kernel(grp_ref, row0_ref, eloc_ref, src_hbm, dst_hbm, wts_hbm, x_ref, wgu_ref,
                       wdn_ref, out_ref, src_smem, dst_smem, wts_smem, xt_ref, yt_ref, sem):
    b, e = pl.program_id(0), pl.program_id(1)
    lp = src_smem.shape[0]
    m, s = MOE_TILE, MOE_STRIDE
    nch = SUBLANES

    @pl.when(e == 0)
    def _():
        off = pl.multiple_of(b * lp, SMEM_CHUNK)
        copies = [pltpu.make_async_copy(h.at[pl.ds(off, lp)], d, sem.at[k]) for k, (h, d) in
                  enumerate(((src_hbm, src_smem), (dst_hbm, dst_smem), (wts_hbm, wts_smem)))]
        for cp in copies:
            cp.start()
        out_ref[...] = jnp.zeros(out_ref.shape, F32)
        for cp in copies:
            cp.wait()

    def gather(r0, xt):
        for mi in range(m):
            t8 = pl.multiple_of(src_smem[r0 + mi], SUBLANES)
            xt[pl.ds(mi, SUBLANES, stride=s), :] = x_ref[pl.ds(t8, SUBLANES), :]

    def expert(xt, yt, el):
        x = jnp.concatenate([xt[pl.ds(j * s, m), :] for j in range(nch)], axis=1)
        y = _swiglu_bf(x.astype(BF16), wgu_ref[el], wdn_ref[el])
        for j in range(nch):
            yt[pl.ds(j * s, m), :] = y[:, j * LANES:(j + 1) * LANES]

    def combine(r0, yt):
        for g0 in range(0, m, MOE_BATCH):
            upd = []
            for mi in range(g0, g0 + MOE_BATCH):
                t8 = pl.multiple_of(dst_smem[r0 + mi], SUBLANES)
                slab = yt[pl.ds(mi, SUBLANES, stride=s), :]
                upd.append((t8, out_ref[pl.ds(t8, SUBLANES), :] + wts_smem[r0 + mi] * slab))
            for t8, v in upd:
                out_ref[pl.ds(t8, SUBLANES), :] = v

    def tiles(t, n):
        r0s = [row0_ref[b, t + c] for c in range(n)]
        for c in range(n):
            gather(r0s[c], xt_ref.at[c])
        for c in range(n):
            expert(xt_ref.at[c], yt_ref.at[c], eloc_ref[b, t + c])
        for c in range(n):
            combine(r0s[c], yt_ref.at[c])

    t0 = grp_ref[b, e]
    nt = grp_ref[b, e + 1] - t0

    def group(i, carry):
        tiles(t0 + i * MOE_GROUP, MOE_GROUP)
        return carry

    lax.fori_loop(0, nt // MOE_GROUP, group, 0)
    for rem in range(1, MOE_GROUP):
        @pl.when(nt % MOE_GROUP == rem)
        def _():
            tiles(t0 + nt - rem, rem)


def _moe_sparse(x_slab, src, dst, wts, grp, row0, eloc, lp, w_gu_bf, w_down_bf):
    n_blocks, rows, _ = x_slab.shape
    n_exp = w_gu_bf.shape[0]
    eps = MOE_EXPERTS_PER_STEP
    any_spec = pl.BlockSpec(memory_space=pl.ANY)
    grid_spec = pltpu.PrefetchScalarGridSpec(
        num_scalar_prefetch=3,
        grid=(n_blocks, n_exp // eps),
        in_specs=[any_spec, any_spec, any_spec,
                  pl.BlockSpec((None, rows, LANES), lambda b, e, *_: (b, 0, 0)),
                  pl.BlockSpec((eps,) + w_gu_bf.shape[1:], lambda b, e, *_: (e, 0, 0)),
                  pl.BlockSpec((eps,) + w_down_bf.shape[1:], lambda b, e, *_: (e, 0, 0))],
        out_specs=pl.BlockSpec((None, rows + SUBLANES, LANES), lambda b, e, *_: (b, 0, 0)),
        scratch_shapes=[pltpu.SMEM((lp,), jnp.int32), pltpu.SMEM((lp,), jnp.int32),
                        pltpu.SMEM((lp,), F32),
                        pltpu.VMEM((MOE_GROUP, SUBLANES * MOE_STRIDE, LANES), F32),
                        pltpu.VMEM((MOE_GROUP, SUBLANES * MOE_STRIDE, LANES), F32),
                        pltpu.SemaphoreType.DMA((3,))],
    )
    return pl.pallas_call(
        _moe_sparse_kernel,
        grid_spec=grid_spec,
        out_shape=jax.ShapeDtypeStruct((n_blocks, rows + SUBLANES, LANES), F32),
        compiler_params=_cparams(("parallel", "arbitrary")),
        name="moe_routed",
    )(grp, row0, eloc, src, dst, wts, x_slab, w_gu_bf, w_down_bf)


def _final_kernel(x1_ref, h2_ref, moe_ref, gf_ref, fg_ref, sgu_ref, sdn_ref, y_ref, cv_ref):
    moe = _slabs_to_rows(moe_ref, cv_ref, x1_ref.shape[1])
    f = _swiglu_bf(h2_ref[0], sgu_ref[...], sdn_ref[...]) + moe
    y_ref[0] = _rms(x1_ref[0] + gf_ref[0] * f, fg_ref[...])


def _final(x1, h2, moe_slab, tb, blk0, gate_f, final_g, sgu_bf, sdn_bf, tm):
    nb, s, d = x1.shape
    tok = pl.BlockSpec((1, tm, d), lambda b, i: (b, i, 0))
    const = lambda a: pl.BlockSpec(a.shape, lambda b, i: (0,) * a.ndim)
    return pl.pallas_call(
        _final_kernel,
        grid=(nb, s // tm),
        in_specs=[tok, tok, _slab_spec(s, tm, tb, blk0),
                  _mod_spec(gate_f, tm), const(final_g), const(sgu_bf), const(sdn_bf)],
        out_specs=tok,
        out_shape=jax.ShapeDtypeStruct((nb, s, d), F32),
        scratch_shapes=[pltpu.VMEM((SUBLANES * (tm + 1), LANES), F32)],
        compiler_params=_cparams(("parallel", "parallel")),
        name="shared_final",
    )(x1, h2, moe_slab, gate_f, final_g, sgu_bf, sdn_bf)


def _rope_tables(start_pos, s):
    half = HEAD_DIM // 2
    inv_freq = ROPE_THETA ** (-jnp.arange(half, dtype=F32) / half)
    ang = (start_pos + jnp.arange(s, dtype=jnp.int32)).astype(F32)[:, None] * inv_freq[None, :]
    cos, sin = jnp.cos(ang), jnp.sin(ang)
    reps = LANES // HEAD_DIM
    return (jnp.tile(jnp.concatenate([cos, cos], axis=1), (1, reps)),
            jnp.tile(jnp.concatenate([-sin, sin], axis=1), (1, reps)))


def _mixer(x, mods, attend, h0, conv0, start_pos, ts, wts, slab):
    shift_m, scale_m, gate_m, shift_f, scale_f, _ = mods
    s = x.shape[1]
    cos, sin = _rope_tables(start_pos, s)
    q, k, v, xl, gl = _inproj(x, shift_m, scale_m, cos, sin, wts["norm_mix_g"], wts["w_in"], ts)
    attn = attend(q, k, v)
    lru, h_last, cbuf = _lru(xl, gl, h0, conv0, wts["conv_w"], wts["conv_b"], wts["wa"],
                             wts["gate_a_b"], wts["wx"], wts["gate_x_b"], wts["lru_lambda"],
                             ts, start_pos)
    x1, h2, route, hs = _outproj(attn, lru, x, gate_m, shift_f, scale_f, wts["attn_out_g"],
                                 wts["lru_out_g"], wts["w_out"], wts["norm_ffn_g"],
                                 wts["router_w_t"], wts["router_bias"], ts, *slab)
    return x1, h2, route, hs, k, v, h_last, cbuf


MOE_BLOCK_TOKENS = 2048


def kernel(x_prompt, x_sample, c_prompt, c_sample, cache_k, cache_v, state_h, state_conv, page_table, ada_w, ada_b, norm_mix_g, w_in, conv_w, conv_b, gate_a_w, gate_a_b, gate_x_w, gate_x_b, lru_lambda, attn_out_g, lru_out_g, w_out, norm_ffn_g, router_w, router_bias, exp_w_gu, exp_w_down, shared_w_gu, shared_w_down, final_g):
    depth = ada_w.shape[0]
    assert depth == 1, "single-layer trunk"
    n_prm, seq, d = x_prompt.shape
    n_dec, dec_seq, _ = x_sample.shape
    n_heads = cache_k.shape[3]
    aw = n_heads * HEAD_DIM
    lw = state_h.shape[2]
    past_len = page_table.shape[1] * PAGE_SIZE
    l = 0
    row = lambda a: a.reshape(1, -1)
    wts = dict(
        norm_mix_g=row(norm_mix_g[l]), w_in=w_in[l].astype(BF16), conv_w=conv_w[l], conv_b=conv_b[l],
        wa=_block_diag(gate_a_w[l]).astype(BF16), gate_a_b=gate_a_b[l],
        wx=_block_diag(gate_x_w[l]).astype(BF16), gate_x_b=gate_x_b[l], lru_lambda=lru_lambda[l],
        attn_out_g=row(attn_out_g[l]), lru_out_g=row(lru_out_g[l]), w_out=w_out[l].astype(BF16),
        norm_ffn_g=row(norm_ffn_g[l]), router_w_t=router_w[l].T,
        router_bias=router_bias[l].reshape(-1, 1), final_g=row(final_g),
        shared_w_gu=shared_w_gu[l].astype(BF16), shared_w_down=shared_w_down[l].astype(BF16))

    n_c = n_prm + n_dec
    c_all = jnp.concatenate([c_prompt, c_sample, jnp.zeros((-n_c % SUBLANES, d), F32)], axis=0)
    mod = _ada_mod(c_all, ada_w[l], ada_b[l])
    mods_p = [m[:n_prm, None, :] for m in jnp.split(mod, 6, axis=-1)]
    mods_s = [jnp.repeat(m[n_prm:n_c], dec_seq, axis=0)[None] for m in jnp.split(mod, 6, axis=-1)]

    tb = MOE_BLOCK_TOKENS
    nt = n_dec * dec_seq
    assert seq % tb == 0 and nt <= tb
    blk_s = n_prm * (seq // tb)
    slab_shape = (blk_s + 1, tb * SUBLANES, LANES)
    x1p, h2p, route_p, slab, k1, v1, h1, c1 = _mixer(
        x_prompt, mods_p, _moba_prompt, jnp.zeros((n_prm, lw), F32),
        jnp.zeros((n_prm, CONV_W - 1, lw), F32), 0, 512, wts,
        (slab_shape, tb, 0, jnp.zeros(slab_shape, F32)))

    ck = jnp.transpose(cache_k[l], (0, 2, 3, 1)).reshape(cache_k.shape[1], aw, PAGE_SIZE)
    cv = jnp.transpose(cache_v[l], (0, 2, 3, 1)).reshape(cache_v.shape[1], aw, PAGE_SIZE)

    def attend_sample(q, k, v):
        r = lambda a: a.reshape(n_dec, dec_seq, aw)
        return _moba_sample(r(q), r(k), r(v), ck, cv, page_table).reshape(1, n_dec * dec_seq, aw)

    xs = x_sample.reshape(1, n_dec * dec_seq, d)
    shift_m, scale_m, gate_m, shift_f, scale_f, gate_f = mods_s
    cos, sin = _rope_tables(past_len, dec_seq)
    cos, sin = jnp.tile(cos, (n_dec, 1)), jnp.tile(sin, (n_dec, 1))
    nt = n_dec * dec_seq
    q, k2, v2, xl, gl = _inproj(xs, shift_m, scale_m, cos, sin, wts["norm_mix_g"], wts["w_in"], nt)
    attn = attend_sample(q, k2, v2)
    lru, h2s, c2 = _lru(xl.reshape(n_dec, dec_seq, lw), gl.reshape(n_dec, dec_seq, lw), state_h[l],
                        state_conv[l], wts["conv_w"], wts["conv_b"], wts["wa"], wts["gate_a_b"],
                        wts["wx"], wts["gate_x_b"], wts["lru_lambda"], dec_seq, past_len)
    x1s, h2s_, route_s, slab = _outproj(
        attn, lru.reshape(1, nt, lw), xs, gate_m, shift_f, scale_f, wts["attn_out_g"],
        wts["lru_out_g"], wts["w_out"], wts["norm_ffn_g"], wts["router_w_t"], wts["router_bias"],
        nt, slab_shape, tb, blk_s, slab)

    n_exp = exp_w_gu.shape[1]
    src, dst, gw, grp, row0, eloc, lp = _moe_lists(
        [route_p.reshape(-1, LANES), route_s.reshape(-1, LANES)], tb, n_exp)
    moe_slab = _moe_sparse(slab, src, dst, gw, grp, row0, eloc, lp,
                           exp_w_gu[l].astype(BF16), exp_w_down[l].astype(BF16))
    fin = lambda x1, h2, blk0, gf, tm: _final(x1, h2, moe_slab, tb, blk0, gf, wts["final_g"],
                                              wts["shared_w_gu"], wts["shared_w_down"], tm)
    yp = fin(x1p, h2p, 0, mods_p[5], 512)
    ys = fin(x1s, h2s_, blk_s, gate_f, nt)

    return (yp, ys.reshape(n_dec, dec_seq, d),
            k1.reshape(1, n_prm, seq, n_heads, HEAD_DIM), v1.reshape(1, n_prm, seq, n_heads, HEAD_DIM),
            h1.reshape(1, n_prm, lw), c1[None],
            k2.reshape(1, n_dec, dec_seq, n_heads, HEAD_DIM), v2.reshape(1, n_dec, dec_seq, n_heads, HEAD_DIM),
            h2s.reshape(1, n_dec, lw), c2[None])
```

```python
import functools
import math

import jax
import jax.numpy as jnp
from jax import lax
from jax.experimental import pallas as pl
from jax.experimental.pallas import tpu as pltpu

F32 = jnp.float32
BF16 = jnp.bfloat16

HEAD_DIM = 64
LRU_BLOCKS = 8
CONV_W = 4
LRU_C = 8.0
MOBA_BLOCK = 256
MOBA_TOPK = 3
ROPE_THETA = 10000.0
N_GROUPS = 8
TOPK_GROUPS = 4
TOP_K = 8
ROUTED_SCALE = 2.5
EPS = 1e-6
PAGE_SIZE = 128

LANES = 128
SUBLANES = 8
NEG = -1e30
VMEM_LIMIT = 56 * 1024 * 1024

NN = (((1,), (0,)), ((), ()))
NT = (((1,), (1,)), ((), ()))


def _cparams(sem):
    return pltpu.CompilerParams(dimension_semantics=sem, vmem_limit_bytes=VMEM_LIMIT)


def _split(x):
    hi = x.astype(BF16)
    lo = (x - hi.astype(F32)).astype(BF16)
    return hi, lo


def _dot3(a, b, dims):
    ah, al = _split(a)
    bh, bl = _split(b)
    d = lambda u, v: lax.dot_general(u, v, dims, preferred_element_type=F32)
    return d(ah, bh) + (d(ah, bl) + d(al, bh))


def _dot(a, b, dims=NN):
    return lax.dot_general(a, b, dims, preferred_element_type=F32)


def _rms(x, g):
    return x * lax.rsqrt(jnp.mean(x * x, axis=-1, keepdims=True) + EPS) * g


def _silu(x):
    return x * jax.nn.sigmoid(x)


def _gelu_tanh(x):
    c = math.sqrt(2.0 / math.pi)
    return 0.5 * x * (1.0 + jnp.tanh(c * (x + 0.044715 * (x * x * x))))


def _ada_kernel(c_ref, w_ref, b_ref, o_ref):
    o_ref[...] = _dot3(_silu(c_ref[...]), w_ref[...], NN) + b_ref[...]


def _ada_mod(c, ada_w, ada_b):
    r, d = c.shape
    n = ada_w.shape[1]
    tn = 1536
    return pl.pallas_call(
        _ada_kernel,
        grid=(n // tn,),
        in_specs=[pl.BlockSpec((r, d), lambda j: (0, 0)),
                  pl.BlockSpec((d, tn), lambda j: (0, j)),
                  pl.BlockSpec((1, tn), lambda j: (0, j))],
        out_specs=pl.BlockSpec((r, tn), lambda j: (0, j)),
        out_shape=jax.ShapeDtypeStruct((r, n), F32),
        compiler_params=_cparams(("parallel",)),
        name="ada_mod",
    )(c, ada_w, ada_b.reshape(1, n))


def _inproj_kernel(x_ref, shift_ref, scale_ref, cos_ref, sin_ref, g_ref, w_ref,
                   q_ref, k_ref, v_ref, xl_ref, gl_ref, *, aw):
    x = x_ref[0]
    h = _rms(x, g_ref[...]) * (1.0 + scale_ref[0]) + shift_ref[0]
    proj = _dot(h.astype(BF16), w_ref[...])
    cos = cos_ref[...]
    sin = sin_ref[...]
    lane = lax.broadcasted_iota(jnp.int32, cos.shape, 1)
    first = (lane % HEAD_DIM) < (HEAD_DIM // 2)

    def rope(t):
        outs = []
        for c in range(aw // LANES):
            tc = t[:, LANES * c:LANES * (c + 1)]
            partner = jnp.where(first, pltpu.roll(tc, LANES - HEAD_DIM // 2, 1),
                                pltpu.roll(tc, HEAD_DIM // 2, 1))
            outs.append(tc * cos + partner * sin)
        return jnp.concatenate(outs, axis=1)

    q_ref[0] = rope(proj[:, 0:aw])
    k_ref[0] = rope(proj[:, aw:2 * aw])
    v_ref[0] = proj[:, 2 * aw:3 * aw]
    lw = (proj.shape[1] - 3 * aw) // 2
    xl_ref[0] = proj[:, 3 * aw:3 * aw + lw]
    gl_ref[0] = proj[:, 3 * aw + lw:]


def _mod_spec(mod, ts):
    if mod.shape[1] == 1:
        return pl.BlockSpec((1, 1, mod.shape[2]), lambda b, i: (b, 0, 0))
    return pl.BlockSpec((1, ts, mod.shape[2]), lambda b, i: (b, i, 0))


def _inproj(x, shift, scale, cos, sin, g, w_bf, ts):
    nb, s, d = x.shape
    aw = 512
    lw = (w_bf.shape[1] - 3 * aw) // 2
    tok = lambda width: pl.BlockSpec((1, ts, width), lambda b, i: (b, i, 0))
    shp = lambda width: jax.ShapeDtypeStruct((nb, s, width), F32)
    return pl.pallas_call(
        functools.partial(_inproj_kernel, aw=aw),
        grid=(nb, s // ts),
        in_specs=[tok(d), _mod_spec(shift, ts), _mod_spec(scale, ts),
                  pl.BlockSpec((ts, LANES), lambda b, i: (i, 0)),
                  pl.BlockSpec((ts, LANES), lambda b, i: (i, 0)),
                  pl.BlockSpec((1, d), lambda b, i: (0, 0)),
                  pl.BlockSpec(w_bf.shape, lambda b, i: (0, 0))],
        out_specs=[tok(aw), tok(aw), tok(aw), tok(lw), tok(lw)],
        out_shape=[shp(aw), shp(aw), shp(aw), shp(lw), shp(lw)],
        compiler_params=_cparams(("parallel", "parallel")),
        name="inproj",
    )(x, shift, scale, cos, sin, g, w_bf)


def _moba_prompt_kernel(q_ref, k_ref, v_ref, o_ref, kbf_ref, vst_ref, km_ref, selb_ref,
                        sa_ref, sb_ref):
    s_len = k_ref.shape[1]
    tq = MOBA_BLOCK
    nblk = s_len // MOBA_BLOCK
    half = lax.broadcasted_iota(jnp.int32, (LANES, tq), 0) < HEAD_DIM
    kbf_ref[...] = k_ref[0].astype(BF16)
    km_ref[...] = jnp.zeros(km_ref.shape, F32)
    for n in range(nblk):
        rows = slice(n * MOBA_BLOCK, (n + 1) * MOBA_BLOCK)
        km_ref[n:n + 1, :] = jnp.mean(k_ref[0, rows, :], axis=0, keepdims=True)
        vt = v_ref[0, rows, :].T
        vst_ref[n] = jnp.concatenate([jnp.where(half, vt, 0.0), jnp.where(half, 0.0, vt)],
                                     axis=1).astype(BF16)
    nrow = km_ref.shape[0]
    n_iota = lax.broadcasted_iota(jnp.int32, (nrow, 2 * tq), 0)
    keyi = lax.broadcasted_iota(jnp.int32, (MOBA_BLOCK, 2 * tq), 0)
    qi = lax.broadcasted_iota(jnp.int32, (MOBA_BLOCK, 2 * tq), 1) % tq

    def stack_heads(pb):
        return jnp.concatenate([pb[:, :tq], pb[:, tq:]], axis=0)

    def per_head(rowvec):
        return jnp.where(half, rowvec[:, :tq], rowvec[:, tq:])

    def qtile(t, carry):
        r0 = pl.multiple_of(t * tq, tq)
        qt = q_ref[0, pl.ds(r0, tq), :].T
        qt2 = jnp.concatenate([jnp.where(half, qt, 0.0), jnp.where(half, 0.0, qt)], axis=1)
        gm = jnp.where(n_iota < t, _dot3(km_ref[...], qt2, NN), -jnp.inf)
        rank = jnp.zeros(gm.shape, jnp.int32)
        for m_ in range(nblk):
            row = gm[m_:m_ + 1, :]
            beats = (row > gm) | ((row == gm) & (m_ < n_iota))
            rank = rank + beats.astype(jnp.int32)
        selb_ref[...] = jnp.where((n_iota < t) & (rank < MOBA_TOPK), 0.0, NEG)
        qb = (qt2 * (HEAD_DIM ** -0.5)).astype(BF16)
        st = jnp.where(keyi <= qi, _dot(kbf_ref[pl.ds(r0, MOBA_BLOCK), :], qb), NEG)
        m = jnp.max(st, axis=0, keepdims=True)
        p = jnp.exp(st - m)
        l = jnp.sum(p, axis=0, keepdims=True)
        acc = _dot(vst_ref[t], stack_heads(p.astype(BF16)))

        def scores(j, dst):
            j = jnp.minimum(j, nblk - 1)
            j0 = pl.multiple_of(j * MOBA_BLOCK, MOBA_BLOCK)
            dst[...] = _dot(kbf_ref[pl.ds(j0, MOBA_BLOCK), :], qb) + selb_ref[pl.ds(j, 1), :]

        def attend(src, j, m, l, acc):
            st = src[...]
            m_new = jnp.maximum(m, jnp.max(st, axis=0, keepdims=True))
            alpha = jnp.exp(m - m_new)
            p = jnp.exp(st - m_new)
            l = alpha * l + jnp.sum(p, axis=0, keepdims=True)
            acc = per_head(alpha) * acc + _dot(vst_ref[j], stack_heads(p.astype(BF16)))
            return m_new, l, acc

        scores(0, sa_ref)

        def body(i, c):
            scores(2 * i + 1, sb_ref)
            c = attend(sa_ref, 2 * i, *c)
            scores(2 * i + 2, sa_ref)
            return attend(sb_ref, jnp.minimum(2 * i + 1, nblk - 1), *c)

        m, l, acc = lax.fori_loop(0, (t + 1) // 2, body, (m, l, acc))
        o_ref[0, pl.ds(r0, tq), :] = (acc / per_head(l)).T
        return carry

    lax.fori_loop(0, s_len // tq, qtile, 0)


def _moba_prompt(q, k, v):
    nb, s, aw = q.shape
    nblk = s // MOBA_BLOCK
    nrow = -(-nblk // 16) * 16
    spec = pl.BlockSpec((1, s, LANES), lambda b, h: (b, 0, h))
    return pl.pallas_call(
        _moba_prompt_kernel,
        grid=(nb, aw // LANES),
        in_specs=[spec, spec, spec],
        out_specs=spec,
        out_shape=jax.ShapeDtypeStruct((nb, s, aw), F32),
        scratch_shapes=[pltpu.VMEM((s, LANES), BF16),
                        pltpu.VMEM((nblk, LANES, 2 * MOBA_BLOCK), BF16),
                        pltpu.VMEM((nrow, LANES), F32),
                        pltpu.VMEM((nrow, 2 * MOBA_BLOCK), F32),
                        pltpu.VMEM((MOBA_BLOCK, 2 * MOBA_BLOCK), F32),
                        pltpu.VMEM((MOBA_BLOCK, 2 * MOBA_BLOCK), F32)],
        compiler_params=_cparams(("parallel", "parallel")),
        name="moba_prompt",
    )(q, k, v)


def _moba_sample_kernel(pt_ref, q_ref, kn_ref, vn_ref, *rest, n_heads, n_q, bps):
    ppb = MOBA_BLOCK // PAGE_SIZE
    npg = bps * ppb
    k_refs, v_refs = rest[:npg], rest[npg:2 * npg]
    o_ref, qbd_ref, km_ref, m_ref, l_ref, o_scr = rest[2 * npg:]
    step = pl.program_id(1)
    nstep = pl.num_programs(1)
    rows = n_heads * n_q
    aw = q_ref.shape[2]
    rowi = lax.broadcasted_iota(jnp.int32, (rows, aw), 0)
    coli = lax.broadcasted_iota(jnp.int32, (rows, aw), 1)
    diag = (rowi // n_q) == (coli // HEAD_DIM)
    lane = lax.broadcasted_iota(jnp.int32, (rows, LANES), 1)

    @pl.when(step == 0)
    def _():
        qt = jnp.concatenate([q_ref[0]] * n_heads, axis=0)
        qbd_ref[...] = jnp.where(diag, qt, 0.0)
        km_ref[...] = jnp.zeros(km_ref.shape, F32)
        m_ref[...] = jnp.full(m_ref.shape, NEG, F32)
        l_ref[...] = jnp.zeros(l_ref.shape, F32)

    qbd = qbd_ref[...]
    qb = (qbd * (HEAD_DIM ** -0.5)).astype(BF16)
    klane = lax.broadcasted_iota(jnp.int32, km_ref.shape, 1)
    km, mm_, ll_ = km_ref[...], m_ref[...], l_ref[...]
    blocks = range(bps)
    kts = [jnp.concatenate([r[0] for r in k_refs[c * ppb:(c + 1) * ppb]], axis=1) for c in blocks]
    ss = [_dot(qb, kt.astype(BF16)) for kt in kts]
    ms = [jnp.max(s, axis=1, keepdims=True) for s in ss]
    ps = [jnp.exp(s - mj) for s, mj in zip(ss, ms)]
    ls = [jnp.sum(p, axis=1, keepdims=True) for p in ps]
    for c in blocks:
        j = step * bps + c
        vt = jnp.concatenate([r[0] for r in v_refs[c * ppb:(c + 1) * ppb]], axis=1)
        o_scr[j] = _dot(ps[c].astype(BF16), vt.astype(BF16), NT)
        km = jnp.where(klane == j, jnp.mean(kts[c], axis=1, keepdims=True), km)
        mm_ = jnp.where(lane == j, ms[c], mm_)
        ll_ = jnp.where(lane == j, ls[c], ll_)
    km_ref[...], m_ref[...], l_ref[...] = km, mm_, ll_

    @pl.when(step == nstep - 1)
    def _():
        npast = o_scr.shape[0]
        gate = _dot3(qbd, km_ref[...], NN)
        valid = lane < npast
        gm = jnp.where(valid, gate, -jnp.inf)
        rank = jnp.zeros((rows, LANES), jnp.int32)
        for mm in range(npast):
            colv = gm[:, mm:mm + 1]
            beats = (colv > gm) | ((colv == gm) & (mm < lane))
            rank = rank + beats.astype(jnp.int32)
        sel = valid & (rank < MOBA_TOPK)
        so = _dot(qb, kn_ref[0].astype(BF16), NT)
        qi = lax.broadcasted_iota(jnp.int32, (rows, n_q), 0) % n_q
        ki = lax.broadcasted_iota(jnp.int32, (rows, n_q), 1)
        so = jnp.where(ki <= qi, so, NEG)
        mo = jnp.max(so, axis=1, keepdims=True)
        po = jnp.exp(so - mo)
        lo = jnp.sum(po, axis=1, keepdims=True)
        oo = _dot(po.astype(BF16), vn_ref[0].astype(BF16))
        mall = jnp.where(sel, m_ref[...], NEG)
        mx = jnp.maximum(jnp.max(mall, axis=1, keepdims=True), mo)
        w = jnp.where(sel, jnp.exp(mall - mx), 0.0)
        wo = jnp.exp(mo - mx)
        den = jnp.sum(w * l_ref[...], axis=1, keepdims=True) + wo * lo
        num = wo * oo
        for jj in range(npast):
            num = num + w[:, jj:jj + 1] * o_scr[jj]
        res = jnp.where(diag, num / den, 0.0)
        out = res[0:n_q, :]
        for hh in range(1, n_heads):
            out = out + res[hh * n_q:(hh + 1) * n_q, :]
        o_ref[0] = out


def _moba_sample(q, k_new, v_new, cache_kt, cache_vt, page_table):
    nb, n_q, aw = q.shape
    n_heads = aw // HEAD_DIM
    n_pages = page_table.shape[1]
    ppb = MOBA_BLOCK // PAGE_SIZE
    npast = n_pages // ppb
    bps = next(n for n in (4, 2, 1) if npast % n == 0)
    npg = bps * ppb
    rows = n_heads * n_q
    tok = pl.BlockSpec((1, n_q, aw), lambda b, j, pt: (b, 0, 0))
    page = lambda o: pl.BlockSpec((1, aw, PAGE_SIZE), lambda b, j, pt: (pt[b, npg * j + o], 0, 0))
    pages = [page(o) for o in range(npg)]
    grid_spec = pltpu.PrefetchScalarGridSpec(
        num_scalar_prefetch=1,
        grid=(nb, npast // bps),
        in_specs=[tok, tok, tok] + pages + pages,
        out_specs=tok,
        scratch_shapes=[pltpu.VMEM((rows, aw), F32), pltpu.VMEM((aw, LANES), F32),
                        pltpu.VMEM((rows, LANES), F32), pltpu.VMEM((rows, LANES), F32),
                        pltpu.VMEM((npast, rows, aw), F32)],
    )
    return pl.pallas_call(
        functools.partial(_moba_sample_kernel, n_heads=n_heads, n_q=n_q, bps=bps),
        grid_spec=grid_spec,
        out_shape=jax.ShapeDtypeStruct((nb, n_q, aw), F32),
        compiler_params=_cparams(("parallel", "arbitrary")),
        name="moba_sample",
    )(page_table, q, k_new, v_new, *([cache_kt] * npg), *([cache_vt] * npg))


def _lru_kernel(xl_ref, gl_ref, h0_ref, c0_ref, cw_ref, cb_ref, wa_ref, ba_ref, wx_ref, bx_ref,
                lam_ref, y_ref, hlast_ref, cbuf_ref, xbuf_ref, a_ref, b_ref, hcar_ref,
                *, ts, start_pos):
    i = pl.program_id(1)
    w = xl_ref.shape[2]
    hist = SUBLANES

    @pl.when(i == 0)
    def _():
        xbuf_ref[0:hist, :] = jnp.zeros((hist, w), F32)
        xbuf_ref[hist - (CONV_W - 1):hist, :] = c0_ref[0]
        hcar_ref[...] = jnp.broadcast_to(h0_ref[0], (SUBLANES, w))

    x = xl_ref[0]
    xbuf_ref[hist:hist + ts, :] = x
    conv = cb_ref[...] + cw_ref[0:1, :] * xbuf_ref[hist - 3:hist - 3 + ts, :]
    conv = conv + cw_ref[1:2, :] * xbuf_ref[hist - 2:hist - 2 + ts, :]
    conv = conv + cw_ref[2:3, :] * xbuf_ref[hist - 1:hist - 1 + ts, :]
    conv = conv + cw_ref[3:4, :] * x
    cbuf_ref[0] = xbuf_ref[hist + ts - (CONV_W - 1):hist + ts, :]
    xbuf_ref[0:hist, :] = xbuf_ref[ts:ts + hist, :]

    cbf = conv.astype(BF16)
    r = jax.nn.sigmoid(_dot(cbf, wa_ref[...]) + ba_ref[...])
    ig = jax.nn.sigmoid(_dot(cbf, wx_ref[...]) + bx_ref[...])
    nl = -lam_ref[...]
    softplus = jnp.maximum(nl, 0.0) + jnp.log1p(jnp.exp(-jnp.abs(nl)))
    log_a = (-LRU_C * r) * softplus
    a = jnp.exp(log_a)
    mult = jnp.sqrt(1.0 - jnp.exp(2.0 * log_a))
    if start_pos == 0:
        reset = (i * ts + lax.broadcasted_iota(jnp.int32, (ts, w), 0)) == 0
        a = jnp.where(reset, 0.0, a)
        mult = jnp.where(reset, 1.0, mult)
    a_ref[...] = a
    b_ref[...] = mult * ig * conv

    rowi = lax.broadcasted_iota(jnp.int32, (SUBLANES, w), 0)

    def group(gi, h):
        o = pl.multiple_of(gi * SUBLANES, SUBLANES)
        a8 = a_ref[pl.ds(o, SUBLANES), :]
        b8 = b_ref[pl.ds(o, SUBLANES), :]
        for d in (1, 2, 4):
            keep = rowi >= d
            a_sh = jnp.where(keep, pltpu.roll(a8, d, 0), 1.0)
            b_sh = jnp.where(keep, pltpu.roll(b8, d, 0), 0.0)
            b8 = a8 * b_sh + b8
            a8 = a8 * a_sh
        h8 = a8 * h + b8
        b_ref[pl.ds(o, SUBLANES), :] = h8
        return jnp.broadcast_to(h8[SUBLANES - 1:SUBLANES, :], (SUBLANES, w))

    h = lax.fori_loop(0, ts // SUBLANES, group, hcar_ref[...])
    hcar_ref[...] = h
    y_ref[0] = b_ref[...] * _gelu_tanh(gl_ref[0])
    hlast_ref[0] = h[0:1, :]


def _block_diag(wb):
    nb, d, _ = wb.shape
    eye = jnp.eye(nb, dtype=wb.dtype)
    return (wb[:, :, None, :] * eye[:, None, :, None]).reshape(nb * d, nb * d)


def _lru(xl, gl, h0, conv0, conv_w, conv_b, wa_bd, ba, wx_bd, bx, lam, ts, start_pos):
    nb, s, w = xl.shape
    tok = pl.BlockSpec((1, ts, w), lambda b, i: (b, i, 0))
    row = pl.BlockSpec((1, w), lambda b, i: (0, 0))
    mat = pl.BlockSpec((w, w), lambda b, i: (0, 0))
    return pl.pallas_call(
        functools.partial(_lru_kernel, ts=ts, start_pos=start_pos),
        grid=(nb, s // ts),
        in_specs=[tok, tok,
                  pl.BlockSpec((1, 1, w), lambda b, i: (b, 0, 0)),
                  pl.BlockSpec((1, CONV_W - 1, w), lambda b, i: (b, 0, 0)),
                  pl.BlockSpec((CONV_W, w), lambda b, i: (0, 0)),
                  row, mat, row, mat, row, row],
        out_specs=[tok,
                   pl.BlockSpec((1, 1, w), lambda b, i: (b, 0, 0)),
                   pl.BlockSpec((1, CONV_W - 1, w), lambda b, i: (b, 0, 0))],
        out_shape=[jax.ShapeDtypeStruct((nb, s, w), F32),
                   jax.ShapeDtypeStruct((nb, 1, w), F32),
                   jax.ShapeDtypeStruct((nb, CONV_W - 1, w), F32)],
        scratch_shapes=[pltpu.VMEM((ts + SUBLANES, w), F32), pltpu.VMEM((ts, w), F32),
                        pltpu.VMEM((ts, w), F32), pltpu.VMEM((SUBLANES, w), F32)],
        compiler_params=_cparams(("parallel", "arbitrary")),
        name="rg_lru",
    )(xl, gl, h0.reshape(nb, 1, w), conv0, conv_w, conv_b.reshape(1, w), wa_bd,
      ba.reshape(1, w), wx_bd, bx.reshape(1, w), lam.reshape(1, w))


def _route(logits_t, bias_col, n_exp, ts):
    scores = jax.nn.sigmoid(logits_t)
    biased = scores + bias_col
    gsz = n_exp // N_GROUPS
    gscore = []
    for g in range(N_GROUPS):
        blk = biased[g * gsz:(g + 1) * gsz, :]
        m1 = jnp.max(blk, axis=0, keepdims=True)
        cnt = jnp.sum((blk == m1).astype(F32), axis=0, keepdims=True)
        m2 = jnp.max(jnp.where(blk < m1, blk, -jnp.inf), axis=0, keepdims=True)
        gscore.append(m1 + jnp.where(cnt >= 2.0, m1, m2))
    cands = []
    for g in range(N_GROUPS):
        rank = jnp.zeros((1, ts), jnp.int32)
        for g2 in range(N_GROUPS):
            if g2 == g:
                continue
            beats = (gscore[g2] > gscore[g]) if g2 > g else (gscore[g2] >= gscore[g])
            rank = rank + beats.astype(jnp.int32)
        blk = biased[g * gsz:(g + 1) * gsz, :]
        cands.append(jnp.where(rank < TOPK_GROUPS, blk, -jnp.inf))
    cand = jnp.concatenate(cands, axis=0)
    e_f = lax.broadcasted_iota(jnp.int32, (n_exp, ts), 0).astype(F32)
    ids, sel = [], []
    for _ in range(TOP_K):
        top = jnp.max(cand, axis=0, keepdims=True)
        idx = jnp.min(jnp.where(cand == top, e_f, float(n_exp)), axis=0, keepdims=True)
        hit = e_f == idx
        ids.append(idx)
        sel.append(jnp.sum(jnp.where(hit, scores, 0.0), axis=0, keepdims=True))
        cand = jnp.where(hit, -jnp.inf, cand)
    total = functools.reduce(lambda a, c: a + c, sel)
    wts = [ROUTED_SCALE * w / total for w in sel]
    return jnp.concatenate(ids + wts, axis=0)


def _rows_to_slabs(x, cv_ref, slab_ref):
    n = x.shape[0]
    s = n + 1
    for j in range(SUBLANES):
        cv_ref[pl.ds(j * s, n), :] = x[:, j * LANES:(j + 1) * LANES]
    for r in range(n):
        slab_ref[pl.ds(SUBLANES * r, SUBLANES), :] = cv_ref[pl.ds(r, SUBLANES, stride=s), :]


def _slabs_to_rows(slab_ref, cv_ref, n):
    s = n + 1
    for r in range(n):
        cv_ref[pl.ds(r, SUBLANES, stride=s), :] = slab_ref[pl.ds(SUBLANES * r, SUBLANES), :]
    return jnp.concatenate([cv_ref[pl.ds(j * s, n), :] for j in range(SUBLANES)], axis=1)


def _slab_spec(s, tile, tb, blk0):
    per_seq = -(-s // tb)
    tiles = min(tb, s) // tile
    return pl.BlockSpec((None, tile * SUBLANES, LANES),
                        lambda b, i: (blk0 + b * per_seq + i // tiles, i % tiles, 0))


def _outproj_kernel(attn_ref, lru_ref, x_ref, gm_ref, sf_ref, cf_ref, ag_ref, lg_ref, wo_ref,
                    ng_ref, rw_ref, rb_ref, *rest, ts):
    x1_ref, h2_ref, gates_ref, hs_ref, cv_ref = rest[-5:]
    aw = attn_ref.shape[2]
    an = _rms(attn_ref[0], ag_ref[...]).astype(BF16)
    ln = _rms(lru_ref[0], lg_ref[...]).astype(BF16)
    mixed = _dot(an, wo_ref[0:aw, :]) + _dot(ln, wo_ref[aw:, :])
    x1 = x_ref[0] + gm_ref[0] * mixed
    x1_ref[0] = x1
    h2 = _rms(x1, ng_ref[...]) * (1.0 + cf_ref[0]) + sf_ref[0]
    h2_ref[0] = h2.astype(BF16)
    _rows_to_slabs(h2, cv_ref, hs_ref)
    n_exp = rw_ref.shape[0]
    route_t = _route(_dot3(rw_ref[...], h2, NT), rb_ref[...], n_exp, ts)
    route_t = jnp.concatenate([route_t, jnp.zeros((LANES - 2 * TOP_K, ts), F32)], axis=0)
    gates_ref[0] = route_t.T


def _outproj(attn, lru, x, gate_m, shift_f, scale_f, ag, lg, wo_bf, ng, rw_t, rb_col, ts,
             slab_shape, tb, blk0, slab_in):
    nb, s, d = x.shape
    aw = attn.shape[2]
    tok = lambda width: pl.BlockSpec((1, ts, width), lambda b, i: (b, i, 0))
    const = lambda a: pl.BlockSpec(a.shape, lambda b, i: (0,) * a.ndim)
    args = [attn, lru, x, gate_m, shift_f, scale_f, ag, lg, wo_bf, ng, rw_t, rb_col]
    in_specs = [tok(aw), tok(lru.shape[2]), tok(d), _mod_spec(gate_m, ts), _mod_spec(shift_f, ts),
                _mod_spec(scale_f, ts), const(ag), const(lg), const(wo_bf), const(ng),
                const(rw_t), const(rb_col)]
    aliases = {}
    if slab_in is not None:
        aliases = {len(args): 3}
        args.append(slab_in)
        in_specs.append(pl.BlockSpec(memory_space=pl.ANY))
    return pl.pallas_call(
        functools.partial(_outproj_kernel, ts=ts),
        grid=(nb, s // ts),
        in_specs=in_specs,
        out_specs=[tok(d), tok(d), tok(LANES), _slab_spec(s, ts, tb, blk0)],
        out_shape=[jax.ShapeDtypeStruct((nb, s, d), F32), jax.ShapeDtypeStruct((nb, s, d), BF16),
                   jax.ShapeDtypeStruct((nb, s, LANES), F32),
                   jax.ShapeDtypeStruct(slab_shape, F32)],
        scratch_shapes=[pltpu.VMEM((SUBLANES * (ts + 1), LANES), F32)],
        input_output_aliases=aliases,
        compiler_params=_cparams(("parallel", "parallel")),
        name="outproj_router",
    )(*args)


def _swiglu_bf(h, w_gu, w_down):
    gu = _dot(h, w_gu)
    de = gu.shape[1] // 2
    act = _silu(gu[:, :de]) * gu[:, de:]
    return _dot(act.astype(BF16), w_down)


MOE_TILE = 128
MOE_STRIDE = MOE_TILE + 1
MOE_BATCH = 16
MOE_GROUP = 2
MOE_EXPERTS_PER_STEP = 4
MOE_KEY_SHIFT = 12
SMEM_CHUNK = 1024


def _moe_lists(routes, tb, n_exp):
    assert tb < (1 << MOE_KEY_SHIFT)
    ids, wts = [], []
    for r in routes:
        fill = -r.shape[0] % tb
        i = jnp.pad(r[:, 0:TOP_K].astype(jnp.int32), ((0, fill), (0, 0)), constant_values=n_exp)
        ids.append(i.reshape(-1, tb, TOP_K))
        wts.append(jnp.pad(r[:, TOP_K:2 * TOP_K], ((0, fill), (0, 0))).reshape(-1, tb * TOP_K))
    ids, wts = jnp.concatenate(ids, axis=0), jnp.concatenate(wts, axis=0)
    n_blocks = ids.shape[0]
    tloc = jnp.arange(tb, dtype=jnp.int32)[None, :, None]
    keys = ((ids << MOE_KEY_SHIFT) | tloc).reshape(n_blocks, tb * TOP_K)
    keys, wts = lax.sort((keys, wts), dimension=1, num_keys=1)
    n = keys.shape[1]
    lp = -(-(n + MOE_TILE) // SMEM_CHUNK) * SMEM_CHUNK
    flat = lambda a: jnp.pad(a, ((0, 0), (0, lp - n))).reshape(-1)
    tok = flat((keys & ((1 << MOE_KEY_SHIFT) - 1)) * SUBLANES)
    cnt = jnp.sum((ids[..., None] == jnp.arange(n_exp, dtype=jnp.int32)).astype(jnp.int32),
                  axis=(1, 2))
    start = jnp.cumsum(cnt, axis=1) - cnt
    ntile = (cnt + MOE_TILE - 1) // MOE_TILE
    tend = jnp.cumsum(ntile, axis=1)
    tfirst = tend - ntile
    nt_max = (tb * TOP_K + n_exp * (MOE_TILE - 1)) // MOE_TILE
    ti = jnp.arange(nt_max, dtype=jnp.int32)[None, :]
    e_i = jnp.minimum(jnp.sum((ti[:, :, None] >= tend[:, None, :]).astype(jnp.int32), axis=2),
                      n_exp - 1)
    done = (ti - jnp.take_along_axis(tfirst, e_i, axis=1)) * MOE_TILE
    row0 = jnp.take_along_axis(start, e_i, axis=1) + done
    nvalid = jnp.clip(jnp.take_along_axis(cnt, e_i, axis=1) - done, 0, MOE_TILE)
    grp = jnp.concatenate([tfirst[:, ::MOE_EXPERTS_PER_STEP], tend[:, -1:]], axis=1)
    i32 = lambda a: a.astype(jnp.int32)
    return (tok, flat(wts), i32(grp), i32(row0), i32(e_i % MOE_EXPERTS_PER_STEP), i32(nvalid), lp)


def _moe_sparse_kernel(grp_ref, row0_ref, eloc_ref, nval_ref, tok_hbm, wts_hbm, x_ref, wgu_ref,
                       wdn_ref, out_ref, tok_smem, wts_smem, xt_ref, yt_ref, sem):
    b, e = pl.program_id(0), pl.program_id(1)
    lp = tok_smem.shape[0]
    m, s = MOE_TILE, MOE_STRIDE
    nch = SUBLANES

    @pl.when(e == 0)
    def _():
        off = pl.multiple_of(b * lp, SMEM_CHUNK)
        copies = [pltpu.make_async_copy(h.at[pl.ds(off, lp)], d, sem.at[k]) for k, (h, d) in
                  enumerate(((tok_hbm, tok_smem), (wts_hbm, wts_smem)))]
        for cp in copies:
            cp.start()
        out_ref[...] = jnp.zeros(out_ref.shape, F32)
        for cp in copies:
            cp.wait()

    def gather(r0, xt):
        for mi in range(m):
            t8 = pl.multiple_of(tok_smem[r0 + mi], SUBLANES)
            xt[pl.ds(mi, SUBLANES, stride=s), :] = x_ref[pl.ds(t8, SUBLANES), :]

    def expert(xt, yt, el, nv):
        x = jnp.concatenate([xt[pl.ds(j * s, m), :] for j in range(nch)], axis=1)
        y = _swiglu_bf(x.astype(BF16), wgu_ref[el], wdn_ref[el])
        y = jnp.where(lax.broadcasted_iota(jnp.int32, (m, 1), 0) < nv, y, 0.0)
        for j in range(nch):
            yt[pl.ds(j * s, m), :] = y[:, j * LANES:(j + 1) * LANES]

    def combine(r0, yt):
        for g0 in range(0, m, MOE_BATCH):
            upd = []
            for mi in range(g0, g0 + MOE_BATCH):
                t8 = pl.multiple_of(tok_smem[r0 + mi], SUBLANES)
                slab = yt[pl.ds(mi, SUBLANES, stride=s), :]
                upd.append((t8, out_ref[pl.ds(t8, SUBLANES), :] + wts_smem[r0 + mi] * slab))
            for t8, v in reversed(upd):
                out_ref[pl.ds(t8, SUBLANES), :] = v

    def tiles(t, n):
        r0s = [row0_ref[b, t + c] for c in range(n)]
        for c in range(n):
            gather(r0s[c], xt_ref.at[c])
        for c in range(n):
            expert(xt_ref.at[c], yt_ref.at[c], eloc_ref[b, t + c], nval_ref[b, t + c])
        for c in range(n):
            combine(r0s[c], yt_ref.at[c])

    t0 = grp_ref[b, e]
    nt = grp_ref[b, e + 1] - t0

    def group(i, carry):
        tiles(t0 + i * MOE_GROUP, MOE_GROUP)
        return carry

    lax.fori_loop(0, nt // MOE_GROUP, group, 0)
    for rem in range(1, MOE_GROUP):
        @pl.when(nt % MOE_GROUP == rem)
        def _():
            tiles(t0 + nt - rem, rem)


def _moe_sparse(x_slab, tok, wts, grp, row0, eloc, nval, lp, w_gu_bf, w_down_bf):
    n_blocks, rows, _ = x_slab.shape
    n_exp = w_gu_bf.shape[0]
    eps = MOE_EXPERTS_PER_STEP
    any_spec = pl.BlockSpec(memory_space=pl.ANY)
    blk = pl.BlockSpec((None, rows, LANES), lambda b, e, *_: (b, 0, 0))
    grid_spec = pltpu.PrefetchScalarGridSpec(
        num_scalar_prefetch=4,
        grid=(n_blocks, n_exp // eps),
        in_specs=[any_spec, any_spec, blk,
                  pl.BlockSpec((eps,) + w_gu_bf.shape[1:], lambda b, e, *_: (e, 0, 0)),
                  pl.BlockSpec((eps,) + w_down_bf.shape[1:], lambda b, e, *_: (e, 0, 0))],
        out_specs=blk,
        scratch_shapes=[pltpu.SMEM((lp,), jnp.int32), pltpu.SMEM((lp,), F32),
                        pltpu.VMEM((MOE_GROUP, SUBLANES * MOE_STRIDE, LANES), F32),
                        pltpu.VMEM((MOE_GROUP, SUBLANES * MOE_STRIDE, LANES), F32),
                        pltpu.SemaphoreType.DMA((2,))],
    )
    return pl.pallas_call(
        _moe_sparse_kernel,
        grid_spec=grid_spec,
        out_shape=jax.ShapeDtypeStruct(x_slab.shape, F32),
        compiler_params=_cparams(("parallel", "arbitrary")),
        name="moe_routed",
    )(grp, row0, eloc, nval, tok, wts, x_slab, w_gu_bf, w_down_bf)


def _final_kernel(x1_ref, h2_ref, moe_ref, gf_ref, fg_ref, sgu_ref, sdn_ref, y_ref, cv_ref):
    moe = _slabs_to_rows(moe_ref, cv_ref, x1_ref.shape[1])
    f = _swiglu_bf(h2_ref[0], sgu_ref[...], sdn_ref[...]) + moe
    y_ref[0] = _rms(x1_ref[0] + gf_ref[0] * f, fg_ref[...])


def _final(x1, h2, moe_slab, tb, blk0, gate_f, final_g, sgu_bf, sdn_bf, tm):
    nb, s, d = x1.shape
    tok = pl.BlockSpec((1, tm, d), lambda b, i: (b, i, 0))
    const = lambda a: pl.BlockSpec(a.shape, lambda b, i: (0,) * a.ndim)
    return pl.pallas_call(
        _final_kernel,
        grid=(nb, s // tm),
        in_specs=[tok, tok, _slab_spec(s, tm, tb, blk0),
                  _mod_spec(gate_f, tm), const(final_g), const(sgu_bf), const(sdn_bf)],
        out_specs=tok,
        out_shape=jax.ShapeDtypeStruct((nb, s, d), F32),
        scratch_shapes=[pltpu.VMEM((SUBLANES * (tm + 1), LANES), F32)],
        compiler_params=_cparams(("parallel", "parallel")),
        name="shared_final",
    )(x1, h2, moe_slab, gate_f, final_g, sgu_bf, sdn_bf)


def _rope_tables(start_pos, s):
    half = HEAD_DIM // 2
    inv_freq = ROPE_THETA ** (-jnp.arange(half, dtype=F32) / half)
    ang = (start_pos + jnp.arange(s, dtype=jnp.int32)).astype(F32)[:, None] * inv_freq[None, :]
    cos, sin = jnp.cos(ang), jnp.sin(ang)
    reps = LANES // HEAD_DIM
    return (jnp.tile(jnp.concatenate([cos, cos], axis=1), (1, reps)),
            jnp.tile(jnp.concatenate([-sin, sin], axis=1), (1, reps)))


def _mixer(x, mods, attend, h0, conv0, start_pos, ts, wts, slab):
    shift_m, scale_m, gate_m, shift_f, scale_f, _ = mods
    s = x.shape[1]
    cos, sin = _rope_tables(start_pos, s)
    q, k, v, xl, gl = _inproj(x, shift_m, scale_m, cos, sin, wts["norm_mix_g"], wts["w_in"], ts)
    attn = attend(q, k, v)
    lru, h_last, cbuf = _lru(xl, gl, h0, conv0, wts["conv_w"], wts["conv_b"], wts["wa"],
                             wts["gate_a_b"], wts["wx"], wts["gate_x_b"], wts["lru_lambda"],
                             ts, start_pos)
    x1, h2, route, hs = _outproj(attn, lru, x, gate_m, shift_f, scale_f, wts["attn_out_g"],
                                 wts["lru_out_g"], wts["w_out"], wts["norm_ffn_g"],
                                 wts["router_w_t"], wts["router_bias"], ts, *slab)
    return x1, h2, route, hs, k, v, h_last, cbuf


MOE_BLOCK_TOKENS = 2048


def kernel(x_prompt, x_sample, c_prompt, c_sample, cache_k, cache_v, state_h, state_conv, page_table, ada_w, ada_b, norm_mix_g, w_in, conv_w, conv_b, gate_a_w, gate_a_b, gate_x_w, gate_x_b, lru_lambda, attn_out_g, lru_out_g, w_out, norm_ffn_g, router_w, router_bias, exp_w_gu, exp_w_down, shared_w_gu, shared_w_down, final_g):
    depth = ada_w.shape[0]
    assert depth == 1, "single-layer trunk"
    n_prm, seq, d = x_prompt.shape
    n_dec, dec_seq, _ = x_sample.shape
    n_heads = cache_k.shape[3]
    aw = n_heads * HEAD_DIM
    lw = state_h.shape[2]
    past_len = page_table.shape[1] * PAGE_SIZE
    l = 0
    row = lambda a: a.reshape(1, -1)
    wts = dict(
        norm_mix_g=row(norm_mix_g[l]), w_in=w_in[l].astype(BF16), conv_w=conv_w[l], conv_b=conv_b[l],
        wa=_block_diag(gate_a_w[l]).astype(BF16), gate_a_b=gate_a_b[l],
        wx=_block_diag(gate_x_w[l]).astype(BF16), gate_x_b=gate_x_b[l], lru_lambda=lru_lambda[l],
        attn_out_g=row(attn_out_g[l]), lru_out_g=row(lru_out_g[l]), w_out=w_out[l].astype(BF16),
        norm_ffn_g=row(norm_ffn_g[l]), router_w_t=router_w[l].T,
        router_bias=router_bias[l].reshape(-1, 1), final_g=row(final_g),
        shared_w_gu=shared_w_gu[l].astype(BF16), shared_w_down=shared_w_down[l].astype(BF16))

    n_c = n_prm + n_dec
    c_all = jnp.concatenate([c_prompt, c_sample, jnp.zeros((-n_c % SUBLANES, d), F32)], axis=0)
    mod = _ada_mod(c_all, ada_w[l], ada_b[l])
    mods_p = [m[:n_prm, None, :] for m in jnp.split(mod, 6, axis=-1)]
    mods_s = [jnp.repeat(m[n_prm:n_c], dec_seq, axis=0)[None] for m in jnp.split(mod, 6, axis=-1)]

    tb = MOE_BLOCK_TOKENS
    nt = n_dec * dec_seq
    assert seq % tb == 0 and nt <= tb
    blk_s = n_prm * (seq // tb)
    slab_shape = (blk_s + 1, tb * SUBLANES, LANES)
    x1p, h2p, route_p, slab, k1, v1, h1, c1 = _mixer(
        x_prompt, mods_p, _moba_prompt, jnp.zeros((n_prm, lw), F32),
        jnp.zeros((n_prm, CONV_W - 1, lw), F32), 0, 512, wts,
        (slab_shape, tb, 0, jnp.zeros(slab_shape, F32)))

    ck = jnp.transpose(cache_k[l], (0, 2, 3, 1)).reshape(cache_k.shape[1], aw, PAGE_SIZE)
    cv = jnp.transpose(cache_v[l], (0, 2, 3, 1)).reshape(cache_v.shape[1], aw, PAGE_SIZE)

    def attend_sample(q, k, v):
        r = lambda a: a.reshape(n_dec, dec_seq, aw)
        return _moba_sample(r(q), r(k), r(v), ck, cv, page_table).reshape(1, n_dec * dec_seq, aw)

    xs = x_sample.reshape(1, n_dec * dec_seq, d)
    shift_m, scale_m, gate_m, shift_f, scale_f, gate_f = mods_s
    cos, sin = _rope_tables(past_len, dec_seq)
    cos, sin = jnp.tile(cos, (n_dec, 1)), jnp.tile(sin, (n_dec, 1))
    nt = n_dec * dec_seq
    q, k2, v2, xl, gl = _inproj(xs, shift_m, scale_m, cos, sin, wts["norm_mix_g"], wts["w_in"], nt)
    attn = attend_sample(q, k2, v2)
    lru, h2s, c2 = _lru(xl.reshape(n_dec, dec_seq, lw), gl.reshape(n_dec, dec_seq, lw), state_h[l],
                        state_conv[l], wts["conv_w"], wts["conv_b"], wts["wa"], wts["gate_a_b"],
                        wts["wx"], wts["gate_x_b"], wts["lru_lambda"], dec_seq, past_len)
    x1s, h2s_, route_s, slab = _outproj(
        attn, lru.reshape(1, nt, lw), xs, gate_m, shift_f, scale_f, wts["attn_out_g"],
        wts["lru_out_g"], wts["w_out"], wts["norm_ffn_g"], wts["router_w_t"], wts["router_bias"],
        nt, slab_shape, tb, blk_s, slab)

    n_exp = exp_w_gu.shape[1]
    tok, gw, grp, row0, eloc, nval, lp = _moe_lists(
        [route_p.reshape(-1, LANES), route_s.reshape(-1, LANES)], tb, n_exp)
    moe_slab = _moe_sparse(slab, tok, gw, grp, row0, eloc, nval, lp,
                           exp_w_gu[l].astype(BF16), exp_w_down[l].astype(BF16))
    fin = lambda x1, h2, blk0, gf, tm: _final(x1, h2, moe_slab, tb, blk0, gf, wts["final_g"],
                                              wts["shared_w_gu"], wts["shared_w_down"], tm)
    yp = fin(x1p, h2p, 0, mods_p[5], 512)
    ys = fin(x1s, h2s_, blk_s, gate_f, nt)

    return (yp, ys.reshape(n_dec, dec_seq, d),
            k1.reshape(1, n_prm, seq, n_heads, HEAD_DIM), v1.reshape(1, n_prm, seq, n_heads, HEAD_DIM),
            h1.reshape(1, n_prm, lw), c1[None],
            k2.reshape(1, n_dec, dec_seq, n_heads, HEAD_DIM), v2.reshape(1, n_dec, dec_seq, n_heads, HEAD_DIM),
            h2s.reshape(1, n_dec, lw), c2[None])
```

```python
import functools
import math

import jax
import jax.numpy as jnp
from jax import lax
from jax.experimental import pallas as pl
from jax.experimental.pallas import tpu as pltpu

F32 = jnp.float32
BF16 = jnp.bfloat16

HEAD_DIM = 64
LRU_BLOCKS = 8
CONV_W = 4
LRU_C = 8.0
MOBA_BLOCK = 256
MOBA_TOPK = 3
ROPE_THETA = 10000.0
N_GROUPS = 8
TOPK_GROUPS = 4
TOP_K = 8
ROUTED_SCALE = 2.5
EPS = 1e-6
PAGE_SIZE = 128

LANES = 128
SUBLANES = 8
NEG = -1e30
VMEM_LIMIT = 56 * 1024 * 1024

NN = (((1,), (0,)), ((), ()))
NT = (((1,), (1,)), ((), ()))


def _cparams(sem):
    return pltpu.CompilerParams(dimension_semantics=sem, vmem_limit_bytes=VMEM_LIMIT)


def _split(x):
    hi = x.astype(BF16)
    lo = (x - hi.astype(F32)).astype(BF16)
    return hi, lo


def _dot3(a, b, dims):
    ah, al = _split(a)
    bh, bl = _split(b)
    d = lambda u, v: lax.dot_general(u, v, dims, preferred_element_type=F32)
    return d(ah, bh) + (d(ah, bl) + d(al, bh))


def _dot(a, b, dims=NN):
    return lax.dot_general(a, b, dims, preferred_element_type=F32)


def _rms(x, g):
    return x * lax.rsqrt(jnp.mean(x * x, axis=-1, keepdims=True) + EPS) * g


def _silu(x):
    return x * jax.nn.sigmoid(x)


def _gelu_tanh(x):
    c = math.sqrt(2.0 / math.pi)
    return 0.5 * x * (1.0 + jnp.tanh(c * (x + 0.044715 * (x * x * x))))


def _ada_kernel(c_ref, w_ref, b_ref, o_ref):
    o_ref[...] = _dot3(_silu(c_ref[...]), w_ref[...], NN) + b_ref[...]


def _ada_mod(c, ada_w, ada_b):
    r, d = c.shape
    n = ada_w.shape[1]
    tn = 1536
    return pl.pallas_call(
        _ada_kernel,
        grid=(n // tn,),
        in_specs=[pl.BlockSpec((r, d), lambda j: (0, 0)),
                  pl.BlockSpec((d, tn), lambda j: (0, j)),
                  pl.BlockSpec((1, tn), lambda j: (0, j))],
        out_specs=pl.BlockSpec((r, tn), lambda j: (0, j)),
        out_shape=jax.ShapeDtypeStruct((r, n), F32),
        compiler_params=_cparams(("parallel",)),
        name="ada_mod",
    )(c, ada_w, ada_b.reshape(1, n))


def _inproj_kernel(x_ref, shift_ref, scale_ref, cos_ref, sin_ref, g_ref, w_ref,
                   q_ref, k_ref, v_ref, xl_ref, gl_ref, *, aw):
    x = x_ref[0]
    h = _rms(x, g_ref[...]) * (1.0 + scale_ref[0]) + shift_ref[0]
    proj = _dot(h.astype(BF16), w_ref[...])
    cos = cos_ref[...]
    sin = sin_ref[...]
    lane = lax.broadcasted_iota(jnp.int32, cos.shape, 1)
    first = (lane % HEAD_DIM) < (HEAD_DIM // 2)

    def rope(t):
        outs = []
        for c in range(aw // LANES):
            tc = t[:, LANES * c:LANES * (c + 1)]
            partner = jnp.where(first, pltpu.roll(tc, LANES - HEAD_DIM // 2, 1),
                                pltpu.roll(tc, HEAD_DIM // 2, 1))
            outs.append(tc * cos + partner * sin)
        return jnp.concatenate(outs, axis=1)

    q_ref[0] = rope(proj[:, 0:aw])
    k_ref[0] = rope(proj[:, aw:2 * aw])
    v_ref[0] = proj[:, 2 * aw:3 * aw]
    lw = (proj.shape[1] - 3 * aw) // 2
    xl_ref[0] = proj[:, 3 * aw:3 * aw + lw]
    gl_ref[0] = proj[:, 3 * aw + lw:]


def _mod_spec(mod, ts):
    if mod.shape[1] == 1:
        return pl.BlockSpec((1, 1, mod.shape[2]), lambda b, i: (b, 0, 0))
    return pl.BlockSpec((1, ts, mod.shape[2]), lambda b, i: (b, i, 0))


def _inproj(x, shift, scale, cos, sin, g, w_bf, ts):
    nb, s, d = x.shape
    aw = 512
    lw = (w_bf.shape[1] - 3 * aw) // 2
    tok = lambda width: pl.BlockSpec((1, ts, width), lambda b, i: (b, i, 0))
    shp = lambda width: jax.ShapeDtypeStruct((nb, s, width), F32)
    return pl.pallas_call(
        functools.partial(_inproj_kernel, aw=aw),
        grid=(nb, s // ts),
        in_specs=[tok(d), _mod_spec(shift, ts), _mod_spec(scale, ts),
                  pl.BlockSpec((ts, LANES), lambda b, i: (i, 0)),
                  pl.BlockSpec((ts, LANES), lambda b, i: (i, 0)),
                  pl.BlockSpec((1, d), lambda b, i: (0, 0)),
                  pl.BlockSpec(w_bf.shape, lambda b, i: (0, 0))],
        out_specs=[tok(aw), tok(aw), tok(aw), tok(lw), tok(lw)],
        out_shape=[shp(aw), shp(aw), shp(aw), shp(lw), shp(lw)],
        compiler_params=_cparams(("parallel", "parallel")),
        name="inproj",
    )(x, shift, scale, cos, sin, g, w_bf)


def _moba_prompt_kernel(q_ref, k_ref, v_ref, o_ref, kbf_ref, vst_ref, km_ref, selb_ref,
                        sa_ref, sb_ref):
    s_len = k_ref.shape[1]
    tq = MOBA_BLOCK
    nblk = s_len // MOBA_BLOCK
    half = lax.broadcasted_iota(jnp.int32, (LANES, tq), 0) < HEAD_DIM
    kbf_ref[...] = k_ref[0].astype(BF16)
    km_ref[...] = jnp.zeros(km_ref.shape, F32)
    for n in range(nblk):
        rows = slice(n * MOBA_BLOCK, (n + 1) * MOBA_BLOCK)
        km_ref[n:n + 1, :] = jnp.mean(k_ref[0, rows, :], axis=0, keepdims=True)
        vt = v_ref[0, rows, :].T
        vst_ref[n] = jnp.concatenate([jnp.where(half, vt, 0.0), jnp.where(half, 0.0, vt)],
                                     axis=1).astype(BF16)
    nrow = km_ref.shape[0]
    n_iota = lax.broadcasted_iota(jnp.int32, (nrow, 2 * tq), 0)
    keyi = lax.broadcasted_iota(jnp.int32, (MOBA_BLOCK, 2 * tq), 0)
    qi = lax.broadcasted_iota(jnp.int32, (MOBA_BLOCK, 2 * tq), 1) % tq

    def stack_heads(pb):
        return jnp.concatenate([pb[:, :tq], pb[:, tq:]], axis=0)

    def per_head(rowvec):
        return jnp.where(half, rowvec[:, :tq], rowvec[:, tq:])

    def qtile(t, carry):
        r0 = pl.multiple_of(t * tq, tq)
        qt = q_ref[0, pl.ds(r0, tq), :].T
        qt2 = jnp.concatenate([jnp.where(half, qt, 0.0), jnp.where(half, 0.0, qt)], axis=1)
        gm = jnp.where(n_iota < t, _dot3(km_ref[...], qt2, NN), -jnp.inf)
        rank = jnp.zeros(gm.shape, jnp.int32)
        for m_ in range(nblk):
            row = gm[m_:m_ + 1, :]
            beats = (row > gm) | ((row == gm) & (m_ < n_iota))
            rank = rank + beats.astype(jnp.int32)
        selb_ref[...] = jnp.where((n_iota < t) & (rank < MOBA_TOPK), 0.0, NEG)
        qb = (qt2 * (HEAD_DIM ** -0.5)).astype(BF16)
        st = jnp.where(keyi <= qi, _dot(kbf_ref[pl.ds(r0, MOBA_BLOCK), :], qb), NEG)
        m = jnp.max(st, axis=0, keepdims=True)
        p = jnp.exp(st - m)
        l = jnp.sum(p, axis=0, keepdims=True)
        acc = _dot(vst_ref[t], stack_heads(p.astype(BF16)))

        def scores(j, dst):
            j = jnp.minimum(j, nblk - 1)
            j0 = pl.multiple_of(j * MOBA_BLOCK, MOBA_BLOCK)
            dst[...] = _dot(kbf_ref[pl.ds(j0, MOBA_BLOCK), :], qb) + selb_ref[pl.ds(j, 1), :]

        def attend(src, j, m, l, acc):
            st = src[...]
            m_new = jnp.maximum(m, jnp.max(st, axis=0, keepdims=True))
            alpha = jnp.exp(m - m_new)
            p = jnp.exp(st - m_new)
            l = alpha * l + jnp.sum(p, axis=0, keepdims=True)
            acc = per_head(alpha) * acc + _dot(vst_ref[j], stack_heads(p.astype(BF16)))
            return m_new, l, acc

        scores(0, sa_ref)

        def body(i, c):
            scores(2 * i + 1, sb_ref)
            c = attend(sa_ref, 2 * i, *c)
            scores(2 * i + 2, sa_ref)
            return attend(sb_ref, jnp.minimum(2 * i + 1, nblk - 1), *c)

        m, l, acc = lax.fori_loop(0, (t + 1) // 2, body, (m, l, acc))
        o_ref[0, pl.ds(r0, tq), :] = (acc / per_head(l)).T
        return carry

    lax.fori_loop(0, s_len // tq, qtile, 0)


def _moba_prompt(q, k, v):
    nb, s, aw = q.shape
    nblk = s // MOBA_BLOCK
    nrow = -(-nblk // 16) * 16
    spec = pl.BlockSpec((1, s, LANES), lambda b, h: (b, 0, h))
    return pl.pallas_call(
        _moba_prompt_kernel,
        grid=(nb, aw // LANES),
        in_specs=[spec, spec, spec],
        out_specs=spec,
        out_shape=jax.ShapeDtypeStruct((nb, s, aw), F32),
        scratch_shapes=[pltpu.VMEM((s, LANES), BF16),
                        pltpu.VMEM((nblk, LANES, 2 * MOBA_BLOCK), BF16),
                        pltpu.VMEM((nrow, LANES), F32),
                        pltpu.VMEM((nrow, 2 * MOBA_BLOCK), F32),
                        pltpu.VMEM((MOBA_BLOCK, 2 * MOBA_BLOCK), F32),
                        pltpu.VMEM((MOBA_BLOCK, 2 * MOBA_BLOCK), F32)],
        compiler_params=_cparams(("parallel", "parallel")),
        name="moba_prompt",
    )(q, k, v)


def _moba_sample_kernel(pt_ref, q_ref, kn_ref, vn_ref, *rest, n_heads, n_q, bps):
    ppb = MOBA_BLOCK // PAGE_SIZE
    npg = bps * ppb
    k_refs, v_refs = rest[:npg], rest[npg:2 * npg]
    o_ref, qbd_ref, km_ref, m_ref, l_ref, o_scr = rest[2 * npg:]
    step = pl.program_id(1)
    nstep = pl.num_programs(1)
    rows = n_heads * n_q
    aw = q_ref.shape[2]
    rowi = lax.broadcasted_iota(jnp.int32, (rows, aw), 0)
    coli = lax.broadcasted_iota(jnp.int32, (rows, aw), 1)
    diag = (rowi // n_q) == (coli // HEAD_DIM)
    lane = lax.broadcasted_iota(jnp.int32, (rows, LANES), 1)

    @pl.when(step == 0)
    def _():
        qt = jnp.concatenate([q_ref[0]] * n_heads, axis=0)
        qbd_ref[...] = jnp.where(diag, qt, 0.0)
        km_ref[...] = jnp.zeros(km_ref.shape, F32)
        m_ref[...] = jnp.full(m_ref.shape, NEG, F32)
        l_ref[...] = jnp.zeros(l_ref.shape, F32)

    qbd = qbd_ref[...]
    qb = (qbd * (HEAD_DIM ** -0.5)).astype(BF16)
    klane = lax.broadcasted_iota(jnp.int32, km_ref.shape, 1)
    km, mm_, ll_ = km_ref[...], m_ref[...], l_ref[...]
    blocks = range(bps)
    kts = [jnp.concatenate([r[0] for r in k_refs[c * ppb:(c + 1) * ppb]], axis=1) for c in blocks]
    ss = [_dot(qb, kt.astype(BF16)) for kt in kts]
    ms = [jnp.max(s, axis=1, keepdims=True) for s in ss]
    ps = [jnp.exp(s - mj) for s, mj in zip(ss, ms)]
    ls = [jnp.sum(p, axis=1, keepdims=True) for p in ps]
    for c in blocks:
        j = step * bps + c
        vt = jnp.concatenate([r[0] for r in v_refs[c * ppb:(c + 1) * ppb]], axis=1)
        o_scr[j] = _dot(ps[c].astype(BF16), vt.astype(BF16), NT)
        km = jnp.where(klane == j, jnp.mean(kts[c], axis=1, keepdims=True), km)
        mm_ = jnp.where(lane == j, ms[c], mm_)
        ll_ = jnp.where(lane == j, ls[c], ll_)
    km_ref[...], m_ref[...], l_ref[...] = km, mm_, ll_

    @pl.when(step == nstep - 1)
    def _():
        npast = o_scr.shape[0]
        gate = _dot3(qbd, km_ref[...], NN)
        valid = lane < npast
        gm = jnp.where(valid, gate, -jnp.inf)
        rank = jnp.zeros((rows, LANES), jnp.int32)
        for mm in range(npast):
            colv = gm[:, mm:mm + 1]
            beats = (colv > gm) | ((colv == gm) & (mm < lane))
            rank = rank + beats.astype(jnp.int32)
        sel = valid & (rank < MOBA_TOPK)
        so = _dot(qb, kn_ref[0].astype(BF16), NT)
        qi = lax.broadcasted_iota(jnp.int32, (rows, n_q), 0) % n_q
        ki = lax.broadcasted_iota(jnp.int32, (rows, n_q), 1)
        so = jnp.where(ki <= qi, so, NEG)
        mo = jnp.max(so, axis=1, keepdims=True)
        po = jnp.exp(so - mo)
        lo = jnp.sum(po, axis=1, keepdims=True)
        oo = _dot(po.astype(BF16), vn_ref[0].astype(BF16))
        mall = jnp.where(sel, m_ref[...], NEG)
        mx = jnp.maximum(jnp.max(mall, axis=1, keepdims=True), mo)
        w = jnp.where(sel, jnp.exp(mall - mx), 0.0)
        wo = jnp.exp(mo - mx)
        den = jnp.sum(w * l_ref[...], axis=1, keepdims=True) + wo * lo
        num = wo * oo
        for jj in range(npast):
            num = num + w[:, jj:jj + 1] * o_scr[jj]
        res = jnp.where(diag, num / den, 0.0)
        out = res[0:n_q, :]
        for hh in range(1, n_heads):
            out = out + res[hh * n_q:(hh + 1) * n_q, :]
        o_ref[0] = out


def _moba_sample(q, k_new, v_new, cache_kt, cache_vt, page_table):
    nb, n_q, aw = q.shape
    n_heads = aw // HEAD_DIM
    n_pages = page_table.shape[1]
    ppb = MOBA_BLOCK // PAGE_SIZE
    npast = n_pages // ppb
    bps = next(n for n in (8, 4, 2, 1) if npast % n == 0)
    npg = bps * ppb
    rows = n_heads * n_q
    tok = pl.BlockSpec((1, n_q, aw), lambda b, j, pt: (b, 0, 0))
    page = lambda o: pl.BlockSpec((1, aw, PAGE_SIZE), lambda b, j, pt: (pt[b, npg * j + o], 0, 0))
    pages = [page(o) for o in range(npg)]
    grid_spec = pltpu.PrefetchScalarGridSpec(
        num_scalar_prefetch=1,
        grid=(nb, npast // bps),
        in_specs=[tok, tok, tok] + pages + pages,
        out_specs=tok,
        scratch_shapes=[pltpu.VMEM((rows, aw), F32), pltpu.VMEM((aw, LANES), F32),
                        pltpu.VMEM((rows, LANES), F32), pltpu.VMEM((rows, LANES), F32),
                        pltpu.VMEM((npast, rows, aw), F32)],
    )
    return pl.pallas_call(
        functools.partial(_moba_sample_kernel, n_heads=n_heads, n_q=n_q, bps=bps),
        grid_spec=grid_spec,
        out_shape=jax.ShapeDtypeStruct((nb, n_q, aw), F32),
        compiler_params=_cparams(("parallel", "arbitrary")),
        name="moba_sample",
    )(page_table, q, k_new, v_new, *([cache_kt] * npg), *([cache_vt] * npg))


def _lru_kernel(xl_ref, gl_ref, h0_ref, c0_ref, cw_ref, cb_ref, wa_ref, ba_ref, wx_ref, bx_ref,
                lam_ref, y_ref, hlast_ref, cbuf_ref, xbuf_ref, a_ref, b_ref, hcar_ref,
                *, ts, start_pos):
    i = pl.program_id(1)
    w = xl_ref.shape[2]
    hist = SUBLANES

    @pl.when(i == 0)
    def _():
        xbuf_ref[0:hist, :] = jnp.zeros((hist, w), F32)
        xbuf_ref[hist - (CONV_W - 1):hist, :] = c0_ref[0]
        hcar_ref[...] = jnp.broadcast_to(h0_ref[0], (SUBLANES, w))

    x = xl_ref[0]
    xbuf_ref[hist:hist + ts, :] = x
    conv = cb_ref[...] + cw_ref[0:1, :] * xbuf_ref[hist - 3:hist - 3 + ts, :]
    conv = conv + cw_ref[1:2, :] * xbuf_ref[hist - 2:hist - 2 + ts, :]
    conv = conv + cw_ref[2:3, :] * xbuf_ref[hist - 1:hist - 1 + ts, :]
    conv = conv + cw_ref[3:4, :] * x
    cbuf_ref[0] = xbuf_ref[hist + ts - (CONV_W - 1):hist + ts, :]
    xbuf_ref[0:hist, :] = xbuf_ref[ts:ts + hist, :]

    cbf = conv.astype(BF16)
    r = jax.nn.sigmoid(_dot(cbf, wa_ref[...]) + ba_ref[...])
    ig = jax.nn.sigmoid(_dot(cbf, wx_ref[...]) + bx_ref[...])
    nl = -lam_ref[...]
    softplus = jnp.maximum(nl, 0.0) + jnp.log1p(jnp.exp(-jnp.abs(nl)))
    log_a = (-LRU_C * r) * softplus
    a = jnp.exp(log_a)
    mult = jnp.sqrt(1.0 - jnp.exp(2.0 * log_a))
    if start_pos == 0:
        reset = (i * ts + lax.broadcasted_iota(jnp.int32, (ts, w), 0)) == 0
        a = jnp.where(reset, 0.0, a)
        mult = jnp.where(reset, 1.0, mult)
    a_ref[...] = a
    b_ref[...] = mult * ig * conv

    rowi = lax.broadcasted_iota(jnp.int32, (SUBLANES, w), 0)

    def group(gi, h):
        o = pl.multiple_of(gi * SUBLANES, SUBLANES)
        a8 = a_ref[pl.ds(o, SUBLANES), :]
        b8 = b_ref[pl.ds(o, SUBLANES), :]
        for d in (1, 2, 4):
            keep = rowi >= d
            a_sh = jnp.where(keep, pltpu.roll(a8, d, 0), 1.0)
            b_sh = jnp.where(keep, pltpu.roll(b8, d, 0), 0.0)
            b8 = a8 * b_sh + b8
            a8 = a8 * a_sh
        h8 = a8 * h + b8
        b_ref[pl.ds(o, SUBLANES), :] = h8
        return jnp.broadcast_to(h8[SUBLANES - 1:SUBLANES, :], (SUBLANES, w))

    h = lax.fori_loop(0, ts // SUBLANES, group, hcar_ref[...])
    hcar_ref[...] = h
    y_ref[0] = b_ref[...] * _gelu_tanh(gl_ref[0])
    hlast_ref[0] = h[0:1, :]


def _block_diag(wb):
    nb, d, _ = wb.shape
    eye = jnp.eye(nb, dtype=wb.dtype)
    return (wb[:, :, None, :] * eye[:, None, :, None]).reshape(nb * d, nb * d)


def _lru(xl, gl, h0, conv0, conv_w, conv_b, wa_bd, ba, wx_bd, bx, lam, ts, start_pos):
    nb, s, w = xl.shape
    tok = pl.BlockSpec((1, ts, w), lambda b, i: (b, i, 0))
    row = pl.BlockSpec((1, w), lambda b, i: (0, 0))
    mat = pl.BlockSpec((w, w), lambda b, i: (0, 0))
    return pl.pallas_call(
        functools.partial(_lru_kernel, ts=ts, start_pos=start_pos),
        grid=(nb, s // ts),
        in_specs=[tok, tok,
                  pl.BlockSpec((1, 1, w), lambda b, i: (b, 0, 0)),
                  pl.BlockSpec((1, CONV_W - 1, w), lambda b, i: (b, 0, 0)),
                  pl.BlockSpec((CONV_W, w), lambda b, i: (0, 0)),
                  row, mat, row, mat, row, row],
        out_specs=[tok,
                   pl.BlockSpec((1, 1, w), lambda b, i: (b, 0, 0)),
                   pl.BlockSpec((1, CONV_W - 1, w), lambda b, i: (b, 0, 0))],
        out_shape=[jax.ShapeDtypeStruct((nb, s, w), F32),
                   jax.ShapeDtypeStruct((nb, 1, w), F32),
                   jax.ShapeDtypeStruct((nb, CONV_W - 1, w), F32)],
        scratch_shapes=[pltpu.VMEM((ts + SUBLANES, w), F32), pltpu.VMEM((ts, w), F32),
                        pltpu.VMEM((ts, w), F32), pltpu.VMEM((SUBLANES, w), F32)],
        compiler_params=_cparams(("parallel", "arbitrary")),
        name="rg_lru",
    )(xl, gl, h0.reshape(nb, 1, w), conv0, conv_w, conv_b.reshape(1, w), wa_bd,
      ba.reshape(1, w), wx_bd, bx.reshape(1, w), lam.reshape(1, w))


def _route(logits_t, bias_col, n_exp, ts):
    scores = jax.nn.sigmoid(logits_t)
    biased = scores + bias_col
    gsz = n_exp // N_GROUPS
    gscore = []
    for g in range(N_GROUPS):
        blk = biased[g * gsz:(g + 1) * gsz, :]
        m1 = jnp.max(blk, axis=0, keepdims=True)
        cnt = jnp.sum((blk == m1).astype(F32), axis=0, keepdims=True)
        m2 = jnp.max(jnp.where(blk < m1, blk, -jnp.inf), axis=0, keepdims=True)
        gscore.append(m1 + jnp.where(cnt >= 2.0, m1, m2))
    cands = []
    for g in range(N_GROUPS):
        rank = jnp.zeros((1, ts), jnp.int32)
        for g2 in range(N_GROUPS):
            if g2 == g:
                continue
            beats = (gscore[g2] > gscore[g]) if g2 > g else (gscore[g2] >= gscore[g])
            rank = rank + beats.astype(jnp.int32)
        blk = biased[g * gsz:(g + 1) * gsz, :]
        cands.append(jnp.where(rank < TOPK_GROUPS, blk, -jnp.inf))
    cand = jnp.concatenate(cands, axis=0)
    e_f = lax.broadcasted_iota(jnp.int32, (n_exp, ts), 0).astype(F32)
    ids, sel = [], []
    for _ in range(TOP_K):
        top = jnp.max(cand, axis=0, keepdims=True)
        idx = jnp.min(jnp.where(cand == top, e_f, float(n_exp)), axis=0, keepdims=True)
        hit = e_f == idx
        ids.append(idx)
        sel.append(jnp.sum(jnp.where(hit, scores, 0.0), axis=0, keepdims=True))
        cand = jnp.where(hit, -jnp.inf, cand)
    total = functools.reduce(lambda a, c: a + c, sel)
    wts = [ROUTED_SCALE * w / total for w in sel]
    return jnp.concatenate(ids + wts, axis=0)


def _rows_to_slabs(x, cv_ref, slab_ref):
    n = x.shape[0]
    s = n + 1
    for j in range(SUBLANES):
        cv_ref[pl.ds(j * s, n), :] = x[:, j * LANES:(j + 1) * LANES]
    for r in range(n):
        slab_ref[pl.ds(SUBLANES * r, SUBLANES), :] = cv_ref[pl.ds(r, SUBLANES, stride=s), :]


def _slabs_to_rows(slab_ref, cv_ref, n):
    s = n + 1
    for r in range(n):
        cv_ref[pl.ds(r, SUBLANES, stride=s), :] = slab_ref[pl.ds(SUBLANES * r, SUBLANES), :]
    return jnp.concatenate([cv_ref[pl.ds(j * s, n), :] for j in range(SUBLANES)], axis=1)


def _slab_spec(s, tile, tb, blk0):
    per_seq = -(-s // tb)
    tiles = min(tb, s) // tile
    return pl.BlockSpec((None, tile * SUBLANES, LANES),
                        lambda b, i: (blk0 + b * per_seq + i // tiles, i % tiles, 0))


def _outproj_kernel(attn_ref, lru_ref, x_ref, gm_ref, sf_ref, cf_ref, ag_ref, lg_ref, wo_ref,
                    ng_ref, rw_ref, rb_ref, *rest, ts):
    x1_ref, h2_ref, gates_ref, hs_ref, cv_ref = rest[-5:]
    aw = attn_ref.shape[2]
    an = _rms(attn_ref[0], ag_ref[...]).astype(BF16)
    ln = _rms(lru_ref[0], lg_ref[...]).astype(BF16)
    mixed = _dot(an, wo_ref[0:aw, :]) + _dot(ln, wo_ref[aw:, :])
    x1 = x_ref[0] + gm_ref[0] * mixed
    x1_ref[0] = x1
    h2 = _rms(x1, ng_ref[...]) * (1.0 + cf_ref[0]) + sf_ref[0]
    h2_ref[0] = h2.astype(BF16)
    _rows_to_slabs(h2, cv_ref, hs_ref)
    n_exp = rw_ref.shape[0]
    route_t = _route(_dot3(rw_ref[...], h2, NT), rb_ref[...], n_exp, ts)
    route_t = jnp.concatenate([route_t, jnp.zeros((LANES - 2 * TOP_K, ts), F32)], axis=0)
    gates_ref[0] = route_t.T


def _outproj(attn, lru, x, gate_m, shift_f, scale_f, ag, lg, wo_bf, ng, rw_t, rb_col, ts,
             slab_shape, tb, blk0, slab_in):
    nb, s, d = x.shape
    aw = attn.shape[2]
    tok = lambda width: pl.BlockSpec((1, ts, width), lambda b, i: (b, i, 0))
    const = lambda a: pl.BlockSpec(a.shape, lambda b, i: (0,) * a.ndim)
    args = [attn, lru, x, gate_m, shift_f, scale_f, ag, lg, wo_bf, ng, rw_t, rb_col]
    in_specs = [tok(aw), tok(lru.shape[2]), tok(d), _mod_spec(gate_m, ts), _mod_spec(shift_f, ts),
                _mod_spec(scale_f, ts), const(ag), const(lg), const(wo_bf), const(ng),
                const(rw_t), const(rb_col)]
    aliases = {}
    if slab_in is not None:
        aliases = {len(args): 3}
        args.append(slab_in)
        in_specs.append(pl.BlockSpec(memory_space=pl.ANY))
    return pl.pallas_call(
        functools.partial(_outproj_kernel, ts=ts),
        grid=(nb, s // ts),
        in_specs=in_specs,
        out_specs=[tok(d), tok(d), tok(LANES), _slab_spec(s, ts, tb, blk0)],
        out_shape=[jax.ShapeDtypeStruct((nb, s, d), F32), jax.ShapeDtypeStruct((nb, s, d), BF16),
                   jax.ShapeDtypeStruct((nb, s, LANES), F32),
                   jax.ShapeDtypeStruct(slab_shape, F32)],
        scratch_shapes=[pltpu.VMEM((SUBLANES * (ts + 1), LANES), F32)],
        input_output_aliases=aliases,
        compiler_params=_cparams(("parallel", "parallel")),
        name="outproj_router",
    )(*args)


def _swiglu_bf(h, w_gu, w_down):
    gu = _dot(h, w_gu)
    de = gu.shape[1] // 2
    act = _silu(gu[:, :de]) * gu[:, de:]
    return _dot(act.astype(BF16), w_down)


MOE_TILE = 128
MOE_STRIDE = MOE_TILE + 1
MOE_BATCH = 8
MOE_GROUP = 2
MOE_EXPERTS_PER_STEP = 4
MOE_KEY_SHIFT = 12
SMEM_CHUNK = 1024


def _moe_lists(routes, tb, n_exp):
    assert tb < (1 << MOE_KEY_SHIFT)
    ids, wts = [], []
    for r in routes:
        fill = -r.shape[0] % tb
        i = jnp.pad(r[:, 0:TOP_K].astype(jnp.int32), ((0, fill), (0, 0)), constant_values=n_exp)
        ids.append(i.reshape(-1, tb, TOP_K))
        wts.append(jnp.pad(r[:, TOP_K:2 * TOP_K], ((0, fill), (0, 0))).reshape(-1, tb * TOP_K))
    ids, wts = jnp.concatenate(ids, axis=0), jnp.concatenate(wts, axis=0)
    n_blocks = ids.shape[0]
    tloc = jnp.arange(tb, dtype=jnp.int32)[None, :, None]
    keys = ((ids << MOE_KEY_SHIFT) | tloc).reshape(n_blocks, tb * TOP_K)
    keys, wts = lax.sort((keys, wts), dimension=1, num_keys=1)
    n = keys.shape[1]
    lp = -(-(n + MOE_TILE) // SMEM_CHUNK) * SMEM_CHUNK
    flat = lambda a: jnp.pad(a, ((0, 0), (0, lp - n))).reshape(-1)
    tok = flat((keys & ((1 << MOE_KEY_SHIFT) - 1)) * SUBLANES)
    cnt = jnp.sum((ids[..., None] == jnp.arange(n_exp, dtype=jnp.int32)).astype(jnp.int32),
                  axis=(1, 2))
    start = jnp.cumsum(cnt, axis=1) - cnt
    ntile = (cnt + MOE_TILE - 1) // MOE_TILE
    tend = jnp.cumsum(ntile, axis=1)
    tfirst = tend - ntile
    nt_max = (tb * TOP_K + n_exp * (MOE_TILE - 1)) // MOE_TILE
    ti = jnp.arange(nt_max, dtype=jnp.int32)[None, :]
    e_i = jnp.minimum(jnp.sum((ti[:, :, None] >= tend[:, None, :]).astype(jnp.int32), axis=2),
                      n_exp - 1)
    done = (ti - jnp.take_along_axis(tfirst, e_i, axis=1)) * MOE_TILE
    row0 = jnp.take_along_axis(start, e_i, axis=1) + done
    nvalid = jnp.clip(jnp.take_along_axis(cnt, e_i, axis=1) - done, 0, MOE_TILE)
    grp = jnp.concatenate([tfirst[:, ::MOE_EXPERTS_PER_STEP], tend[:, -1:]], axis=1)
    i32 = lambda a: a.astype(jnp.int32)
    return (tok, flat(wts), i32(grp), i32(row0), i32(e_i % MOE_EXPERTS_PER_STEP), i32(nvalid), lp)


def _moe_sparse_kernel(grp_ref, row0_ref, eloc_ref, nval_ref, tok_hbm, wts_hbm, x_ref, wgu_ref,
                       wdn_ref, out_ref, tok_smem, wts_smem, xt_ref, yt_ref, sem):
    b, e = pl.program_id(0), pl.program_id(1)
    lp = tok_smem.shape[0]
    m, s = MOE_TILE, MOE_STRIDE
    nch = SUBLANES

    @pl.when(e == 0)
    def _():
        off = pl.multiple_of(b * lp, SMEM_CHUNK)
        copies = [pltpu.make_async_copy(h.at[pl.ds(off, lp)], d, sem.at[k]) for k, (h, d) in
                  enumerate(((tok_hbm, tok_smem), (wts_hbm, wts_smem)))]
        for cp in copies:
            cp.start()
        out_ref[...] = jnp.zeros(out_ref.shape, F32)
        for cp in copies:
            cp.wait()

    def gather(r0, xt):
        for mi in range(m):
            t8 = pl.multiple_of(tok_smem[r0 + mi], SUBLANES)
            xt[pl.ds(mi, SUBLANES, stride=s), :] = x_ref[pl.ds(t8, SUBLANES), :]

    def expert(xt, yt, el, nv):
        x = jnp.concatenate([xt[pl.ds(j * s, m), :] for j in range(nch)], axis=1)
        y = _swiglu_bf(x.astype(BF16), wgu_ref[el], wdn_ref[el])
        y = jnp.where(lax.broadcasted_iota(jnp.int32, (m, 1), 0) < nv, y, 0.0)
        for j in range(nch):
            yt[pl.ds(j * s, m), :] = y[:, j * LANES:(j + 1) * LANES]

    def combine(r0, yt):
        for g0 in range(0, m, MOE_BATCH):
            upd = []
            for mi in range(g0, g0 + MOE_BATCH):
                t8 = pl.multiple_of(tok_smem[r0 + mi], SUBLANES)
                slab = yt[pl.ds(mi, SUBLANES, stride=s), :]
                upd.append((t8, out_ref[pl.ds(t8, SUBLANES), :] + wts_smem[r0 + mi] * slab))
            for t8, v in reversed(upd):
                out_ref[pl.ds(t8, SUBLANES), :] = v

    def tiles(t, n):
        r0s = [row0_ref[b, t + c] for c in range(n)]
        for c in range(n):
            gather(r0s[c], xt_ref.at[c])
        for c in range(n):
            expert(xt_ref.at[c], yt_ref.at[c], eloc_ref[b, t + c], nval_ref[b, t + c])
        for c in range(n):
            combine(r0s[c], yt_ref.at[c])

    t0 = grp_ref[b, e]
    nt = grp_ref[b, e + 1] - t0

    def group(i, carry):
        tiles(t0 + i * MOE_GROUP, MOE_GROUP)
        return carry

    lax.fori_loop(0, nt // MOE_GROUP, group, 0)
    for rem in range(1, MOE_GROUP):
        @pl.when(nt % MOE_GROUP == rem)
        def _():
            tiles(t0 + nt - rem, rem)


def _moe_sparse(x_slab, tok, wts, grp, row0, eloc, nval, lp, w_gu_bf, w_down_bf):
    n_blocks, rows, _ = x_slab.shape
    n_exp = w_gu_bf.shape[0]
    eps = MOE_EXPERTS_PER_STEP
    any_spec = pl.BlockSpec(memory_space=pl.ANY)
    blk = pl.BlockSpec((None, rows, LANES), lambda b, e, *_: (b, 0, 0))
    grid_spec = pltpu.PrefetchScalarGridSpec(
        num_scalar_prefetch=4,
        grid=(n_blocks, n_exp // eps),
        in_specs=[any_spec, any_spec, blk,
                  pl.BlockSpec((eps,) + w_gu_bf.shape[1:], lambda b, e, *_: (e, 0, 0)),
                  pl.BlockSpec((eps,) + w_down_bf.shape[1:], lambda b, e, *_: (e, 0, 0))],
        out_specs=blk,
        scratch_shapes=[pltpu.SMEM((lp,), jnp.int32), pltpu.SMEM((lp,), F32),
                        pltpu.VMEM((MOE_GROUP, SUBLANES * MOE_STRIDE, LANES), F32),
                        pltpu.VMEM((MOE_GROUP, SUBLANES * MOE_STRIDE, LANES), F32),
                        pltpu.SemaphoreType.DMA((2,))],
    )
    return pl.pallas_call(
        _moe_sparse_kernel,
        grid_spec=grid_spec,
        out_shape=jax.ShapeDtypeStruct(x_slab.shape, F32),
        compiler_params=_cparams(("parallel", "arbitrary")),
        name="moe_routed",
    )(grp, row0, eloc, nval, tok, wts, x_slab, w_gu_bf, w_down_bf)


def _final_kernel(x1_ref, h2_ref, moe_ref, gf_ref, fg_ref, sgu_ref, sdn_ref, y_ref, cv_ref):
    moe = _slabs_to_rows(moe_ref, cv_ref, x1_ref.shape[1])
    f = _swiglu_bf(h2_ref[0], sgu_ref[...], sdn_ref[...]) + moe
    y_ref[0] = _rms(x1_ref[0] + gf_ref[0] * f, fg_ref[...])


def _final(x1, h2, moe_slab, tb, blk0, gate_f, final_g, sgu_bf, sdn_bf, tm):
    nb, s, d = x1.shape
    tok = pl.BlockSpec((1, tm, d), lambda b, i: (b, i, 0))
    const = lambda a: pl.BlockSpec(a.shape, lambda b, i: (0,) * a.ndim)
    return pl.pallas_call(
        _final_kernel,
        grid=(nb, s // tm),
        in_specs=[tok, tok, _slab_spec(s, tm, tb, blk0),
                  _mod_spec(gate_f, tm), const(final_g), const(sgu_bf), const(sdn_bf)],
        out_specs=tok,
        out_shape=jax.ShapeDtypeStruct((nb, s, d), F32),
        scratch_shapes=[pltpu.VMEM((SUBLANES * (tm + 1), LANES), F32)],
        compiler_params=_cparams(("parallel", "parallel")),
        name="shared_final",
    )(x1, h2, moe_slab, gate_f, final_g, sgu_bf, sdn_bf)


def _rope_tables(start_pos, s):
    half = HEAD_DIM // 2
    inv_freq = ROPE_THETA ** (-jnp.arange(half, dtype=F32) / half)
    ang = (start_pos + jnp.arange(s, dtype=jnp.int32)).astype(F32)[:, None] * inv_freq[None, :]
    cos, sin = jnp.cos(ang), jnp.sin(ang)
    reps = LANES // HEAD_DIM
    return (jnp.tile(jnp.concatenate([cos, cos], axis=1), (1, reps)),
            jnp.tile(jnp.concatenate([-sin, sin], axis=1), (1, reps)))


def _mixer(x, mods, attend, h0, conv0, start_pos, ts, wts, slab):
    shift_m, scale_m, gate_m, shift_f, scale_f, _ = mods
    s = x.shape[1]
    cos, sin = _rope_tables(start_pos, s)
    q, k, v, xl, gl = _inproj(x, shift_m, scale_m, cos, sin, wts["norm_mix_g"], wts["w_in"], ts)
    attn = attend(q, k, v)
    lru, h_last, cbuf = _lru(xl, gl, h0, conv0, wts["conv_w"], wts["conv_b"], wts["wa"],
                             wts["gate_a_b"], wts["wx"], wts["gate_x_b"], wts["lru_lambda"],
                             ts, start_pos)
    x1, h2, route, hs = _outproj(attn, lru, x, gate_m, shift_f, scale_f, wts["attn_out_g"],
                                 wts["lru_out_g"], wts["w_out"], wts["norm_ffn_g"],
                                 wts["router_w_t"], wts["router_bias"], ts, *slab)
    return x1, h2, route, hs, k, v, h_last, cbuf


MOE_BLOCK_TOKENS = 2048


def kernel(x_prompt, x_sample, c_prompt, c_sample, cache_k, cache_v, state_h, state_conv, page_table, ada_w, ada_b, norm_mix_g, w_in, conv_w, conv_b, gate_a_w, gate_a_b, gate_x_w, gate_x_b, lru_lambda, attn_out_g, lru_out_g, w_out, norm_ffn_g, router_w, router_bias, exp_w_gu, exp_w_down, shared_w_gu, shared_w_down, final_g):
    depth = ada_w.shape[0]
    assert depth == 1, "single-layer trunk"
    n_prm, seq, d = x_prompt.shape
    n_dec, dec_seq, _ = x_sample.shape
    n_heads = cache_k.shape[3]
    aw = n_heads * HEAD_DIM
    lw = state_h.shape[2]
    past_len = page_table.shape[1] * PAGE_SIZE
    l = 0
    row = lambda a: a.reshape(1, -1)
    wts = dict(
        norm_mix_g=row(norm_mix_g[l]), w_in=w_in[l].astype(BF16), conv_w=conv_w[l], conv_b=conv_b[l],
        wa=_block_diag(gate_a_w[l]).astype(BF16), gate_a_b=gate_a_b[l],
        wx=_block_diag(gate_x_w[l]).astype(BF16), gate_x_b=gate_x_b[l], lru_lambda=lru_lambda[l],
        attn_out_g=row(attn_out_g[l]), lru_out_g=row(lru_out_g[l]), w_out=w_out[l].astype(BF16),
        norm_ffn_g=row(norm_ffn_g[l]), router_w_t=router_w[l].T,
        router_bias=router_bias[l].reshape(-1, 1), final_g=row(final_g),
        shared_w_gu=shared_w_gu[l].astype(BF16), shared_w_down=shared_w_down[l].astype(BF16))

    n_c = n_prm + n_dec
    c_all = jnp.concatenate([c_prompt, c_sample, jnp.zeros((-n_c % SUBLANES, d), F32)], axis=0)
    mod = _ada_mod(c_all, ada_w[l], ada_b[l])
    mods_p = [m[:n_prm, None, :] for m in jnp.split(mod, 6, axis=-1)]
    mods_s = [jnp.repeat(m[n_prm:n_c], dec_seq, axis=0)[None] for m in jnp.split(mod, 6, axis=-1)]

    tb = MOE_BLOCK_TOKENS
    nt = n_dec * dec_seq
    assert seq % tb == 0 and nt <= tb
    blk_s = n_prm * (seq // tb)
    slab_shape = (blk_s + 1, tb * SUBLANES, LANES)
    x1p, h2p, route_p, slab, k1, v1, h1, c1 = _mixer(
        x_prompt, mods_p, _moba_prompt, jnp.zeros((n_prm, lw), F32),
        jnp.zeros((n_prm, CONV_W - 1, lw), F32), 0, 512, wts,
        (slab_shape, tb, 0, jnp.zeros(slab_shape, F32)))

    ck = jnp.transpose(cache_k[l], (0, 2, 3, 1)).reshape(cache_k.shape[1], aw, PAGE_SIZE)
    cv = jnp.transpose(cache_v[l], (0, 2, 3, 1)).reshape(cache_v.shape[1], aw, PAGE_SIZE)

    def attend_sample(q, k, v):
        r = lambda a: a.reshape(n_dec, dec_seq, aw)
        return _moba_sample(r(q), r(k), r(v), ck, cv, page_table).reshape(1, n_dec * dec_seq, aw)

    xs = x_sample.reshape(1, n_dec * dec_seq, d)
    shift_m, scale_m, gate_m, shift_f, scale_f, gate_f = mods_s
    cos, sin = _rope_tables(past_len, dec_seq)
    cos, sin = jnp.tile(cos, (n_dec, 1)), jnp.tile(sin, (n_dec, 1))
    nt = n_dec * dec_seq
    q, k2, v2, xl, gl = _inproj(xs, shift_m, scale_m, cos, sin, wts["norm_mix_g"], wts["w_in"], nt)
    attn = attend_sample(q, k2, v2)
    lru, h2s, c2 = _lru(xl.reshape(n_dec, dec_seq, lw), gl.reshape(n_dec, dec_seq, lw), state_h[l],
                        state_conv[l], wts["conv_w"], wts["conv_b"], wts["wa"], wts["gate_a_b"],
                        wts["wx"], wts["gate_x_b"], wts["lru_lambda"], dec_seq, past_len)
    x1s, h2s_, route_s, slab = _outproj(
        attn, lru.reshape(1, nt, lw), xs, gate_m, shift_f, scale_f, wts["attn_out_g"],
        wts["lru_out_g"], wts["w_out"], wts["norm_ffn_g"], wts["router_w_t"], wts["router_bias"],
        nt, slab_shape, tb, blk_s, slab)

    n_exp = exp_w_gu.shape[1]
    tok, gw, grp, row0, eloc, nval, lp = _moe_lists(
        [route_p.reshape(-1, LANES), route_s.reshape(-1, LANES)], tb, n_exp)
    moe_slab = _moe_sparse(slab, tok, gw, grp, row0, eloc, nval, lp,
                           exp_w_gu[l].astype(BF16), exp_w_down[l].astype(BF16))
    fin = lambda x1, h2, blk0, gf, tm: _final(x1, h2, moe_slab, tb, blk0, gf, wts["final_g"],
                                              wts["shared_w_gu"], wts["shared_w_down"], tm)
    yp = fin(x1p, h2p, 0, mods_p[5], 512)
    ys = fin(x1s, h2s_, blk_s, gate_f, nt)

    return (yp, ys.reshape(n_dec, dec_seq, d),
            k1.reshape(1, n_prm, seq, n_heads, HEAD_DIM), v1.reshape(1, n_prm, seq, n_heads, HEAD_DIM),
            h1.reshape(1, n_prm, lw), c1[None],
            k2.reshape(1, n_dec, dec_seq, n_heads, HEAD_DIM), v2.reshape(1, n_dec, dec_seq, n_heads, HEAD_DIM),
            h2s.reshape(1, n_dec, lw), c2[None])
```

```python
import functools
import math

import jax
import jax.numpy as jnp
from jax import lax
from jax.experimental import pallas as pl
from jax.experimental.pallas import tpu as pltpu

F32 = jnp.float32
BF16 = jnp.bfloat16

HEAD_DIM = 64
LRU_BLOCKS = 8
CONV_W = 4
LRU_C = 8.0
MOBA_BLOCK = 256
MOBA_TOPK = 3
ROPE_THETA = 10000.0
N_GROUPS = 8
TOPK_GROUPS = 4
TOP_K = 8
ROUTED_SCALE = 2.5
EPS = 1e-6
PAGE_SIZE = 128

LANES = 128
SUBLANES = 8
NEG = -1e30
VMEM_LIMIT = 56 * 1024 * 1024

NN = (((1,), (0,)), ((), ()))
NT = (((1,), (1,)), ((), ()))


def _cparams(sem):
    return pltpu.CompilerParams(dimension_semantics=sem, vmem_limit_bytes=VMEM_LIMIT)


def _split(x):
    hi = x.astype(BF16)
    lo = (x - hi.astype(F32)).astype(BF16)
    return hi, lo


def _dot3(a, b, dims):
    ah, al = _split(a)
    bh, bl = _split(b)
    d = lambda u, v: lax.dot_general(u, v, dims, preferred_element_type=F32)
    return d(ah, bh) + (d(ah, bl) + d(al, bh))


def _dot(a, b, dims=NN):
    return lax.dot_general(a, b, dims, preferred_element_type=F32)


def _rms(x, g):
    return x * lax.rsqrt(jnp.mean(x * x, axis=-1, keepdims=True) + EPS) * g


def _silu(x):
    return x * jax.nn.sigmoid(x)


def _gelu_tanh(x):
    c = math.sqrt(2.0 / math.pi)
    return 0.5 * x * (1.0 + jnp.tanh(c * (x + 0.044715 * (x * x * x))))


def _ada_kernel(c_ref, w_ref, b_ref, o_ref):
    o_ref[...] = _dot3(_silu(c_ref[...]), w_ref[...], NN) + b_ref[...]


def _ada_mod(c, ada_w, ada_b):
    r, d = c.shape
    n = ada_w.shape[1]
    tn = 1536
    return pl.pallas_call(
        _ada_kernel,
        grid=(n // tn,),
        in_specs=[pl.BlockSpec((r, d), lambda j: (0, 0)),
                  pl.BlockSpec((d, tn), lambda j: (0, j)),
                  pl.BlockSpec((1, tn), lambda j: (0, j))],
        out_specs=pl.BlockSpec((r, tn), lambda j: (0, j)),
        out_shape=jax.ShapeDtypeStruct((r, n), F32),
        compiler_params=_cparams(("parallel",)),
        name="ada_mod",
    )(c, ada_w, ada_b.reshape(1, n))


def _inproj_kernel(x_ref, shift_ref, scale_ref, cos_ref, sin_ref, g_ref, w_ref,
                   q_ref, k_ref, v_ref, xl_ref, gl_ref, *, aw):
    x = x_ref[0]
    h = _rms(x, g_ref[...]) * (1.0 + scale_ref[0]) + shift_ref[0]
    proj = _dot(h.astype(BF16), w_ref[...])
    cos = cos_ref[...]
    sin = sin_ref[...]
    lane = lax.broadcasted_iota(jnp.int32, cos.shape, 1)
    first = (lane % HEAD_DIM) < (HEAD_DIM // 2)

    def rope(t):
        outs = []
        for c in range(aw // LANES):
            tc = t[:, LANES * c:LANES * (c + 1)]
            partner = jnp.where(first, pltpu.roll(tc, LANES - HEAD_DIM // 2, 1),
                                pltpu.roll(tc, HEAD_DIM // 2, 1))
            outs.append(tc * cos + partner * sin)
        return jnp.concatenate(outs, axis=1)

    q_ref[0] = rope(proj[:, 0:aw])
    k_ref[0] = rope(proj[:, aw:2 * aw])
    v_ref[0] = proj[:, 2 * aw:3 * aw]
    lw = (proj.shape[1] - 3 * aw) // 2
    xl_ref[0] = proj[:, 3 * aw:3 * aw + lw]
    gl_ref[0] = proj[:, 3 * aw + lw:]


def _mod_spec(mod, ts):
    if mod.shape[1] == 1:
        return pl.BlockSpec((1, 1, mod.shape[2]), lambda b, i: (b, 0, 0))
    return pl.BlockSpec((1, ts, mod.shape[2]), lambda b, i: (b, i, 0))


def _inproj(x, shift, scale, cos, sin, g, w_bf, ts):
    nb, s, d = x.shape
    aw = 512
    lw = (w_bf.shape[1] - 3 * aw) // 2
    tok = lambda width: pl.BlockSpec((1, ts, width), lambda b, i: (b, i, 0))
    shp = lambda width: jax.ShapeDtypeStruct((nb, s, width), F32)
    return pl.pallas_call(
        functools.partial(_inproj_kernel, aw=aw),
        grid=(nb, s // ts),
        in_specs=[tok(d), _mod_spec(shift, ts), _mod_spec(scale, ts),
                  pl.BlockSpec((ts, LANES), lambda b, i: (i, 0)),
                  pl.BlockSpec((ts, LANES), lambda b, i: (i, 0)),
                  pl.BlockSpec((1, d), lambda b, i: (0, 0)),
                  pl.BlockSpec(w_bf.shape, lambda b, i: (0, 0))],
        out_specs=[tok(aw), tok(aw), tok(aw), tok(lw), tok(lw)],
        out_shape=[shp(aw), shp(aw), shp(aw), shp(lw), shp(lw)],
        compiler_params=_cparams(("parallel", "parallel")),
        name="inproj",
    )(x, shift, scale, cos, sin, g, w_bf)


def _moba_prompt_kernel(q_ref, k_ref, v_ref, o_ref, kbf_ref, vst_ref, km_ref, selb_ref,
                        sa_ref, sb_ref):
    s_len = k_ref.shape[1]
    tq = MOBA_BLOCK
    nblk = s_len // MOBA_BLOCK
    half = lax.broadcasted_iota(jnp.int32, (LANES, tq), 0) < HEAD_DIM
    kbf_ref[...] = k_ref[0].astype(BF16)
    km_ref[...] = jnp.zeros(km_ref.shape, F32)
    for n in range(nblk):
        rows = slice(n * MOBA_BLOCK, (n + 1) * MOBA_BLOCK)
        km_ref[n:n + 1, :] = jnp.mean(k_ref[0, rows, :], axis=0, keepdims=True)
        vt = v_ref[0, rows, :].T
        vst_ref[n] = jnp.concatenate([jnp.where(half, vt, 0.0), jnp.where(half, 0.0, vt)],
                                     axis=1).astype(BF16)
    nrow = km_ref.shape[0]
    n_iota = lax.broadcasted_iota(jnp.int32, (nrow, 2 * tq), 0)
    keyi = lax.broadcasted_iota(jnp.int32, (MOBA_BLOCK, 2 * tq), 0)
    qi = lax.broadcasted_iota(jnp.int32, (MOBA_BLOCK, 2 * tq), 1) % tq

    def stack_heads(pb):
        return jnp.concatenate([pb[:, :tq], pb[:, tq:]], axis=0)

    def per_head(rowvec):
        return jnp.where(half, rowvec[:, :tq], rowvec[:, tq:])

    def qtile(t, carry):
        r0 = pl.multiple_of(t * tq, tq)
        qt = q_ref[0, pl.ds(r0, tq), :].T
        qt2 = jnp.concatenate([jnp.where(half, qt, 0.0), jnp.where(half, 0.0, qt)], axis=1)
        gm = jnp.where(n_iota < t, _dot3(km_ref[...], qt2, NN), -jnp.inf)
        rank = jnp.zeros(gm.shape, jnp.int32)
        for m_ in range(nblk):
            row = gm[m_:m_ + 1, :]
            beats = (row > gm) | ((row == gm) & (m_ < n_iota))
            rank = rank + beats.astype(jnp.int32)
        selb_ref[...] = jnp.where((n_iota < t) & (rank < MOBA_TOPK), 0.0, NEG)
        qb = (qt2 * (HEAD_DIM ** -0.5)).astype(BF16)
        st = jnp.where(keyi <= qi, _dot(kbf_ref[pl.ds(r0, MOBA_BLOCK), :], qb), NEG)
        m = jnp.max(st, axis=0, keepdims=True)
        p = jnp.exp(st - m)
        l = jnp.sum(p, axis=0, keepdims=True)
        acc = _dot(vst_ref[t], stack_heads(p.astype(BF16)))

        def scores(j, dst):
            j = jnp.minimum(j, nblk - 1)
            j0 = pl.multiple_of(j * MOBA_BLOCK, MOBA_BLOCK)
            dst[...] = _dot(kbf_ref[pl.ds(j0, MOBA_BLOCK), :], qb) + selb_ref[pl.ds(j, 1), :]

        def attend(src, j, m, l, acc):
            st = src[...]
            m_new = jnp.maximum(m, jnp.max(st, axis=0, keepdims=True))
            alpha = jnp.exp(m - m_new)
            p = jnp.exp(st - m_new)
            l = alpha * l + jnp.sum(p, axis=0, keepdims=True)
            acc = per_head(alpha) * acc + _dot(vst_ref[j], stack_heads(p.astype(BF16)))
            return m_new, l, acc

        scores(0, sa_ref)

        def body(i, c):
            scores(2 * i + 1, sb_ref)
            c = attend(sa_ref, 2 * i, *c)
            scores(2 * i + 2, sa_ref)
            return attend(sb_ref, jnp.minimum(2 * i + 1, nblk - 1), *c)

        m, l, acc = lax.fori_loop(0, (t + 1) // 2, body, (m, l, acc))
        o_ref[0, pl.ds(r0, tq), :] = (acc / per_head(l)).T
        return carry

    lax.fori_loop(0, s_len // tq, qtile, 0)


def _moba_prompt(q, k, v):
    nb, s, aw = q.shape
    nblk = s // MOBA_BLOCK
    nrow = -(-nblk // 16) * 16
    spec = pl.BlockSpec((1, s, LANES), lambda b, h: (b, 0, h))
    return pl.pallas_call(
        _moba_prompt_kernel,
        grid=(nb, aw // LANES),
        in_specs=[spec, spec, spec],
        out_specs=spec,
        out_shape=jax.ShapeDtypeStruct((nb, s, aw), F32),
        scratch_shapes=[pltpu.VMEM((s, LANES), BF16),
                        pltpu.VMEM((nblk, LANES, 2 * MOBA_BLOCK), BF16),
                        pltpu.VMEM((nrow, LANES), F32),
                        pltpu.VMEM((nrow, 2 * MOBA_BLOCK), F32),
                        pltpu.VMEM((MOBA_BLOCK, 2 * MOBA_BLOCK), F32),
                        pltpu.VMEM((MOBA_BLOCK, 2 * MOBA_BLOCK), F32)],
        compiler_params=_cparams(("parallel", "parallel")),
        name="moba_prompt",
    )(q, k, v)


def _moba_sample_kernel(pt_ref, q_ref, kn_ref, vn_ref, *rest, n_heads, n_q, bps):
    ppb = MOBA_BLOCK // PAGE_SIZE
    npg = bps * ppb
    k_refs, v_refs = rest[:npg], rest[npg:2 * npg]
    o_ref, qbd_ref, km_ref, m_ref, l_ref, o_scr = rest[2 * npg:]
    step = pl.program_id(1)
    nstep = pl.num_programs(1)
    rows = n_heads * n_q
    aw = q_ref.shape[2]
    rowi = lax.broadcasted_iota(jnp.int32, (rows, aw), 0)
    coli = lax.broadcasted_iota(jnp.int32, (rows, aw), 1)
    diag = (rowi // n_q) == (coli // HEAD_DIM)
    lane = lax.broadcasted_iota(jnp.int32, (rows, LANES), 1)

    @pl.when(step == 0)
    def _():
        qt = jnp.concatenate([q_ref[0]] * n_heads, axis=0)
        qbd_ref[...] = jnp.where(diag, qt, 0.0)
        km_ref[...] = jnp.zeros(km_ref.shape, F32)
        m_ref[...] = jnp.full(m_ref.shape, NEG, F32)
        l_ref[...] = jnp.zeros(l_ref.shape, F32)

    qbd = qbd_ref[...]
    qb = (qbd * (HEAD_DIM ** -0.5)).astype(BF16)
    klane = lax.broadcasted_iota(jnp.int32, km_ref.shape, 1)
    km, mm_, ll_ = km_ref[...], m_ref[...], l_ref[...]
    blocks = range(bps)
    kts = [jnp.concatenate([r[0] for r in k_refs[c * ppb:(c + 1) * ppb]], axis=1) for c in blocks]
    ss = [_dot(qb, kt.astype(BF16)) for kt in kts]
    ms = [jnp.max(s, axis=1, keepdims=True) for s in ss]
    ps = [jnp.exp(s - mj) for s, mj in zip(ss, ms)]
    ls = [jnp.sum(p, axis=1, keepdims=True) for p in ps]
    for c in blocks:
        j = step * bps + c
        vt = jnp.concatenate([r[0] for r in v_refs[c * ppb:(c + 1) * ppb]], axis=1)
        o_scr[j] = _dot(ps[c].astype(BF16), vt.astype(BF16), NT)
        km = jnp.where(klane == j, jnp.mean(kts[c], axis=1, keepdims=True), km)
        mm_ = jnp.where(lane == j, ms[c], mm_)
        ll_ = jnp.where(lane == j, ls[c], ll_)
    km_ref[...], m_ref[...], l_ref[...] = km, mm_, ll_

    @pl.when(step == nstep - 1)
    def _():
        npast = o_scr.shape[0]
        gate = _dot3(qbd, km_ref[...], NN)
        valid = lane < npast
        gm = jnp.where(valid, gate, -jnp.inf)
        rank = jnp.zeros((rows, LANES), jnp.int32)
        for mm in range(npast):
            colv = gm[:, mm:mm + 1]
            beats = (colv > gm) | ((colv == gm) & (mm < lane))
            rank = rank + beats.astype(jnp.int32)
        sel = valid & (rank < MOBA_TOPK)
        so = _dot(qb, kn_ref[0].astype(BF16), NT)
        qi = lax.broadcasted_iota(jnp.int32, (rows, n_q), 0) % n_q
        ki = lax.broadcasted_iota(jnp.int32, (rows, n_q), 1)
        so = jnp.where(ki <= qi, so, NEG)
        mo = jnp.max(so, axis=1, keepdims=True)
        po = jnp.exp(so - mo)
        lo = jnp.sum(po, axis=1, keepdims=True)
        oo = _dot(po.astype(BF16), vn_ref[0].astype(BF16))
        mall = jnp.where(sel, m_ref[...], NEG)
        mx = jnp.maximum(jnp.max(mall, axis=1, keepdims=True), mo)
        w = jnp.where(sel, jnp.exp(mall - mx), 0.0)
        wo = jnp.exp(mo - mx)
        den = jnp.sum(w * l_ref[...], axis=1, keepdims=True) + wo * lo
        num = wo * oo
        for jj in range(npast):
            num = num + w[:, jj:jj + 1] * o_scr[jj]
        res = jnp.where(diag, num / den, 0.0)
        out = res[0:n_q, :]
        for hh in range(1, n_heads):
            out = out + res[hh * n_q:(hh + 1) * n_q, :]
        o_ref[0] = out


def _moba_sample(q, k_new, v_new, cache_kt, cache_vt, page_table):
    nb, n_q, aw = q.shape
    n_heads = aw // HEAD_DIM
    n_pages = page_table.shape[1]
    ppb = MOBA_BLOCK // PAGE_SIZE
    npast = n_pages // ppb
    bps = next(n for n in (8, 4, 2, 1) if npast % n == 0)
    npg = bps * ppb
    rows = n_heads * n_q
    tok = pl.BlockSpec((1, n_q, aw), lambda b, j, pt: (b, 0, 0))
    page = lambda o: pl.BlockSpec((1, aw, PAGE_SIZE), lambda b, j, pt: (pt[b, npg * j + o], 0, 0))
    pages = [page(o) for o in range(npg)]
    grid_spec = pltpu.PrefetchScalarGridSpec(
        num_scalar_prefetch=1,
        grid=(nb, npast // bps),
        in_specs=[tok, tok, tok] + pages + pages,
        out_specs=tok,
        scratch_shapes=[pltpu.VMEM((rows, aw), F32), pltpu.VMEM((aw, LANES), F32),
                        pltpu.VMEM((rows, LANES), F32), pltpu.VMEM((rows, LANES), F32),
                        pltpu.VMEM((npast, rows, aw), F32)],
    )
    return pl.pallas_call(
        functools.partial(_moba_sample_kernel, n_heads=n_heads, n_q=n_q, bps=bps),
        grid_spec=grid_spec,
        out_shape=jax.ShapeDtypeStruct((nb, n_q, aw), F32),
        compiler_params=_cparams(("parallel", "arbitrary")),
        name="moba_sample",
    )(page_table, q, k_new, v_new, *([cache_kt] * npg), *([cache_vt] * npg))


def _lru_kernel(xl_ref, gl_ref, h0_ref, c0_ref, cw_ref, cb_ref, wa_ref, ba_ref, wx_ref, bx_ref,
                lam_ref, y_ref, hlast_ref, cbuf_ref, xbuf_ref, a_ref, b_ref, hcar_ref,
                *, ts, start_pos):
    i = pl.program_id(1)
    w = xl_ref.shape[2]
    hist = SUBLANES

    @pl.when(i == 0)
    def _():
        xbuf_ref[0:hist, :] = jnp.zeros((hist, w), F32)
        xbuf_ref[hist - (CONV_W - 1):hist, :] = c0_ref[0]
        hcar_ref[...] = jnp.broadcast_to(h0_ref[0], (SUBLANES, w))

    x = xl_ref[0]
    xbuf_ref[hist:hist + ts, :] = x
    conv = cb_ref[...] + cw_ref[0:1, :] * xbuf_ref[hist - 3:hist - 3 + ts, :]
    conv = conv + cw_ref[1:2, :] * xbuf_ref[hist - 2:hist - 2 + ts, :]
    conv = conv + cw_ref[2:3, :] * xbuf_ref[hist - 1:hist - 1 + ts, :]
    conv = conv + cw_ref[3:4, :] * x
    cbuf_ref[0] = xbuf_ref[hist + ts - (CONV_W - 1):hist + ts, :]
    xbuf_ref[0:hist, :] = xbuf_ref[ts:ts + hist, :]

    cbf = conv.astype(BF16)
    r = jax.nn.sigmoid(_dot(cbf, wa_ref[...]) + ba_ref[...])
    ig = jax.nn.sigmoid(_dot(cbf, wx_ref[...]) + bx_ref[...])
    nl = -lam_ref[...]
    softplus = jnp.maximum(nl, 0.0) + jnp.log1p(jnp.exp(-jnp.abs(nl)))
    log_a = (-LRU_C * r) * softplus
    a = jnp.exp(log_a)
    mult = jnp.sqrt(1.0 - jnp.exp(2.0 * log_a))
    if start_pos == 0:
        reset = (i * ts + lax.broadcasted_iota(jnp.int32, (ts, w), 0)) == 0
        a = jnp.where(reset, 0.0, a)
        mult = jnp.where(reset, 1.0, mult)
    a_ref[...] = a
    b_ref[...] = mult * ig * conv

    rowi = lax.broadcasted_iota(jnp.int32, (SUBLANES, w), 0)

    def group(gi, h):
        o = pl.multiple_of(gi * SUBLANES, SUBLANES)
        a8 = a_ref[pl.ds(o, SUBLANES), :]
        b8 = b_ref[pl.ds(o, SUBLANES), :]
        for d in (1, 2, 4):
            keep = rowi >= d
            a_sh = jnp.where(keep, pltpu.roll(a8, d, 0), 1.0)
            b_sh = jnp.where(keep, pltpu.roll(b8, d, 0), 0.0)
            b8 = a8 * b_sh + b8
            a8 = a8 * a_sh
        h8 = a8 * h + b8
        b_ref[pl.ds(o, SUBLANES), :] = h8
        return jnp.broadcast_to(h8[SUBLANES - 1:SUBLANES, :], (SUBLANES, w))

    h = lax.fori_loop(0, ts // SUBLANES, group, hcar_ref[...])
    hcar_ref[...] = h
    y_ref[0] = b_ref[...] * _gelu_tanh(gl_ref[0])
    hlast_ref[0] = h[0:1, :]


def _block_diag(wb):
    nb, d, _ = wb.shape
    eye = jnp.eye(nb, dtype=wb.dtype)
    return (wb[:, :, None, :] * eye[:, None, :, None]).reshape(nb * d, nb * d)


def _lru(xl, gl, h0, conv0, conv_w, conv_b, wa_bd, ba, wx_bd, bx, lam, ts, start_pos):
    nb, s, w = xl.shape
    tok = pl.BlockSpec((1, ts, w), lambda b, i: (b, i, 0))
    row = pl.BlockSpec((1, w), lambda b, i: (0, 0))
    mat = pl.BlockSpec((w, w), lambda b, i: (0, 0))
    return pl.pallas_call(
        functools.partial(_lru_kernel, ts=ts, start_pos=start_pos),
        grid=(nb, s // ts),
        in_specs=[tok, tok,
                  pl.BlockSpec((1, 1, w), lambda b, i: (b, 0, 0)),
                  pl.BlockSpec((1, CONV_W - 1, w), lambda b, i: (b, 0, 0)),
                  pl.BlockSpec((CONV_W, w), lambda b, i: (0, 0)),
                  row, mat, row, mat, row, row],
        out_specs=[tok,
                   pl.BlockSpec((1, 1, w), lambda b, i: (b, 0, 0)),
                   pl.BlockSpec((1, CONV_W - 1, w), lambda b, i: (b, 0, 0))],
        out_shape=[jax.ShapeDtypeStruct((nb, s, w), F32),
                   jax.ShapeDtypeStruct((nb, 1, w), F32),
                   jax.ShapeDtypeStruct((nb, CONV_W - 1, w), F32)],
        scratch_shapes=[pltpu.VMEM((ts + SUBLANES, w), F32), pltpu.VMEM((ts, w), F32),
                        pltpu.VMEM((ts, w), F32), pltpu.VMEM((SUBLANES, w), F32)],
        compiler_params=_cparams(("parallel", "arbitrary")),
        name="rg_lru",
    )(xl, gl, h0.reshape(nb, 1, w), conv0, conv_w, conv_b.reshape(1, w), wa_bd,
      ba.reshape(1, w), wx_bd, bx.reshape(1, w), lam.reshape(1, w))


def _route(logits_t, bias_col, n_exp, ts):
    scores = jax.nn.sigmoid(logits_t)
    biased = scores + bias_col
    gsz = n_exp // N_GROUPS
    gscore = []
    for g in range(N_GROUPS):
        blk = biased[g * gsz:(g + 1) * gsz, :]
        m1 = jnp.max(blk, axis=0, keepdims=True)
        cnt = jnp.sum((blk == m1).astype(F32), axis=0, keepdims=True)
        m2 = jnp.max(jnp.where(blk < m1, blk, -jnp.inf), axis=0, keepdims=True)
        gscore.append(m1 + jnp.where(cnt >= 2.0, m1, m2))
    cands = []
    for g in range(N_GROUPS):
        rank = jnp.zeros((1, ts), jnp.int32)
        for g2 in range(N_GROUPS):
            if g2 == g:
                continue
            beats = (gscore[g2] > gscore[g]) if g2 > g else (gscore[g2] >= gscore[g])
            rank = rank + beats.astype(jnp.int32)
        blk = biased[g * gsz:(g + 1) * gsz, :]
        cands.append(jnp.where(rank < TOPK_GROUPS, blk, -jnp.inf))
    cand = jnp.concatenate(cands, axis=0)
    e_f = lax.broadcasted_iota(jnp.int32, (n_exp, ts), 0).astype(F32)
    ids, sel = [], []
    for _ in range(TOP_K):
        top = jnp.max(cand, axis=0, keepdims=True)
        idx = jnp.min(jnp.where(cand == top, e_f, float(n_exp)), axis=0, keepdims=True)
        hit = e_f == idx
        ids.append(idx)
        sel.append(jnp.sum(jnp.where(hit, scores, 0.0), axis=0, keepdims=True))
        cand = jnp.where(hit, -jnp.inf, cand)
    total = functools.reduce(lambda a, c: a + c, sel)
    wts = [ROUTED_SCALE * w / total for w in sel]
    return jnp.concatenate(ids + wts, axis=0)


def _rows_to_slabs(x, cv_ref, slab_ref):
    n = x.shape[0]
    s = n + 1
    for j in range(SUBLANES):
        cv_ref[pl.ds(j * s, n), :] = x[:, j * LANES:(j + 1) * LANES]
    for r in range(n):
        slab_ref[pl.ds(SUBLANES * r, SUBLANES), :] = cv_ref[pl.ds(r, SUBLANES, stride=s), :]


def _slabs_to_rows(slab_ref, cv_ref, n):
    s = n + 1
    for r in range(n):
        cv_ref[pl.ds(r, SUBLANES, stride=s), :] = slab_ref[pl.ds(SUBLANES * r, SUBLANES), :]
    return jnp.concatenate([cv_ref[pl.ds(j * s, n), :] for j in range(SUBLANES)], axis=1)


def _slab_spec(s, tile, tb, blk0):
    per_seq = -(-s // tb)
    tiles = min(tb, s) // tile
    return pl.BlockSpec((None, tile * SUBLANES, LANES),
                        lambda b, i: (blk0 + b * per_seq + i // tiles, i % tiles, 0))


def _outproj_kernel(attn_ref, lru_ref, x_ref, gm_ref, sf_ref, cf_ref, ag_ref, lg_ref, wo_ref,
                    ng_ref, rw_ref, rb_ref, *rest, ts):
    x1_ref, h2_ref, gates_ref, hs_ref, cv_ref = rest[-5:]
    aw = attn_ref.shape[2]
    an = _rms(attn_ref[0], ag_ref[...]).astype(BF16)
    ln = _rms(lru_ref[0], lg_ref[...]).astype(BF16)
    mixed = _dot(an, wo_ref[0:aw, :]) + _dot(ln, wo_ref[aw:, :])
    x1 = x_ref[0] + gm_ref[0] * mixed
    x1_ref[0] = x1
    h2 = _rms(x1, ng_ref[...]) * (1.0 + cf_ref[0]) + sf_ref[0]
    h2_ref[0] = h2.astype(BF16)
    _rows_to_slabs(h2, cv_ref, hs_ref)
    n_exp = rw_ref.shape[0]
    route_t = _route(_dot3(rw_ref[...], h2, NT), rb_ref[...], n_exp, ts)
    route_t = jnp.concatenate([route_t, jnp.zeros((LANES - 2 * TOP_K, ts), F32)], axis=0)
    gates_ref[0] = route_t.T


def _outproj(attn, lru, x, gate_m, shift_f, scale_f, ag, lg, wo_bf, ng, rw_t, rb_col, ts,
             slab_shape, tb, blk0, slab_in):
    nb, s, d = x.shape
    aw = attn.shape[2]
    tok = lambda width: pl.BlockSpec((1, ts, width), lambda b, i: (b, i, 0))
    const = lambda a: pl.BlockSpec(a.shape, lambda b, i: (0,) * a.ndim)
    args = [attn, lru, x, gate_m, shift_f, scale_f, ag, lg, wo_bf, ng, rw_t, rb_col]
    in_specs = [tok(aw), tok(lru.shape[2]), tok(d), _mod_spec(gate_m, ts), _mod_spec(shift_f, ts),
                _mod_spec(scale_f, ts), const(ag), const(lg), const(wo_bf), const(ng),
                const(rw_t), const(rb_col)]
    aliases = {}
    if slab_in is not None:
        aliases = {len(args): 3}
        args.append(slab_in)
        in_specs.append(pl.BlockSpec(memory_space=pl.ANY))
    return pl.pallas_call(
        functools.partial(_outproj_kernel, ts=ts),
        grid=(nb, s // ts),
        in_specs=in_specs,
        out_specs=[tok(d), tok(d), tok(LANES), _slab_spec(s, ts, tb, blk0)],
        out_shape=[jax.ShapeDtypeStruct((nb, s, d), F32), jax.ShapeDtypeStruct((nb, s, d), BF16),
                   jax.ShapeDtypeStruct((nb, s, LANES), F32),
                   jax.ShapeDtypeStruct(slab_shape, F32)],
        scratch_shapes=[pltpu.VMEM((SUBLANES * (ts + 1), LANES), F32)],
        input_output_aliases=aliases,
        compiler_params=_cparams(("parallel", "parallel")),
        name="outproj_router",
    )(*args)


def _swiglu_bf(h, w_gu, w_down):
    gu = _dot(h, w_gu)
    de = gu.shape[1] // 2
    act = _silu(gu[:, :de]) * gu[:, de:]
    return _dot(act.astype(BF16), w_down)


MOE_TILE = 128
MOE_STRIDE = MOE_TILE + 1
MOE_BATCH = 8
MOE_GROUP = 2
MOE_EXPERTS_PER_STEP = 4
MOE_KEY_SHIFT = 12
SMEM_CHUNK = 1024


def _moe_lists(routes, tb, n_exp):
    assert tb < (1 << MOE_KEY_SHIFT)
    ids, wts = [], []
    for r in routes:
        fill = -r.shape[0] % tb
        i = jnp.pad(r[:, 0:TOP_K].astype(jnp.int32), ((0, fill), (0, 0)), constant_values=n_exp)
        ids.append(i.reshape(-1, tb, TOP_K))
        wts.append(jnp.pad(r[:, TOP_K:2 * TOP_K], ((0, fill), (0, 0))).reshape(-1, tb * TOP_K))
    ids, wts = jnp.concatenate(ids, axis=0), jnp.concatenate(wts, axis=0)
    n_blocks = ids.shape[0]
    tloc = jnp.arange(tb, dtype=jnp.int32)[None, :, None]
    keys = ((ids << MOE_KEY_SHIFT) | tloc).reshape(n_blocks, tb * TOP_K)
    keys, wts = lax.sort((keys, wts), dimension=1, num_keys=1, is_stable=False)
    n = keys.shape[1]
    lp = -(-(n + MOE_TILE) // SMEM_CHUNK) * SMEM_CHUNK
    flat = lambda a: jnp.pad(a, ((0, 0), (0, lp - n))).reshape(-1)
    tok = flat((keys & ((1 << MOE_KEY_SHIFT) - 1)) * SUBLANES)
    cnt = jnp.sum((ids[..., None] == jnp.arange(n_exp, dtype=jnp.int32)).astype(jnp.int32),
                  axis=(1, 2))
    start = jnp.cumsum(cnt, axis=1) - cnt
    ntile = (cnt + MOE_TILE - 1) // MOE_TILE
    tend = jnp.cumsum(ntile, axis=1)
    tfirst = tend - ntile
    nt_max = (tb * TOP_K + n_exp * (MOE_TILE - 1)) // MOE_TILE
    ti = jnp.arange(nt_max, dtype=jnp.int32)[None, :]
    e_i = jnp.minimum(jnp.sum((ti[:, :, None] >= tend[:, None, :]).astype(jnp.int32), axis=2),
                      n_exp - 1)
    done = (ti - jnp.take_along_axis(tfirst, e_i, axis=1)) * MOE_TILE
    row0 = jnp.take_along_axis(start, e_i, axis=1) + done
    nvalid = jnp.clip(jnp.take_along_axis(cnt, e_i, axis=1) - done, 0, MOE_TILE)
    grp = jnp.concatenate([tfirst[:, ::MOE_EXPERTS_PER_STEP], tend[:, -1:]], axis=1)
    i32 = lambda a: a.astype(jnp.int32)
    return (tok, flat(wts), i32(grp), i32(row0), i32(e_i % MOE_EXPERTS_PER_STEP), i32(nvalid), lp)


def _moe_sparse_kernel(grp_ref, row0_ref, eloc_ref, nval_ref, tok_hbm, wts_hbm, x_ref, wgu_ref,
                       wdn_ref, out_ref, tok_smem, wts_smem, xt_ref, yt_ref, sem):
    b, e = pl.program_id(0), pl.program_id(1)
    lp = tok_smem.shape[0]
    m, s = MOE_TILE, MOE_STRIDE
    nch = SUBLANES

    @pl.when(e == 0)
    def _():
        off = pl.multiple_of(b * lp, SMEM_CHUNK)
        copies = [pltpu.make_async_copy(h.at[pl.ds(off, lp)], d, sem.at[k]) for k, (h, d) in
                  enumerate(((tok_hbm, tok_smem), (wts_hbm, wts_smem)))]
        for cp in copies:
            cp.start()
        out_ref[...] = jnp.zeros(out_ref.shape, F32)
        for cp in copies:
            cp.wait()

    def gather(r0, xt):
        for mi in range(m):
            t8 = pl.multiple_of(tok_smem[r0 + mi], SUBLANES)
            xt[pl.ds(mi, SUBLANES, stride=s), :] = x_ref[pl.ds(t8, SUBLANES), :]

    def expert(xt, yt, el, nv):
        x = jnp.concatenate([xt[pl.ds(j * s, m), :] for j in range(nch)], axis=1)
        y = _swiglu_bf(x.astype(BF16), wgu_ref[el], wdn_ref[el])
        y = jnp.where(lax.broadcasted_iota(jnp.int32, (m, 1), 0) < nv, y, 0.0)
        for j in range(nch):
            yt[pl.ds(j * s, m), :] = y[:, j * LANES:(j + 1) * LANES]

    def combine(r0, yt):
        for g0 in range(0, m, MOE_BATCH):
            upd = []
            for mi in range(g0, g0 + MOE_BATCH):
                t8 = pl.multiple_of(tok_smem[r0 + mi], SUBLANES)
                slab = yt[pl.ds(mi, SUBLANES, stride=s), :]
                upd.append((t8, out_ref[pl.ds(t8, SUBLANES), :] + wts_smem[r0 + mi] * slab))
            for t8, v in reversed(upd):
                out_ref[pl.ds(t8, SUBLANES), :] = v

    def tiles(t, n):
        r0s = [row0_ref[b, t + c] for c in range(n)]
        for c in range(n):
            gather(r0s[c], xt_ref.at[c])
        for c in range(n):
            expert(xt_ref.at[c], yt_ref.at[c], eloc_ref[b, t + c], nval_ref[b, t + c])
        for c in range(n):
            combine(r0s[c], yt_ref.at[c])

    t0 = grp_ref[b, e]
    nt = grp_ref[b, e + 1] - t0

    def group(i, carry):
        tiles(t0 + i * MOE_GROUP, MOE_GROUP)
        return carry

    lax.fori_loop(0, nt // MOE_GROUP, group, 0)
    for rem in range(1, MOE_GROUP):
        @pl.when(nt % MOE_GROUP == rem)
        def _():
            tiles(t0 + nt - rem, rem)


def _moe_sparse(x_slab, tok, wts, grp, row0, eloc, nval, lp, w_gu_bf, w_down_bf):
    n_blocks, rows, _ = x_slab.shape
    n_exp = w_gu_bf.shape[0]
    eps = MOE_EXPERTS_PER_STEP
    any_spec = pl.BlockSpec(memory_space=pl.ANY)
    blk = pl.BlockSpec((None, rows, LANES), lambda b, e, *_: (b, 0, 0))
    grid_spec = pltpu.PrefetchScalarGridSpec(
        num_scalar_prefetch=4,
        grid=(n_blocks, n_exp // eps),
        in_specs=[any_spec, any_spec, blk,
                  pl.BlockSpec((eps,) + w_gu_bf.shape[1:], lambda b, e, *_: (e, 0, 0)),
                  pl.BlockSpec((eps,) + w_down_bf.shape[1:], lambda b, e, *_: (e, 0, 0))],
        out_specs=blk,
        scratch_shapes=[pltpu.SMEM((lp,), jnp.int32), pltpu.SMEM((lp,), F32),
                        pltpu.VMEM((MOE_GROUP, SUBLANES * MOE_STRIDE, LANES), F32),
                        pltpu.VMEM((MOE_GROUP, SUBLANES * MOE_STRIDE, LANES), F32),
                        pltpu.SemaphoreType.DMA((2,))],
    )
    return pl.pallas_call(
        _moe_sparse_kernel,
        grid_spec=grid_spec,
        out_shape=jax.ShapeDtypeStruct(x_slab.shape, F32),
        compiler_params=_cparams(("parallel", "arbitrary")),
        name="moe_routed",
    )(grp, row0, eloc, nval, tok, wts, x_slab, w_gu_bf, w_down_bf)


def _final_kernel(x1_ref, h2_ref, moe_ref, gf_ref, fg_ref, sgu_ref, sdn_ref, y_ref, cv_ref):
    moe = _slabs_to_rows(moe_ref, cv_ref, x1_ref.shape[1])
    f = _swiglu_bf(h2_ref[0], sgu_ref[...], sdn_ref[...]) + moe
    y_ref[0] = _rms(x1_ref[0] + gf_ref[0] * f, fg_ref[...])


def _final(x1, h2, moe_slab, tb, blk0, gate_f, final_g, sgu_bf, sdn_bf, tm):
    nb, s, d = x1.shape
    tok = pl.BlockSpec((1, tm, d), lambda b, i: (b, i, 0))
    const = lambda a: pl.BlockSpec(a.shape, lambda b, i: (0,) * a.ndim)
    return pl.pallas_call(
        _final_kernel,
        grid=(nb, s // tm),
        in_specs=[tok, tok, _slab_spec(s, tm, tb, blk0),
                  _mod_spec(gate_f, tm), const(final_g), const(sgu_bf), const(sdn_bf)],
        out_specs=tok,
        out_shape=jax.ShapeDtypeStruct((nb, s, d), F32),
        scratch_shapes=[pltpu.VMEM((SUBLANES * (tm + 1), LANES), F32)],
        compiler_params=_cparams(("parallel", "parallel")),
        name="shared_final",
    )(x1, h2, moe_slab, gate_f, final_g, sgu_bf, sdn_bf)


def _rope_tables(start_pos, s):
    half = HEAD_DIM // 2
    inv_freq = ROPE_THETA ** (-jnp.arange(half, dtype=F32) / half)
    ang = (start_pos + jnp.arange(s, dtype=jnp.int32)).astype(F32)[:, None] * inv_freq[None, :]
    cos, sin = jnp.cos(ang), jnp.sin(ang)
    reps = LANES // HEAD_DIM
    return (jnp.tile(jnp.concatenate([cos, cos], axis=1), (1, reps)),
            jnp.tile(jnp.concatenate([-sin, sin], axis=1), (1, reps)))


def _mixer(x, mods, attend, h0, conv0, start_pos, ts, wts, slab):
    shift_m, scale_m, gate_m, shift_f, scale_f, _ = mods
    s = x.shape[1]
    cos, sin = _rope_tables(start_pos, s)
    q, k, v, xl, gl = _inproj(x, shift_m, scale_m, cos, sin, wts["norm_mix_g"], wts["w_in"], ts)
    attn = attend(q, k, v)
    lru, h_last, cbuf = _lru(xl, gl, h0, conv0, wts["conv_w"], wts["conv_b"], wts["wa"],
                             wts["gate_a_b"], wts["wx"], wts["gate_x_b"], wts["lru_lambda"],
                             ts, start_pos)
    x1, h2, route, hs = _outproj(attn, lru, x, gate_m, shift_f, scale_f, wts["attn_out_g"],
                                 wts["lru_out_g"], wts["w_out"], wts["norm_ffn_g"],
                                 wts["router_w_t"], wts["router_bias"], ts, *slab)
    return x1, h2, route, hs, k, v, h_last, cbuf


MOE_BLOCK_TOKENS = 2048


def kernel(x_prompt, x_sample, c_prompt, c_sample, cache_k, cache_v, state_h, state_conv, page_table, ada_w, ada_b, norm_mix_g, w_in, conv_w, conv_b, gate_a_w, gate_a_b, gate_x_w, gate_x_b, lru_lambda, attn_out_g, lru_out_g, w_out, norm_ffn_g, router_w, router_bias, exp_w_gu, exp_w_down, shared_w_gu, shared_w_down, final_g):
    depth = ada_w.shape[0]
    assert depth == 1, "single-layer trunk"
    n_prm, seq, d = x_prompt.shape
    n_dec, dec_seq, _ = x_sample.shape
    n_heads = cache_k.shape[3]
    aw = n_heads * HEAD_DIM
    lw = state_h.shape[2]
    past_len = page_table.shape[1] * PAGE_SIZE
    l = 0
    row = lambda a: a.reshape(1, -1)
    wts = dict(
        norm_mix_g=row(norm_mix_g[l]), w_in=w_in[l].astype(BF16), conv_w=conv_w[l], conv_b=conv_b[l],
        wa=_block_diag(gate_a_w[l]).astype(BF16), gate_a_b=gate_a_b[l],
        wx=_block_diag(gate_x_w[l]).astype(BF16), gate_x_b=gate_x_b[l], lru_lambda=lru_lambda[l],
        attn_out_g=row(attn_out_g[l]), lru_out_g=row(lru_out_g[l]), w_out=w_out[l].astype(BF16),
        norm_ffn_g=row(norm_ffn_g[l]), router_w_t=router_w[l].T,
        router_bias=router_bias[l].reshape(-1, 1), final_g=row(final_g),
        shared_w_gu=shared_w_gu[l].astype(BF16), shared_w_down=shared_w_down[l].astype(BF16))

    n_c = n_prm + n_dec
    c_all = jnp.concatenate([c_prompt, c_sample, jnp.zeros((-n_c % SUBLANES, d), F32)], axis=0)
    mod = _ada_mod(c_all, ada_w[l], ada_b[l])
    mods_p = [m[:n_prm, None, :] for m in jnp.split(mod, 6, axis=-1)]
    mods_s = [jnp.repeat(m[n_prm:n_c], dec_seq, axis=0)[None] for m in jnp.split(mod, 6, axis=-1)]

    tb = MOE_BLOCK_TOKENS
    nt = n_dec * dec_seq
    assert seq % tb == 0 and nt <= tb
    blk_s = n_prm * (seq // tb)
    slab_shape = (blk_s + 1, tb * SUBLANES, LANES)
    x1p, h2p, route_p, slab, k1, v1, h1, c1 = _mixer(
        x_prompt, mods_p, _moba_prompt, jnp.zeros((n_prm, lw), F32),
        jnp.zeros((n_prm, CONV_W - 1, lw), F32), 0, 512, wts,
        (slab_shape, tb, 0, jnp.zeros(slab_shape, F32)))

    ck = jnp.transpose(cache_k[l], (0, 2, 3, 1)).reshape(cache_k.shape[1], aw, PAGE_SIZE)
    cv = jnp.transpose(cache_v[l], (0, 2, 3, 1)).reshape(cache_v.shape[1], aw, PAGE_SIZE)

    def attend_sample(q, k, v):
        r = lambda a: a.reshape(n_dec, dec_seq, aw)
        return _moba_sample(r(q), r(k), r(v), ck, cv, page_table).reshape(1, n_dec * dec_seq, aw)

    xs = x_sample.reshape(1, n_dec * dec_seq, d)
    shift_m, scale_m, gate_m, shift_f, scale_f, gate_f = mods_s
    cos, sin = _rope_tables(past_len, dec_seq)
    cos, sin = jnp.tile(cos, (n_dec, 1)), jnp.tile(sin, (n_dec, 1))
    nt = n_dec * dec_seq
    q, k2, v2, xl, gl = _inproj(xs, shift_m, scale_m, cos, sin, wts["norm_mix_g"], wts["w_in"], nt)
    attn = attend_sample(q, k2, v2)
    lru, h2s, c2 = _lru(xl.reshape(n_dec, dec_seq, lw), gl.reshape(n_dec, dec_seq, lw), state_h[l],
                        state_conv[l], wts["conv_w"], wts["conv_b"], wts["wa"], wts["gate_a_b"],
                        wts["wx"], wts["gate_x_b"], wts["lru_lambda"], dec_seq, past_len)
    x1s, h2s_, route_s, slab = _outproj(
        attn, lru.reshape(1, nt, lw), xs, gate_m, shift_f, scale_f, wts["attn_out_g"],
        wts["lru_out_g"], wts["w_out"], wts["norm_ffn_g"], wts["router_w_t"], wts["router_bias"],
        nt, slab_shape, tb, blk_s, slab)

    n_exp = exp_w_gu.shape[1]
    tok, gw, grp, row0, eloc, nval, lp = _moe_lists(
        [route_p.reshape(-1, LANES), route_s.reshape(-1, LANES)], tb, n_exp)
    moe_slab = _moe_sparse(slab, tok, gw, grp, row0, eloc, nval, lp,
                           exp_w_gu[l].astype(BF16), exp_w_down[l].astype(BF16))
    fin = lambda x1, h2, blk0, gf, tm: _final(x1, h2, moe_slab, tb, blk0, gf, wts["final_g"],
                                              wts["shared_w_gu"], wts["shared_w_down"], tm)
    yp = fin(x1p, h2p, 0, mods_p[5], 512)
    ys = fin(x1s, h2s_, blk_s, gate_f, nt)

    return (yp, ys.reshape(n_dec, dec_seq, d),
            k1.reshape(1, n_prm, seq, n_heads, HEAD_DIM), v1.reshape(1, n_prm, seq, n_heads, HEAD_DIM),
            h1.reshape(1, n_prm, lw), c1[None],
            k2.reshape(1, n_dec, dec_seq, n_heads, HEAD_DIM), v2.reshape(1, n_dec, dec_seq, n_heads, HEAD_DIM),
            h2s.reshape(1, n_dec, lw), c2[None])
```
